```python
import jax, jax.numpy as jnp
from jax import lax
import numpy as np

D_MODEL = 1024
BATCH = 8
SEQ = 2048
DEPTH = 2
DEC_BATCH = 32
DEC_SEQ = 4
PAST_LEN = 16384
PAGE_SIZE = 128

HEAD_DIM = 64
N_EVEN = (DEPTH + 1) // 2
N_ODD = DEPTH // 2
NSA_HEADS = D_MODEL // (2 * HEAD_DIM)
NSA_KV = 2
NSA_R = NSA_HEADS // NSA_KV
CMP_LEN = 32
CMP_STRIDE = 16
CMP_HIDDEN = 128
SLC_BLOCK = 64
SLC_TOPN = 16
WINDOW = 512
FOX_HEADS = D_MODEL // (2 * HEAD_DIM)
FGT_BIAS_MEAN = 4.0
MOBA_HEADS = D_MODEL // HEAD_DIM
MOBA_KV = 4
MOBA_R = MOBA_HEADS // MOBA_KV
MOBA_BLOCK = 256
MOBA_TOPK = 3
D_FF = 4 * D_MODEL
QBLK = 128
SPARSE_QBLK = 16
RMS_EPS = 1e-6
NEG = -1e30
EVEN_SPLITS = (NSA_HEADS * HEAD_DIM, 2 * NSA_KV * HEAD_DIM, 2 * NSA_KV * HEAD_DIM, 2 * NSA_KV * HEAD_DIM,
               3 * NSA_HEADS, FOX_HEADS * HEAD_DIM, 2 * FOX_HEADS * HEAD_DIM, FOX_HEADS)
EVEN_COLS = sum(EVEN_SPLITS)
ODD_SPLITS = (MOBA_HEADS * HEAD_DIM, 2 * MOBA_KV * HEAD_DIM)
ODD_COLS = sum(ODD_SPLITS)

kernel_name = 'hybrid_nsa_fox_moba_decode_step'


def rmsnorm(x, g):
    xf = x.astype(jnp.float32)
    y = xf * lax.rsqrt(jnp.mean(xf * xf, axis=-1, keepdims=True) + RMS_EPS)
    return (y * g.astype(jnp.float32)).astype(x.dtype)


def masked_softmax(s, mask):
    s = jnp.where(mask, s.astype(jnp.float32), NEG)
    p = jnp.exp(s - jnp.max(s, axis=-1, keepdims=True)) * mask
    return p / jnp.maximum(jnp.sum(p, axis=-1, keepdims=True), 1e-30)


def alibi_slopes(n):
    return np.asarray([2.0 ** (-8.0 * (i + 1) / n) for i in range(n)], dtype=np.float32)


def split_cols(a, sizes):
    return jnp.split(a, [int(i) for i in np.cumsum(sizes)[:-1]], axis=-1)


def sweep(fn, xs, q0, qblk):
    B, Q = xs[0].shape[0], xs[0].shape[1]
    if Q <= qblk:
        return fn(xs, q0)
    n = Q // qblk
    xs_c = tuple(a.reshape(B, n, qblk, *a.shape[2:]).swapaxes(0, 1) for a in xs)
    starts = q0 + qblk * jnp.arange(n, dtype=jnp.int32)
    out = lax.map(lambda a: fn(a[0], a[1]), (xs_c, starts))
    return out.swapaxes(0, 1).reshape(B, Q, *out.shape[3:])


def gather_pages(pool, li, page_table):
    g = pool[li, page_table]
    return g.reshape(g.shape[0], -1, *g.shape[3:])


def nsa_compress(rows, pe, w1, w2):
    B, L, G, D = rows.shape
    n_sub = L // CMP_STRIDE
    r = CMP_LEN // CMP_STRIDE
    n_c = n_sub - r + 1
    sub = rows[:, :n_sub * CMP_STRIDE].reshape(B, n_sub, CMP_STRIDE, G, D)
    blocks = jnp.concatenate([sub[:, j:j + n_c] for j in range(r)], axis=2) + pe[None, None, :, None, :]
    flat = blocks.transpose(0, 1, 3, 2, 4).reshape(B, n_c, G, CMP_LEN * D)
    return jax.nn.gelu(flat @ w1) @ w2


def cmp_to_slc(n_c, n_s):
    c = np.arange(n_c)[:, None]
    j = np.arange(n_s)[None, :]
    lo = c * CMP_STRIDE
    hi = lo + CMP_LEN
    return ((lo < (j + 1) * SLC_BLOCK) & (hi > j * SLC_BLOCK)).astype(np.float32)


def nsa_mixer(q, gates, kvc, kvs, kvw, q0, pe, w1, w2):
    B = q.shape[0]
    L = kvc.shape[1]
    ck = nsa_compress(kvc[:, :, 0], pe[0], w1[0], w2[0])
    cv = nsa_compress(kvc[:, :, 1], pe[1], w1[1], w2[1])
    n_c = ck.shape[1]
    c_end = jnp.arange(n_c, dtype=jnp.int32) * CMP_STRIDE + (CMP_LEN - 1)
    n_s = -(-L // SLC_BLOCK)
    kvs_b = jnp.pad(kvs, ((0, 0), (0, n_s * SLC_BLOCK - L), (0, 0), (0, 0), (0, 0))).reshape(
        B, n_s, SLC_BLOCK, 2, NSA_KV, HEAD_DIM)
    m_cs = jnp.asarray(cmp_to_slc(n_c, n_s))
    n_sel = min(SLC_TOPN, n_s)
    slopes = jnp.asarray(alibi_slopes(NSA_HEADS)).reshape(NSA_KV, NSA_R)
    scale = HEAD_DIM ** -0.5
    bi = jnp.arange(B)[:, None, None, None]
    gi = jnp.arange(NSA_KV)[None, None, :, None]
    blk = jnp.arange(n_s, dtype=jnp.int32)

    def chunk(args, s0):
        qc, gc = args
        Qc = qc.shape[1]
        t = s0 + jnp.arange(Qc, dtype=jnp.int32)
        qg = qc.reshape(B, Qc, NSA_KV, NSA_R, HEAD_DIM)
        dc = t[:, None] - c_end[None, :]
        sc = jnp.einsum('bqgrd,bcgd->bqgrc', qg, ck) * scale \
            - slopes[None, None, :, :, None] * dc[None, :, None, None, :].astype(jnp.float32)
        p_cmp = masked_softmax(sc, (dc >= 0)[None, :, None, None, :])
        o_cmp = jnp.einsum('bqgrc,bcgd->bqgrd', p_cmp.astype(cv.dtype), cv)
        imp = jnp.einsum('bqgrc,cj->bqgj', p_cmp, m_cs)
        tb = (t // SLC_BLOCK)[:, None]
        avail = (blk[None, :] <= tb)[None, :, None, :]
        forced = ((blk[None, :] == 0) | (blk[None, :] == tb) | (blk[None, :] == tb - 1))[None, :, None, :]
        imp = jnp.where(forced, jnp.inf, jnp.where(avail, imp, -jnp.inf))
        top, idx = lax.top_k(imp, n_sel)
        kvg = kvs_b[bi, idx, :, :, gi]
        kp = idx[..., None] * SLC_BLOCK + jnp.arange(SLC_BLOCK, dtype=jnp.int32)
        ds = t[None, :, None, None, None] - kp
        ms = (top > -jnp.inf)[..., None] & (ds >= 0)
        ss = jnp.einsum('bqgrd,bqgnkd->bqgrnk', qg, kvg[..., 0, :]) * scale \
            - slopes[None, None, :, :, None, None] * ds[:, :, :, None].astype(jnp.float32)
        p = masked_softmax(ss.reshape(B, Qc, NSA_KV, NSA_R, -1), ms.reshape(B, Qc, NSA_KV, 1, -1))
        o_slc = jnp.einsum('bqgrnk,bqgnkd->bqgrd', p.reshape(ss.shape).astype(kvg.dtype), kvg[..., 1, :])
        kvwc = lax.dynamic_slice_in_dim(kvw, s0 - q0, WINDOW + Qc, axis=1)
        wp = s0 - WINDOW + jnp.arange(WINDOW + Qc, dtype=jnp.int32)
        dw = t[:, None] - wp[None, :]
        mw = (wp[None, :] >= 0) & (dw >= 0) & (dw < WINDOW)
        sw = jnp.einsum('bqgrd,bkgd->bqgrk', qg, kvwc[:, :, 0]) * scale \
            - slopes[None, None, :, :, None] * dw[None, :, None, None, :].astype(jnp.float32)
        p = masked_softmax(sw, mw[None, :, None, None, :])
        o_win = jnp.einsum('bqgrk,bkgd->bqgrd', p.astype(kvwc.dtype), kvwc[:, :, 1])
        o = jnp.stack([o_cmp, o_slc, o_win], axis=-1).reshape(B, Qc, NSA_HEADS, HEAD_DIM, 3)
        g = jax.nn.sigmoid(gc.astype(jnp.float32))
        return jnp.einsum('bqhdk,bqhk->bqhd', o.astype(jnp.float32), g).astype(qc.dtype)

    return sweep(chunk, (q, gates), q0, SPARSE_QBLK)


def fox_mixer(q, kv, c, q0):
    Q = q.shape[1]
    L = kv.shape[1]
    k, v = kv[:, :, 0], kv[:, :, 1]
    c_k = c.transpose(0, 2, 1)
    c_q = c[:, q0:q0 + Q]
    scale = HEAD_DIM ** -0.5
    kpos = jnp.arange(L, dtype=jnp.int32)

    def chunk(args, s0):
        qc, cq = args
        t = s0 + jnp.arange(qc.shape[1], dtype=jnp.int32)
        s = jnp.einsum('bqhd,bkhd->bhqk', qc, k) * scale \
            + (cq.transpose(0, 2, 1)[..., None] - c_k[:, :, None, :])
        p = masked_softmax(s, (kpos[None, :] <= t[:, None])[None, None])
        return jnp.einsum('bhqk,bkhd->bqhd', p.astype(v.dtype), v)

    return sweep(chunk, (q, c_q), q0, QBLK)


def moba_mixer(q, kv, q0):
    B = q.shape[0]
    L = kv.shape[1]
    n_b = -(-L // MOBA_BLOCK)
    kvp = jnp.pad(kv, ((0, 0), (0, (n_b + 1) * MOBA_BLOCK - L), (0, 0), (0, 0), (0, 0)))
    kvb = kvp.reshape(B, n_b + 1, MOBA_BLOCK, 2, MOBA_KV, HEAD_DIM)
    kmean = jnp.mean(kvb[:, :n_b, :, 0].astype(jnp.float32), axis=2)
    k_eff = min(MOBA_TOPK, n_b)
    slopes = jnp.asarray(alibi_slopes(MOBA_HEADS)).reshape(MOBA_KV, MOBA_R)
    scale = HEAD_DIM ** -0.5
    bi = jnp.arange(B)[:, None, None, None, None]
    gi = jnp.arange(MOBA_KV)[None, None, :, None, None]
    blk = jnp.arange(n_b, dtype=jnp.int32)
    n_g = k_eff * MOBA_BLOCK

    def chunk(args, s0):
        (qc,) = args
        Qc = qc.shape[1]
        t = s0 + jnp.arange(Qc, dtype=jnp.int32)
        tb = t // MOBA_BLOCK
        qg = qc.reshape(B, Qc, MOBA_KV, MOBA_R, HEAD_DIM)
        gate = jnp.einsum('bqgrd,bngd->bqgrn', qg.astype(jnp.float32), kmean)
        gate = jnp.where((blk[None, :] < tb[:, None])[None, :, None, None, :], gate, -jnp.inf)
        top, idx = lax.top_k(gate, k_eff)
        kvg = kvb[bi, idx, :, :, gi]
        kp = idx[..., None] * MOBA_BLOCK + jnp.arange(MOBA_BLOCK, dtype=jnp.int32)
        ds = t[None, :, None, None, None, None] - kp
        ss = jnp.einsum('bqgrd,bqgrnkd->bqgrnk', qg, kvg[..., 0, :]) * scale \
            - slopes[None, None, :, :, None, None] * ds.astype(jnp.float32)
        ms = jnp.broadcast_to((top > -jnp.inf)[..., None], ds.shape)
        start = (s0 // MOBA_BLOCK) * MOBA_BLOCK
        kvo = lax.dynamic_slice_in_dim(kvp, start, MOBA_BLOCK + Qc, axis=1)
        op = start + jnp.arange(MOBA_BLOCK + Qc, dtype=jnp.int32)
        do = t[:, None] - op[None, :]
        mo = (do >= 0) & ((op // MOBA_BLOCK)[None, :] == tb[:, None])
        so = jnp.einsum('bqgrd,bkgd->bqgrk', qg, kvo[:, :, 0]) * scale \
            - slopes[None, None, :, :, None] * do[None, :, None, None, :].astype(jnp.float32)
        s = jnp.concatenate([ss.reshape(B, Qc, MOBA_KV, MOBA_R, n_g).astype(jnp.float32),
                             so.astype(jnp.float32)], axis=-1)
        m = jnp.concatenate([ms.reshape(B, Qc, MOBA_KV, MOBA_R, n_g),
                             jnp.broadcast_to(mo[None, :, None, None, :], so.shape)], axis=-1)
        p = masked_softmax(s, m)
        o = jnp.einsum('bqgrnk,bqgrnkd->bqgrd', p[..., :n_g].reshape(ss.shape).astype(kvg.dtype), kvg[..., 1, :]) \
            + jnp.einsum('bqgrk,bkgd->bqgrd', p[..., n_g:].astype(kvo.dtype), kvo[:, :, 1])
        return o.reshape(B, Qc, MOBA_HEADS, HEAD_DIM)

    return sweep(chunk, (q,), q0, SPARSE_QBLK)


def even_layer(h, q0, past, w_in, b_fgt, w_out, pe, w1, w2):
    B, Q, _ = h.shape
    wb = min(WINDOW, PAST_LEN)
    nq, c_kv, s_kv, w_kv, gl, fq, fkv, fl = split_cols(h @ w_in, EVEN_SPLITS)
    q_nsa = nq.reshape(B, Q, NSA_HEADS, HEAD_DIM)
    kvc = c_kv.reshape(B, Q, 2, NSA_KV, HEAD_DIM)
    kvs = s_kv.reshape(B, Q, 2, NSA_KV, HEAD_DIM)
    kvw = w_kv.reshape(B, Q, 2, NSA_KV, HEAD_DIM)
    gates = gl.reshape(B, Q, NSA_HEADS, 3)
    q_fox = fq.reshape(B, Q, FOX_HEADS, HEAD_DIM)
    kvf = fkv.reshape(B, Q, 2, FOX_HEADS, HEAD_DIM)
    lf = jax.nn.log_sigmoid(fl.astype(jnp.float32) + b_fgt.astype(jnp.float32))
    if past is None:
        kvc_all, kvs_all, kvf_all, lf_all = kvc, kvs, kvf, lf
        kvw_all = jnp.pad(kvw, ((0, 0), (WINDOW, 0), (0, 0), (0, 0), (0, 0)))
    else:
        p_c, p_s, p_w, p_f, p_lf = past
        kvc_all = jnp.concatenate([p_c, kvc], axis=1)
        kvs_all = jnp.concatenate([p_s, kvs], axis=1)
        kvf_all = jnp.concatenate([p_f, kvf], axis=1)
        lf_all = jnp.concatenate([p_lf.astype(jnp.float32), lf], axis=1)
        kvw_all = jnp.concatenate([jnp.zeros((B, WINDOW - wb, 2, NSA_KV, HEAD_DIM), kvw.dtype), p_w, kvw], axis=1)
    c_all = jnp.cumsum(lf_all, axis=1)
    o_nsa = nsa_mixer(q_nsa, gates, kvc_all, kvs_all, kvw_all, q0, pe, w1, w2)
    o_fox = fox_mixer(q_fox, kvf_all, c_all, q0)
    o = jnp.concatenate([o_nsa.reshape(B, Q, -1), o_fox.reshape(B, Q, -1)], axis=-1) @ w_out
    return o, (kvc, kvs, kvw_all[:, -wb:], kvf, lf.astype(h.dtype))


def odd_layer(h, q0, past, w_in, w_out):
    B, Q, _ = h.shape
    qm, kvm = split_cols(h @ w_in, ODD_SPLITS)
    q = qm.reshape(B, Q, MOBA_HEADS, HEAD_DIM)
    kv = kvm.reshape(B, Q, 2, MOBA_KV, HEAD_DIM)
    kv_all = kv if past is None else jnp.concatenate([past, kv], axis=1)
    o = moba_mixer(q, kv_all, q0)
    return o.reshape(B, Q, MOBA_HEADS * HEAD_DIM) @ w_out, kv


def trunk(x, q0, caches, page_table, norm_mix, norm_mlp, w_in_even, b_fgt, w_out_even, cmp_pe, cmp_w1, cmp_w2,
          w_in_odd, w_out_odd, w_up, w_down, norm_final):
    st_c, st_s, st_w, st_f, st_lf, st_m = [], [], [], [], [], []
    for layer in range(DEPTH):
        li = layer // 2
        h = rmsnorm(x, norm_mix[layer])
        if layer % 2 == 0:
            past = None
            if caches is not None:
                c_cmp, c_slc, s_win, c_fox, c_lf, _ = caches
                past = (gather_pages(c_cmp, li, page_table), gather_pages(c_slc, li, page_table), s_win[li],
                        gather_pages(c_fox, li, page_table), gather_pages(c_lf, li, page_table))
            o, (nc, ns, nw, nf, nlf) = even_layer(h, q0, past, w_in_even[li], b_fgt[li], w_out_even[li],
                                                  cmp_pe[li], cmp_w1[li], cmp_w2[li])
            st_c.append(nc)
            st_s.append(ns)
            st_w.append(nw)
            st_f.append(nf)
            st_lf.append(nlf)
        else:
            past = None if caches is None else gather_pages(caches[5], li, page_table)
            o, nm = odd_layer(h, q0, past, w_in_odd[li], w_out_odd[li])
            st_m.append(nm)
        x = x + o
        h = rmsnorm(x, norm_mlp[layer])
        x = x + jnp.square(jax.nn.relu(h @ w_up[layer])) @ w_down[layer]
    return rmsnorm(x, norm_final), (jnp.stack(st_c), jnp.stack(st_s), jnp.stack(st_w), jnp.stack(st_f),
                                    jnp.stack(st_lf), jnp.stack(st_m))


def setup_inputs(seed: int = 0) -> dict:
    key = jax.random.key(seed)
    ks = jax.random.split(key, 24)
    f32 = jnp.float32
    n_pages = PAST_LEN // PAGE_SIZE
    n_used = DEC_BATCH * n_pages
    n_pool = (5 * n_used + 3) // 4
    wb = min(WINDOW, PAST_LEN)

    def nrm(k, shape, scale=1.0):
        return scale * jax.random.normal(k, shape, f32)

    page_table = jax.random.permutation(ks[0], n_pool)[:n_used].reshape(DEC_BATCH, n_pages).astype(jnp.int32)
    return {
        'x_prompt': nrm(ks[1], (BATCH, SEQ, D_MODEL)),
        'x_sample': nrm(ks[2], (DEC_BATCH, DEC_SEQ, D_MODEL)),
        'cache_nsa_cmp_kv': nrm(ks[3], (N_EVEN, n_pool, PAGE_SIZE, 2, NSA_KV, HEAD_DIM)),
        'cache_nsa_slc_kv': nrm(ks[4], (N_EVEN, n_pool, PAGE_SIZE, 2, NSA_KV, HEAD_DIM)),
        'state_nsa_win_kv': nrm(ks[5], (N_EVEN, DEC_BATCH, wb, 2, NSA_KV, HEAD_DIM)),
        'cache_fox_kv': nrm(ks[6], (N_EVEN, n_pool, PAGE_SIZE, 2, FOX_HEADS, HEAD_DIM)),
        'cache_fox_lf': jax.nn.log_sigmoid(FGT_BIAS_MEAN + nrm(ks[7], (N_EVEN, n_pool, PAGE_SIZE, FOX_HEADS), 0.5)),
        'cache_moba_kv': nrm(ks[8], (N_ODD, n_pool, PAGE_SIZE, 2, MOBA_KV, HEAD_DIM)),
        'page_table': page_table,
        'norm_mix': 1.0 + nrm(ks[9], (DEPTH, D_MODEL), 0.05),
        'norm_mlp': 1.0 + nrm(ks[10], (DEPTH, D_MODEL), 0.05),
        'w_in_even': nrm(ks[11], (N_EVEN, D_MODEL, EVEN_COLS), D_MODEL ** -0.5),
        'b_fgt': FGT_BIAS_MEAN + nrm(ks[12], (N_EVEN, FOX_HEADS), 0.5),
        'w_out_even': nrm(ks[13], (N_EVEN, (NSA_HEADS + FOX_HEADS) * HEAD_DIM, D_MODEL),
                          ((NSA_HEADS + FOX_HEADS) * HEAD_DIM) ** -0.5),
        'cmp_pe': nrm(ks[14], (N_EVEN, 2, CMP_LEN, HEAD_DIM), 0.1),
        'cmp_w1': nrm(ks[15], (N_EVEN, 2, CMP_LEN * HEAD_DIM, CMP_HIDDEN), (CMP_LEN * HEAD_DIM) ** -0.5),
        'cmp_w2': nrm(ks[16], (N_EVEN, 2, CMP_HIDDEN, HEAD_DIM), CMP_HIDDEN ** -0.5),
        'w_in_odd': nrm(ks[17], (N_ODD, D_MODEL, ODD_COLS), D_MODEL ** -0.5),
        'w_out_odd': nrm(ks[18], (N_ODD, MOBA_HEADS * HEAD_DIM, D_MODEL), (MOBA_HEADS * HEAD_DIM) ** -0.5),
        'w_up': nrm(ks[19], (DEPTH, D_MODEL, D_FF), D_MODEL ** -0.5),
        'w_down': nrm(ks[20], (DEPTH, D_FF, D_MODEL), D_FF ** -0.5),
        'norm_final': 1.0 + nrm(ks[21], (D_MODEL,), 0.05),
    }


def reference(x_prompt, x_sample, cache_nsa_cmp_kv, cache_nsa_slc_kv, state_nsa_win_kv, cache_fox_kv, cache_fox_lf,
              cache_moba_kv, page_table, norm_mix, norm_mlp, w_in_even, b_fgt, w_out_even, cmp_pe, cmp_w1, cmp_w2,
              w_in_odd, w_out_odd, w_up, w_down, norm_final):
    weights = (norm_mix, norm_mlp, w_in_even, b_fgt, w_out_even, cmp_pe, cmp_w1, cmp_w2,
               w_in_odd, w_out_odd, w_up, w_down, norm_final)
    y_prompt, (p_cmp, p_slc, p_win, p_fox, p_lf, p_moba) = trunk(x_prompt, 0, None, None, *weights)
    caches = (cache_nsa_cmp_kv, cache_nsa_slc_kv, state_nsa_win_kv, cache_fox_kv, cache_fox_lf, cache_moba_kv)
    y_sample, (s_cmp, s_slc, s_win, s_fox, s_lf, s_moba) = trunk(x_sample, PAST_LEN, caches, page_table, *weights)
    return (y_prompt, y_sample, p_cmp, s_cmp, p_slc, s_slc, p_win, s_win, p_fox, s_fox, p_lf, s_lf, p_moba, s_moba)
```

```python
import functools
import math

import numpy as np
import jax
import jax.numpy as jnp
from jax import lax
from jax.experimental import pallas as pl
from jax.experimental.pallas import tpu as pltpu

F32 = jnp.float32
BF16 = jnp.bfloat16
I32 = jnp.int32

D_MODEL = 1024
HEAD_DIM = 64
NSA_HEADS = 8
NSA_KV = 2
NSA_R = NSA_HEADS // NSA_KV
CMP_LEN = 32
CMP_STRIDE = 16
CMP_HIDDEN = 128
SLC_BLOCK = 64
SLC_TOPN = 16
WINDOW = 512
FOX_HEADS = 8
MOBA_HEADS = 16
MOBA_KV = 4
MOBA_R = MOBA_HEADS // MOBA_KV
MOBA_BLOCK = 256
MOBA_TOPK = 3
D_FF = 4 * D_MODEL
PAGE = 128
RMS_EPS = 1e-6
NEG = -1e30
SCALE = HEAD_DIM ** -0.5
LANES = 128
VMEM_LIMIT = 56 * 1024 * 1024

NT_DIMS = (((1,), (1,)), ((), ()))


def _alibi(n):
    return [float(np.float32(2.0 ** (-8.0 * (i + 1) / n))) for i in range(n)]


NSA_SLOPES = _alibi(NSA_HEADS)
MOBA_SLOPES = _alibi(MOBA_HEADS)


def _div(x, n):
    assert n & (n - 1) == 0
    return jnp.right_shift(x, int(math.log2(n)))


def _mod(x, n):
    assert n & (n - 1) == 0
    return jnp.bitwise_and(x, n - 1)


def _dot(a, b):
    return jnp.dot(a, b, preferred_element_type=F32)


def _dot_nt(a, b):
    return lax.dot_general(a, b, NT_DIMS, preferred_element_type=F32)


def _split3(x):
    hi = x.astype(BF16)
    r1 = x - hi.astype(F32)
    mid = r1.astype(BF16)
    lo = (r1 - mid.astype(F32)).astype(BF16)
    return hi, mid, lo


def _dot01(x, mat, nt=False):
    n = x.shape[0]
    st = jnp.concatenate(_split3(x), axis=0)
    y = _dot_nt(st, mat) if nt else _dot(st, mat)
    return y[0:n] + y[n:2 * n] + y[2 * n:3 * n]


def _softmax_parts(s, valid):
    s = jnp.where(valid, s, NEG)
    m = jnp.max(s, axis=-1, keepdims=True)
    p = jnp.where(valid, jnp.exp(s - m), 0.0)
    l = jnp.maximum(jnp.sum(p, axis=-1, keepdims=True), 1e-30)
    return p, l, m


def _topk_mask(v, nsel, nb):
    lane = lax.broadcasted_iota(I32, v.shape, 1)
    cnt = jnp.zeros(v.shape, I32)
    for k in range(nb):
        col = v[:, k:k + 1]
        beats = (col > v) | ((col == v) & (lane > k))
        cnt = cnt + jnp.where(beats, 1, 0)
    return (cnt < nsel) & (v > -jnp.inf)


def _stack_heads(q, g, nr):
    return jnp.concatenate([q[:, (g * nr + r) * HEAD_DIM:(g * nr + r + 1) * HEAD_DIM] for r in range(nr)], axis=0)


def _row_slopes(r4, slopes, g, nr):
    out = jnp.zeros(r4.shape, F32)
    for r in range(nr):
        out = jnp.where(r4 == r, slopes[g * nr + r], out)
    return out


def _proj_kernel(x_ref, g_ref, wn_ref, wt_ref, b_ref, *outs, nat, tr):
    x = x_ref[...]
    ms = jnp.mean(x * x, axis=-1, keepdims=True)
    h = (x * lax.rsqrt(ms + RMS_EPS)) * g_ref[...]
    hb = h.astype(BF16)
    yn = _dot(hb, wn_ref[...])
    yt = _dot_nt(wt_ref[...], hb)
    k = 0
    for (c0, w, kind) in nat:
        v = yn[:, c0:c0 + w]
        if kind == "q":
            outs[k][...] = (v * SCALE).astype(BF16)
        else:
            outs[k][...] = v
        k += 1
    for (r0, n, kind) in tr:
        v = yt[r0:r0 + n, :]
        if kind == "f32":
            outs[k][...] = v
        elif kind == "bf16":
            outs[k][...] = v.astype(BF16)
        elif kind == "bf16c":
            vb = v.astype(BF16)
            for c in range(v.shape[1] // LANES):
                outs[k][c] = vb[:, c * LANES:(c + 1) * LANES]
        else:
            z = v + b_ref[...]
            outs[k][...] = jnp.minimum(z, 0.0) - jnp.log(1.0 + jnp.exp(-jnp.abs(z)))
        k += 1


def _proj(x2d, gain, wn, wt, bias, nat, tr, nb, s_len, tm):
    t_tot = x2d.shape[0]
    tpb = s_len // tm
    assert nb * s_len == t_tot
    out_shape, out_specs = [], []
    for (c0, w, kind) in nat:
        out_shape.append(jax.ShapeDtypeStruct((t_tot, w), BF16 if kind == "q" else F32))
        out_specs.append(pl.BlockSpec((tm, w), lambda i: (i, 0)))
    for (r0, n, kind) in tr:
        if kind == "bf16c":
            out_shape.append(jax.ShapeDtypeStruct((nb, s_len // LANES, n, LANES), BF16))
            out_specs.append(pl.BlockSpec((None, tm // LANES, n, LANES), lambda i: (i // tpb, i % tpb, 0, 0)))
        else:
            out_shape.append(jax.ShapeDtypeStruct((nb, n, s_len), BF16 if kind == "bf16" else F32))
            out_specs.append(pl.BlockSpec((None, n, tm), lambda i: (i // tpb, 0, i % tpb)))
    return pl.pallas_call(
        functools.partial(_proj_kernel, nat=tuple(nat), tr=tuple(tr)),
        grid=(t_tot // tm,),
        in_specs=[pl.BlockSpec((tm, D_MODEL), lambda i: (i, 0)),
                  pl.BlockSpec((1, D_MODEL), lambda i: (0, 0)),
                  pl.BlockSpec(wn.shape, lambda i: (0, 0)),
                  pl.BlockSpec(wt.shape, lambda i: (0, 0)),
                  pl.BlockSpec(bias.shape, lambda i: (0, 0))],
        out_specs=out_specs,
        out_shape=out_shape,
        compiler_params=pltpu.CompilerParams(dimension_semantics=("parallel",), vmem_limit_bytes=VMEM_LIMIT),
        name="norm_proj",
    )(x2d, gain, wn, wt, bias)


def _gelu_tanh(x):
    return 0.5 * x * (1.0 + jnp.tanh(math.sqrt(2.0 / math.pi) * (x + 0.044715 * (x * x * x))))


def _compress_kernel(*refs, npg, n_prefetch):
    refs = refs[n_prefetch:]
    pages = refs[:npg]
    pt_ref, pea_ref, peb_ref, w1a_ref, w1b_ref, w2_ref, out_ref, t_scr, carry = refs[npg:]
    sub = PAGE // CMP_STRIDE
    m = npg * sub

    @pl.when(pl.program_id(1) == 0)
    def _():
        carry[...] = jnp.zeros(carry.shape, F32)

    for k in range(npg):
        xb = pages[k][...].astype(BF16)
        tt = _dot_nt(pt_ref[...], xb)
        for p in range(CMP_STRIDE):
            t_scr[p, k * sub:(k + 1) * sub, :] = tt[p * sub:(p + 1) * sub, :]
    acc_a = jnp.zeros((m, w1a_ref.shape[2]), F32)
    acc_b = jnp.zeros((m, w1a_ref.shape[2]), F32)
    for p in range(CMP_STRIDE):
        tp = t_scr[p]
        acc_a = acc_a + _dot((tp + pea_ref[p]).astype(BF16), w1a_ref[p])
        acc_b = acc_b + _dot((tp + peb_ref[p]).astype(BF16), w1b_ref[p])
    rolled = pltpu.roll(acc_a, 1, axis=0)
    row = lax.broadcasted_iota(I32, acc_a.shape, 0)
    hid = jnp.where(row == 0, carry[...], rolled) + acc_b
    carry[...] = acc_a[m - 1:m, :]
    act = _gelu_tanh(hid)
    out_ref[...] = _dot(act.astype(BF16), w2_ref[...])


def _compress_weights(cmp_pe, cmp_w1, cmp_w2):
    nkg = 2 * NSA_KV
    eye = jnp.eye(nkg, dtype=F32)
    w1 = cmp_w1.reshape(2, CMP_LEN, HEAD_DIM, CMP_HIDDEN)
    w1 = jnp.repeat(w1, NSA_KV, axis=0)
    bd = jnp.einsum("kpdh,kl->pkdlh", w1, eye).reshape(CMP_LEN, nkg * HEAD_DIM, nkg * CMP_HIDDEN).astype(BF16)
    w2 = jnp.repeat(cmp_w2, NSA_KV, axis=0)
    w2bd = jnp.einsum("khd,kl->khld", w2, eye).reshape(nkg * CMP_HIDDEN, nkg * HEAD_DIM).astype(BF16)
    pe = jnp.repeat(cmp_pe, NSA_KV, axis=0)
    pe = pe.transpose(1, 0, 2).reshape(CMP_LEN, 1, nkg * HEAD_DIM)
    return bd[:CMP_STRIDE], bd[CMP_STRIDE:], w2bd, pe[:CMP_STRIDE], pe[CMP_STRIDE:]


def _perm_matrix():
    sub = PAGE // CMP_STRIDE
    pt = np.zeros((PAGE, PAGE), np.float32)
    for p in range(CMP_STRIDE):
        for j in range(sub):
            pt[p * sub + j, CMP_STRIDE * j + p] = 1.0
    return jnp.asarray(pt, BF16)


def _compress(pages_arr, page_table, cw, nb, n_pages, npg, paged):
    w1a, w1b, w2bd, pea, peb = cw
    rows = 2 * NSA_KV * HEAD_DIM
    m = npg * (PAGE // CMP_STRIDE)
    n_chunks = n_pages // npg
    page_specs = []
    for k in range(npg):
        if paged:
            page_specs.append(pl.BlockSpec((None, rows, PAGE), lambda b, c, pt, k=k: (pt[b, c * npg + k], 0, 0)))
        else:
            page_specs.append(pl.BlockSpec((None, rows, PAGE), lambda b, c, k=k: (b, 0, c * npg + k)))

    def const(shape):
        nd = len(shape)
        if paged:
            return pl.BlockSpec(shape, lambda b, c, pt: (0,) * nd)
        return pl.BlockSpec(shape, lambda b, c: (0,) * nd)

    perm = _perm_matrix()
    consts = [perm, pea, peb, w1a, w1b, w2bd]
    in_specs = page_specs + [const(a.shape) for a in consts]
    if paged:
        out_spec = pl.BlockSpec((None, m, rows), lambda b, c, pt: (b, c, 0))
    else:
        out_spec = pl.BlockSpec((None, m, rows), lambda b, c: (b, c, 0))
    n_prefetch = 1 if paged else 0
    grid_spec = pltpu.PrefetchScalarGridSpec(
        num_scalar_prefetch=n_prefetch, grid=(nb, n_chunks), in_specs=in_specs, out_specs=out_spec,
        scratch_shapes=[pltpu.VMEM((CMP_STRIDE, m, rows), F32), pltpu.VMEM((1, w1a.shape[2]), F32)])
    args = ([page_table] if paged else []) + [pages_arr] * npg + consts
    return pl.pallas_call(
        functools.partial(_compress_kernel, npg=npg, n_prefetch=n_prefetch),
        grid_spec=grid_spec,
        out_shape=jax.ShapeDtypeStruct((nb, n_chunks * m, rows), F32),
        compiler_params=pltpu.CompilerParams(dimension_semantics=("parallel", "arbitrary"),
                                             vmem_limit_bytes=VMEM_LIMIT),
        name="nsa_compress",
    )(*args)


def _cmp_to_slc(n_rows, n_c, n_s, n_cols):
    m = np.zeros((n_rows, n_cols), np.float32)
    c = np.arange(n_c)[:, None]
    j = np.arange(n_s)[None, :]
    lo = c * CMP_STRIDE
    hi = lo + CMP_LEN
    m[1:n_c + 1, :n_s] = ((lo < (j + 1) * SLC_BLOCK) & (hi > j * SLC_BLOCK)).astype(np.float32)
    return jnp.asarray(m, BF16)


def _cmp_branch(qg4, cmp_tok, mcs, g, tq, t4, slope4, n_c):
    ck = cmp_tok[:, g * HEAD_DIM:(g + 1) * HEAD_DIM].astype(BF16)
    v0 = (NSA_KV + g) * HEAD_DIM
    cv = cmp_tok[:, v0:v0 + HEAD_DIM].astype(BF16)
    sc = _dot_nt(qg4, ck)
    col = lax.broadcasted_iota(I32, sc.shape, 1)
    dc = t4 - ((col - 1) * CMP_STRIDE + (CMP_LEN - 1))
    valid = (dc >= 0) & (col >= 1) & (col <= n_c)
    p, l, _ = _softmax_parts(sc - slope4 * dc.astype(F32), valid)
    pn = p / l
    o = _dot(pn.astype(BF16), cv)
    psum = pn[0:tq]
    for r in range(1, NSA_R):
        psum = psum + pn[r * tq:(r + 1) * tq]
    return o, _dot01(psum, mcs)


def _slc_select(imp, t1, n_s):
    blk = lax.broadcasted_iota(I32, imp.shape, 1)
    tb = _div(t1, SLC_BLOCK)
    avail = (blk <= tb) & (blk < n_s)
    forced = (blk == 0) | (blk == tb) | (blk == tb - 1)
    v = jnp.where(forced, jnp.inf, jnp.where(avail, imp, -jnp.inf))
    return _topk_mask(v, min(SLC_TOPN, n_s), n_s)


def _nsa_prompt_kernel(q_ref, gl_ref, cmp_ref, mcs_ref, ks_ref, kw_ref, o_ref, *, tq, s_len, n_c):
    qi = pl.program_id(1)
    s0 = qi * tq
    q = q_ref[...]
    sig = 1.0 / (1.0 + jnp.exp(-gl_ref[...]))
    cmp_tok = cmp_ref[...]
    mcs = mcs_ref[...]
    n_s = s_len // SLC_BLOCK
    row4 = lax.broadcasted_iota(I32, (NSA_R * tq, 1), 0)
    t4 = s0 + _mod(row4, tq)
    r4 = _div(row4, tq)
    t1 = s0 + lax.broadcasted_iota(I32, (tq, 1), 0)
    nwc = WINDOW // LANES + 1
    outs = [None] * NSA_HEADS
    for g in range(NSA_KV):
        qg4 = _stack_heads(q, g, NSA_R)
        slope4 = _row_slopes(r4, NSA_SLOPES, g, NSA_R)
        o_cmp, imp = _cmp_branch(qg4, cmp_tok, mcs, g, tq, t4, slope4, n_c)
        sel = _slc_select(imp, t1, n_s)
        eb = lax.broadcasted_iota(I32, (sel.shape[1], s_len), 0)
        ep = lax.broadcasted_iota(I32, (sel.shape[1], s_len), 1)
        expand = jnp.where(_div(ep, SLC_BLOCK) == eb, 1.0, 0.0).astype(BF16)
        mexp = _dot(jnp.where(sel, 1.0, 0.0).astype(BF16), expand)
        mexp4 = jnp.concatenate([mexp] * NSA_R, axis=0)
        pos = lax.broadcasted_iota(I32, (NSA_R * tq, s_len), 1)
        ds = t4 - pos
        valid = (mexp4 > 0.5) & (ds >= 0)
        kt = ks_ref[g * HEAD_DIM:(g + 1) * HEAD_DIM, :]
        vt = ks_ref[(NSA_KV + g) * HEAD_DIM:(NSA_KV + g + 1) * HEAD_DIM, :]
        p, l, _ = _softmax_parts(_dot(qg4, kt) - slope4 * ds.astype(F32), valid)
        o_slc = _dot_nt(p.astype(BF16), vt) / l
        chunks = [kw_ref[jnp.maximum(qi - (nwc - 1) + j, 0)] for j in range(nwc)]
        kwin = jnp.concatenate([c[g * HEAD_DIM:(g + 1) * HEAD_DIM, :] for c in chunks], axis=1)
        vwin = jnp.concatenate([c[(NSA_KV + g) * HEAD_DIM:(NSA_KV + g + 1) * HEAD_DIM, :] for c in chunks], axis=1)
        wp = (qi - (nwc - 1)) * LANES + lax.broadcasted_iota(I32, (NSA_R * tq, nwc * LANES), 1)
        dw = t4 - wp
        valid = (wp >= 0) & (dw >= 0) & (dw < WINDOW)
        p, l, _ = _softmax_parts(_dot(qg4, kwin) - slope4 * dw.astype(F32), valid)
        o_win = _dot_nt(p.astype(BF16), vwin) / l
        for r in range(NSA_R):
            h = g * NSA_R + r
            sl = slice(r * tq, (r + 1) * tq)
            outs[h] = (o_cmp[sl] * sig[:, 3 * h:3 * h + 1] + o_slc[sl] * sig[:, 3 * h + 1:3 * h + 2]
                       + o_win[sl] * sig[:, 3 * h + 2:3 * h + 3])
    o_ref[...] = jnp.concatenate(outs, axis=1).astype(BF16)


def _nsa_prompt(nq, gl, cmp_tok, ks_b, kw_c, nb, s_len):
    tq = LANES
    nqt = s_len // tq
    n_c = s_len // CMP_STRIDE - CMP_LEN // CMP_STRIDE + 1
    mcs = _cmp_to_slc(cmp_tok.shape[1], n_c, s_len // SLC_BLOCK, LANES)
    rows = 2 * NSA_KV * HEAD_DIM
    return pl.pallas_call(
        functools.partial(_nsa_prompt_kernel, tq=tq, s_len=s_len, n_c=n_c),
        grid=(nb, nqt),
        in_specs=[pl.BlockSpec((tq, NSA_HEADS * HEAD_DIM), lambda b, i: (b * nqt + i, 0)),
                  pl.BlockSpec((tq, LANES), lambda b, i: (b * nqt + i, 0)),
                  pl.BlockSpec((None,) + cmp_tok.shape[1:], lambda b, i: (b, 0, 0)),
                  pl.BlockSpec(mcs.shape, lambda b, i: (0, 0)),
                  pl.BlockSpec((None, rows, s_len), lambda b, i: (b, 0, 0)),
                  pl.BlockSpec((None, s_len // LANES, rows, LANES), lambda b, i: (b, 0, 0, 0))],
        out_specs=pl.BlockSpec((tq, NSA_HEADS * HEAD_DIM), lambda b, i: (b * nqt + i, 0)),
        out_shape=jax.ShapeDtypeStruct((nb * s_len, NSA_HEADS * HEAD_DIM), BF16),
        compiler_params=pltpu.CompilerParams(dimension_semantics=("parallel", "arbitrary"),
                                             vmem_limit_bytes=VMEM_LIMIT),
        name="nsa_prompt",
    )(nq, gl, cmp_tok, mcs, ks_b, kw_c)


def _tri_matrix():
    i = np.arange(LANES)
    return jnp.asarray((i[:, None] <= i[None, :]).astype(np.float32), BF16)


def _fox_prompt_kernel(q_ref, kv_ref, lf_ref, u_ref, o_ref, c_scr, *, tq, s_len):
    qi = pl.program_id(1)

    @pl.when(qi == 0)
    def _():
        carry = jnp.zeros((FOX_HEADS, 1), F32)
        for blk in range(s_len // LANES):
            cs = _dot01(lf_ref[:, blk * LANES:(blk + 1) * LANES], u_ref[...]) + carry
            c_scr[:, blk * LANES:(blk + 1) * LANES] = cs
            carry = cs[:, LANES - 1:LANES]

    t = qi * tq + lax.broadcasted_iota(I32, (tq, 1), 0)
    pos = lax.broadcasted_iota(I32, (tq, s_len), 1)
    valid = pos <= t
    nh = FOX_HEADS
    outs = []
    for h in range(nh):
        qh = q_ref[:, h * HEAD_DIM:(h + 1) * HEAD_DIM]
        kt = kv_ref[h * HEAD_DIM:(h + 1) * HEAD_DIM, :]
        vt = kv_ref[(nh + h) * HEAD_DIM:(nh + h + 1) * HEAD_DIM, :]
        p, l, _ = _softmax_parts(_dot(qh, kt) - c_scr[h:h + 1, :], valid)
        outs.append(_dot_nt(p.astype(BF16), vt) / l)
    o_ref[...] = jnp.concatenate(outs, axis=1).astype(BF16)


def _fox_prompt(fq, fkv_b, lf_t, nb, s_len):
    tq = LANES
    nqt = s_len // tq
    rows = 2 * FOX_HEADS * HEAD_DIM
    tri = _tri_matrix()
    return pl.pallas_call(
        functools.partial(_fox_prompt_kernel, tq=tq, s_len=s_len),
        grid=(nb, nqt),
        in_specs=[pl.BlockSpec((tq, FOX_HEADS * HEAD_DIM), lambda b, i: (b * nqt + i, 0)),
                  pl.BlockSpec((None, rows, s_len), lambda b, i: (b, 0, 0)),
                  pl.BlockSpec((None, FOX_HEADS, s_len), lambda b, i: (b, 0, 0)),
                  pl.BlockSpec(tri.shape, lambda b, i: (0, 0))],
        out_specs=pl.BlockSpec((tq, FOX_HEADS * HEAD_DIM), lambda b, i: (b * nqt + i, 0)),
        out_shape=jax.ShapeDtypeStruct((nb * s_len, FOX_HEADS * HEAD_DIM), BF16),
        scratch_shapes=[pltpu.VMEM((FOX_HEADS, s_len), F32)],
        compiler_params=pltpu.CompilerParams(dimension_semantics=("parallel", "arbitrary"),
                                             vmem_limit_bytes=VMEM_LIMIT),
        name="fox_prompt",
    )(fq, fkv_b, lf_t, tri)


def _moba_prompt_kernel(q_ref, kv_ref, eavg_ref, o_ref, km_scr, *, tq, s_len):
    qi = pl.program_id(1)
    n_b = s_len // MOBA_BLOCK

    @pl.when(qi == 0)
    def _():
        for g in range(MOBA_KV):
            km = _dot(kv_ref[g * HEAD_DIM:(g + 1) * HEAD_DIM, :], eavg_ref[...]) * (1.0 / MOBA_BLOCK)
            km_scr[g] = km.astype(BF16)

    s0 = qi * tq
    tb = s0 // MOBA_BLOCK
    q = q_ref[...]
    nrow = MOBA_R * tq
    row4 = lax.broadcasted_iota(I32, (nrow, 1), 0)
    t4 = s0 + _mod(row4, tq)
    r4 = _div(row4, tq)
    pos = lax.broadcasted_iota(I32, (nrow, s_len), 1)
    do = t4 - pos
    own = (_div(pos, MOBA_BLOCK) == tb) & (do >= 0)
    blk_lane = lax.broadcasted_iota(I32, (nrow, LANES), 1)
    outs = [None] * MOBA_HEADS
    for g in range(MOBA_KV):
        qg4 = _stack_heads(q, g, MOBA_R)
        slope4 = _row_slopes(r4, MOBA_SLOPES, g, MOBA_R)
        gate = _dot(qg4, km_scr[g])
        sel = _topk_mask(jnp.where(blk_lane < tb, gate, -jnp.inf), min(MOBA_TOPK, n_b), n_b)
        self = jnp.where(sel, 1.0, 0.0)
        mexp = jnp.concatenate([jnp.broadcast_to(self[:, j:j + 1], (nrow, MOBA_BLOCK)) for j in range(n_b)], axis=1)
        valid = (mexp > 0.5) | own
        kt = kv_ref[g * HEAD_DIM:(g + 1) * HEAD_DIM, :]
        vt = kv_ref[(MOBA_KV + g) * HEAD_DIM:(MOBA_KV + g + 1) * HEAD_DIM, :]
        p, l, _ = _softmax_parts(_dot(qg4, kt) - slope4 * do.astype(F32), valid)
        o = _dot_nt(p.astype(BF16), vt) / l
        for r in range(MOBA_R):
            outs[g * MOBA_R + r] = o[r * tq:(r + 1) * tq]
    o_ref[...] = jnp.concatenate(outs, axis=1).astype(BF16)


def _moba_prompt(mq, mkv_b, nb, s_len):
    tq = LANES
    nqt = s_len // tq
    rows = 2 * MOBA_KV * HEAD_DIM
    e = np.zeros((s_len, LANES), np.float32)
    e[np.arange(s_len), np.arange(s_len) // MOBA_BLOCK] = 1.0
    eavg = jnp.asarray(e, BF16)
    return pl.pallas_call(
        functools.partial(_moba_prompt_kernel, tq=tq, s_len=s_len),
        grid=(nb, nqt),
        in_specs=[pl.BlockSpec((tq, MOBA_HEADS * HEAD_DIM), lambda b, i: (b * nqt + i, 0)),
                  pl.BlockSpec((None, rows, s_len), lambda b, i: (b, 0, 0)),
                  pl.BlockSpec(eavg.shape, lambda b, i: (0, 0))],
        out_specs=pl.BlockSpec((tq, MOBA_HEADS * HEAD_DIM), lambda b, i: (b * nqt + i, 0)),
        out_shape=jax.ShapeDtypeStruct((nb * s_len, MOBA_HEADS * HEAD_DIM), BF16),
        scratch_shapes=[pltpu.VMEM((MOBA_KV, HEAD_DIM, LANES), BF16)],
        compiler_params=pltpu.CompilerParams(dimension_semantics=("parallel", "arbitrary"),
                                             vmem_limit_bytes=VMEM_LIMIT),
        name="moba_prompt",
    )(mq, mkv_b, eavg)


def _post_kernel(*refs, n_o, final):
    x_ref = refs[0]
    o_refs = refs[1:1 + n_o]
    wo_refs = refs[1 + n_o:1 + 2 * n_o]
    g_ref, wup_ref, wdn_ref, gf_ref, out_ref, x1_scr, h_scr, acc_scr = refs[1 + 2 * n_o:]
    j = pl.program_id(1)

    @pl.when(j == 0)
    def _():
        x1 = x_ref[...]
        for o_ref, wo_ref in zip(o_refs, wo_refs):
            x1 = x1 + _dot(o_ref[...], wo_ref[...])
        x1_scr[...] = x1
        ms = jnp.mean(x1 * x1, axis=-1, keepdims=True)
        h_scr[...] = ((x1 * lax.rsqrt(ms + RMS_EPS)) * g_ref[...]).astype(BF16)
        acc_scr[...] = jnp.zeros(acc_scr.shape, F32)

    u = jnp.maximum(_dot(h_scr[...], wup_ref[...]), 0.0)
    acc_scr[...] += _dot((u * u).astype(BF16), wdn_ref[...])

    @pl.when(j == pl.num_programs(1) - 1)
    def _():
        y = x1_scr[...] + acc_scr[...]
        if final:
            ms = jnp.mean(y * y, axis=-1, keepdims=True)
            y = (y * lax.rsqrt(ms + RMS_EPS)) * gf_ref[...]
        out_ref[...] = y


def _post(x2d, o_list, wo_list, g_mlp, w_up, w_down, g_final, final, tm, tf=1024):
    t_tot = x2d.shape[0]
    n_o = len(o_list)
    in_specs = [pl.BlockSpec((tm, D_MODEL), lambda i, j: (i, 0))]
    in_specs += [pl.BlockSpec((tm, o.shape[1]), lambda i, j: (i, 0)) for o in o_list]
    in_specs += [pl.BlockSpec(w.shape, lambda i, j: (0, 0)) for w in wo_list]
    in_specs += [pl.BlockSpec((1, D_MODEL), lambda i, j: (0, 0)),
                 pl.BlockSpec((D_MODEL, tf), lambda i, j: (0, j)),
                 pl.BlockSpec((tf, D_MODEL), lambda i, j: (j, 0)),
                 pl.BlockSpec((1, D_MODEL), lambda i, j: (0, 0))]
    return pl.pallas_call(
        functools.partial(_post_kernel, n_o=n_o, final=final),
        grid=(t_tot // tm, D_FF // tf),
        in_specs=in_specs,
        out_specs=pl.BlockSpec((tm, D_MODEL), lambda i, j: (i, 0)),
        out_shape=jax.ShapeDtypeStruct((t_tot, D_MODEL), F32),
        scratch_shapes=[pltpu.VMEM((tm, D_MODEL), F32), pltpu.VMEM((tm, D_MODEL), BF16),
                        pltpu.VMEM((tm, D_MODEL), F32)],
        compiler_params=pltpu.CompilerParams(dimension_semantics=("parallel", "arbitrary"),
                                             vmem_limit_bytes=VMEM_LIMIT),
        name="post_mlp",
    )(x2d, *o_list, *wo_list, g_mlp, w_up, w_down, g_final)


def _nsa_dec_kernel(q_ref, cmp_ref, mcs_ref, kw_ref, ocmp_ref, owin_ref, sel_ref, *, tq, q0, n_c, n_s):
    q = q_ref[...]
    cmp_tok = cmp_ref[...]
    mcs = mcs_ref[...]
    row4 = lax.broadcasted_iota(I32, (NSA_R * tq, 1), 0)
    t4 = q0 + _mod(row4, tq)
    r4 = _div(row4, tq)
    t1 = q0 + lax.broadcasted_iota(I32, (tq, 1), 0)
    nbp = mcs.shape[1]
    wlen = kw_ref.shape[1]
    for g in range(NSA_KV):
        qg4 = _stack_heads(q, g, NSA_R)
        slope4 = _row_slopes(r4, NSA_SLOPES, g, NSA_R)
        o_cmp, imp = _cmp_branch(qg4, cmp_tok, mcs, g, tq, t4, slope4, n_c)
        sel = _slc_select(imp, t1, n_s)
        sel_ref[:, g * nbp:(g + 1) * nbp] = jnp.where(sel, 1.0, 0.0)
        kwin = kw_ref[g * HEAD_DIM:(g + 1) * HEAD_DIM, :].astype(BF16)
        vwin = kw_ref[(NSA_KV + g) * HEAD_DIM:(NSA_KV + g + 1) * HEAD_DIM, :].astype(BF16)
        wp = (q0 - WINDOW) + lax.broadcasted_iota(I32, (NSA_R * tq, wlen), 1)
        dw = t4 - wp
        valid = (wp >= 0) & (dw >= 0) & (dw < WINDOW)
        p, l, _ = _softmax_parts(_dot(qg4, kwin) - slope4 * dw.astype(F32), valid)
        o_win = _dot_nt(p.astype(BF16), vwin) / l
        for r in range(NSA_R):
            h = g * NSA_R + r
            ocmp_ref[:, h * HEAD_DIM:(h + 1) * HEAD_DIM] = o_cmp[r * tq:(r + 1) * tq]
            owin_ref[:, h * HEAD_DIM:(h + 1) * HEAD_DIM] = o_win[r * tq:(r + 1) * tq]


def _nsa_dec(q8, cmp_tok, kwin_t, nb, q0, n_c, n_s):
    tq = q8.shape[1]
    nbp = -(-n_s // LANES) * LANES
    mcs = _cmp_to_slc(cmp_tok.shape[1], n_c, n_s, nbp)
    width = NSA_HEADS * HEAD_DIM
    return pl.pallas_call(
        functools.partial(_nsa_dec_kernel, tq=tq, q0=q0, n_c=n_c, n_s=n_s),
        grid=(nb,),
        in_specs=[pl.BlockSpec((None, tq, width), lambda b: (b, 0, 0)),
                  pl.BlockSpec((None,) + cmp_tok.shape[1:], lambda b: (b, 0, 0)),
                  pl.BlockSpec(mcs.shape, lambda b: (0, 0)),
                  pl.BlockSpec((None,) + kwin_t.shape[1:], lambda b: (b, 0, 0))],
        out_specs=[pl.BlockSpec((None, tq, width), lambda b: (b, 0, 0)),
                   pl.BlockSpec((None, tq, width), lambda b: (b, 0, 0)),
                   pl.BlockSpec((None, tq, NSA_KV * nbp), lambda b: (b, 0, 0))],
        out_shape=[jax.ShapeDtypeStruct((nb, tq, width), F32),
                   jax.ShapeDtypeStruct((nb, tq, width), F32),
                   jax.ShapeDtypeStruct((nb, tq, NSA_KV * nbp), F32)],
        compiler_params=pltpu.CompilerParams(dimension_semantics=("parallel",), vmem_limit_bytes=VMEM_LIMIT),
        name="nsa_decode_cmp_win",
    )(q8, cmp_tok, mcs, kwin_t)


def _stream_kernel(*refs, mode, pps, n_rows, hk, q0, n_segs):
    refs = refs[1:]
    qbd_ref, slope_ref, t_ref = refs[0:3]
    k = 3
    sel_ref = None
    if mode == "slc":
        sel_ref = refs[k]
        k += 1
    pages = refs[k:k + pps]
    k += pps
    lf_pages = None
    if mode == "fox":
        lf_pages = refs[k:k + pps]
        k += pps
    new_ref = refs[k]
    k += 1
    newlf_ref = tri_ref = None
    if mode == "fox":
        newlf_ref, tri_ref = refs[k], refs[k + 1]
        k += 2
    out_ref = refs[k]
    oparts, m_s, l_s, x_s = refs[k + 1:k + 5]
    st = pl.program_id(1)
    nsteps = pl.num_programs(1)
    qbd = qbd_ref[...]
    slope = slope_ref[...]
    tcol = t_ref[...]
    lane = lax.broadcasted_iota(I32, (n_rows, LANES), 1)
    seg_lane = lax.broadcasted_iota(I32, m_s.shape, 1)
    reps = n_rows // FOX_HEADS

    @pl.when(st == 0)
    def _():
        m_s[...] = jnp.zeros(m_s.shape, F32)
        l_s[...] = jnp.zeros(l_s.shape, F32)
        x_s[...] = jnp.zeros(x_s.shape, F32)

    def put(ref, seg, col):
        ref[...] = jnp.where(seg_lane == seg, col, ref[...])

    def scores(page, page_pos0, lf_page, run):
        kt = page[0:hk, :].astype(BF16)
        vt = page[hk:2 * hk, :].astype(BF16)
        s_raw = _dot(qbd, kt)
        if mode == "fox":
            cs = _dot01(lf_page, tri_ref[...]) + run
            run = cs[:, LANES - 1:LANES]
            s = s_raw - jnp.concatenate([cs] * reps, axis=0)
        else:
            s = s_raw - slope * (tcol - (page_pos0 + lane)).astype(F32)
        return s_raw, s, vt, run

    run = jnp.zeros((FOX_HEADS, 1), F32)
    if mode == "slc":
        for i in range(pps):
            pidx = st * pps + i
            _, s, vt, _ = scores(pages[i][...], pidx * PAGE, None, None)
            for half in range(PAGE // SLC_BLOCK):
                hm = (lane >= half * SLC_BLOCK) & (lane < (half + 1) * SLC_BLOCK)
                p, l, m = _softmax_parts(s, hm)
                seg = pidx * (PAGE // SLC_BLOCK) + half
                oparts[seg] = _dot_nt(p.astype(BF16), vt)
                put(m_s, seg, m)
                put(l_s, seg, l)
    else:
        ss, vts, raws = [], [], []
        for i in range(pps):
            s_raw, s, vt, run = scores(pages[i][...], (st * pps + i) * PAGE,
                                       lf_pages[i][...] if mode == "fox" else None, run)
            ss.append(s)
            vts.append(vt)
            raws.append(s_raw)
        m = ss[0].max(axis=-1, keepdims=True)
        for s in ss[1:]:
            m = jnp.maximum(m, s.max(axis=-1, keepdims=True))
        l = jnp.zeros((n_rows, 1), F32)
        o = jnp.zeros((n_rows, hk), F32)
        for s, vt in zip(ss, vts):
            p = jnp.exp(s - m)
            l = l + p.sum(axis=-1, keepdims=True)
            o = o + _dot_nt(p.astype(BF16), vt)
        oparts[st] = o
        put(m_s, st, m)
        put(l_s, st, l)
        if mode == "fox":
            put(x_s, st, jnp.concatenate([run] * reps, axis=0))
        else:
            gsum = raws[0].sum(axis=-1, keepdims=True)
            for r_ in raws[1:]:
                gsum = gsum + r_.sum(axis=-1, keepdims=True)
            put(x_s, st, gsum * (1.0 / MOBA_BLOCK))

    @pl.when(st == nsteps - 1)
    def _():
        last = n_segs - 1
        _, s, vt, _ = scores(new_ref[...], q0, newlf_ref[...] if mode == "fox" else None,
                             jnp.zeros((FOX_HEADS, 1), F32))
        p, l, m = _softmax_parts(s, (q0 + lane) <= tcol)
        oparts[last] = _dot_nt(p.astype(BF16), vt)
        mm = jnp.where(seg_lane == last, m, m_s[...])
        ll = jnp.where(seg_lane == last, l, l_s[...])
        if mode == "fox":
            mm = mm + _dot01(x_s[...], tri_ref[...], nt=True)
            valid = seg_lane < n_segs
        elif mode == "slc":
            valid = sel_ref[...] > 0.5
        else:
            n_b = n_segs - 1
            gate = jnp.where(seg_lane < n_b, x_s[...], -jnp.inf)
            valid = _topk_mask(gate, min(MOBA_TOPK, n_b), n_b) | (seg_lane == last)
        mx = jnp.max(jnp.where(valid, mm, NEG), axis=-1, keepdims=True)
        w = jnp.where(valid, jnp.exp(mm - mx), 0.0)
        den = jnp.maximum(jnp.sum(w * ll, axis=-1, keepdims=True), 1e-30)
        num = jnp.zeros((n_rows, hk), F32)
        for seg in range(n_segs):
            num = num + w[:, seg:seg + 1] * oparts[seg]
        out_ref[...] = num / den


def _stream(mode, qbd, slope_col, t_col, sel, pool_t, lf_pool_t, new_t, newlf_t, page_table, q0, pps):
    nb, n_rows, hk = qbd.shape
    n_pages = page_table.shape[1]
    nsteps = n_pages // pps
    n_segs = (n_pages * (PAGE // SLC_BLOCK) if mode == "slc" else nsteps) + 1
    segp = -(-n_segs // LANES) * LANES
    if mode == "fox":
        assert segp == LANES

    def cst(shape):
        nd = len(shape)
        return pl.BlockSpec(shape, lambda b, s, pt: (0,) * nd)

    def per_b(shape):
        nd = len(shape)
        return pl.BlockSpec((None,) + shape, lambda b, s, pt: (b,) + (0,) * nd)

    in_specs = [per_b((n_rows, hk)), cst((n_rows, 1)), cst((n_rows, 1))]
    args = [qbd, slope_col, t_col]
    if mode == "slc":
        in_specs.append(per_b((n_rows, segp)))
        args.append(sel)
    for i in range(pps):
        in_specs.append(pl.BlockSpec((None, 2 * hk, PAGE), lambda b, s, pt, i=i: (pt[b, s * pps + i], 0, 0)))
        args.append(pool_t)
    if mode == "fox":
        for i in range(pps):
            in_specs.append(pl.BlockSpec((None, FOX_HEADS, PAGE), lambda b, s, pt, i=i: (pt[b, s * pps + i], 0, 0)))
            args.append(lf_pool_t)
    in_specs.append(per_b((2 * hk, PAGE)))
    args.append(new_t)
    if mode == "fox":
        tri = _tri_matrix()
        in_specs += [per_b((FOX_HEADS, PAGE)), cst(tri.shape)]
        args += [newlf_t, tri]
    grid_spec = pltpu.PrefetchScalarGridSpec(
        num_scalar_prefetch=1, grid=(nb, nsteps), in_specs=in_specs,
        out_specs=pl.BlockSpec((None, n_rows, hk), lambda b, s, pt: (b, 0, 0)),
        scratch_shapes=[pltpu.VMEM((n_segs, n_rows, hk), F32), pltpu.VMEM((n_rows, segp), F32),
                        pltpu.VMEM((n_rows, segp), F32), pltpu.VMEM((n_rows, segp), F32)])
    return pl.pallas_call(
        functools.partial(_stream_kernel, mode=mode, pps=pps, n_rows=n_rows, hk=hk, q0=q0, n_segs=n_segs),
        grid_spec=grid_spec,
        out_shape=jax.ShapeDtypeStruct((nb, n_rows, hk), F32),
        compiler_params=pltpu.CompilerParams(dimension_semantics=("parallel", "arbitrary"),
                                             vmem_limit_bytes=VMEM_LIMIT),
        name="decode_stream_" + mode,
    )(page_table, *args)


def _nsa_gate_kernel(oc_ref, os_ref, ow_ref, gl_ref, o_ref):
    sig = 1.0 / (1.0 + jnp.exp(-gl_ref[...]))
    for h in range(NSA_HEADS):
        sl = slice(h * HEAD_DIM, (h + 1) * HEAD_DIM)
        o_ref[:, sl] = (oc_ref[:, sl] * sig[:, 3 * h:3 * h + 1] + os_ref[:, sl] * sig[:, 3 * h + 1:3 * h + 2]
                        + ow_ref[:, sl] * sig[:, 3 * h + 2:3 * h + 3]).astype(BF16)


def _nsa_gate(oc, os_, ow, gl):
    return pl.pallas_call(
        _nsa_gate_kernel,
        out_shape=jax.ShapeDtypeStruct(oc.shape, BF16),
        name="nsa_gate",
    )(oc, os_, ow, gl)


def _pool_t(cache, li):
    c = jnp.transpose(cache[li], (0, 2, 3, 4, 1))
    return c.reshape(c.shape[0], -1, c.shape[-1])


def _block_diag_q(q, n_tok, n_kv, n_r):
    b = q.shape[0]
    q5 = q.reshape(b, n_tok, n_kv, n_r, 1, HEAD_DIM)
    eye = jnp.eye(n_kv, dtype=q.dtype).reshape(1, 1, n_kv, 1, n_kv, 1)
    return (q5 * eye).reshape(b, n_tok * n_kv * n_r, n_kv * HEAD_DIM)


def _diag_heads(o, n_tok, n_kv, n_r):
    b = o.shape[0]
    o6 = o.reshape(b, n_tok, n_kv, n_r, n_kv, HEAD_DIM)
    d = jnp.einsum("btgrgd->btgrd", o6)
    return d.reshape(b * n_tok, n_kv * n_r * HEAD_DIM)


def _row_consts(n_tok, n_kv, n_r, slopes, q0):
    t = np.repeat(np.arange(n_tok), n_kv * n_r)
    h = np.tile(np.arange(n_kv * n_r), n_tok)
    sl = np.asarray(slopes, np.float32)[h] if slopes is not None else np.zeros(h.shape, np.float32)
    return jnp.asarray(sl.reshape(-1, 1), F32), jnp.asarray((q0 + t).reshape(-1, 1), I32)


def _new_pages(kv_t, nb, n_tok):
    rows = kv_t.shape[0]
    x = kv_t.reshape(rows, nb, n_tok).transpose(1, 0, 2)
    return jnp.pad(x, ((0, 0), (0, 0), (0, PAGE - n_tok)))


def kernel(x_prompt, x_sample, cache_nsa_cmp_kv, cache_nsa_slc_kv, state_nsa_win_kv, cache_fox_kv, cache_fox_lf,
           cache_moba_kv, page_table, norm_mix, norm_mlp, w_in_even, b_fgt, w_out_even, cmp_pe, cmp_w1, cmp_w2,
           w_in_odd, w_out_odd, w_up, w_down, norm_final):
    nb_p, s_len, _ = x_prompt.shape
    nb_d, n_tok, _ = x_sample.shape
    n_pages = page_table.shape[1]
    past = n_pages * PAGE
    nq_w = NSA_HEADS * HEAD_DIM
    nkv_w = 2 * NSA_KV * HEAD_DIM
    fq_w = FOX_HEADS * HEAD_DIM
    fkv_w = 2 * FOX_HEADS * HEAD_DIM
    mq_w = MOBA_HEADS * HEAD_DIM
    mkv_w = 2 * MOBA_KV * HEAD_DIM
    ngl = 3 * NSA_HEADS

    wte = jnp.transpose(w_in_even[0])
    o_nq, o_ckv, o_skv, o_wkv = 0, nq_w, nq_w + nkv_w, nq_w + 2 * nkv_w
    o_gl = nq_w + 3 * nkv_w
    o_fq = o_gl + ngl
    o_fkv = o_fq + fq_w
    o_fl = o_fkv + fkv_w
    wn_e = jnp.concatenate([wte[o_nq:o_nq + nq_w], wte[o_fq:o_fq + fq_w], wte[o_gl:o_gl + ngl],
                            jnp.zeros((LANES - ngl, D_MODEL), F32)], axis=0)
    wn_e = jnp.transpose(wn_e).astype(BF16)
    wt_e = jnp.concatenate([wte[o_ckv:o_ckv + 3 * nkv_w], wte[o_fkv:o_fkv + fkv_w], wte[o_fl:o_fl + FOX_HEADS]],
                           axis=0).astype(BF16)
    bias_e = b_fgt[0].reshape(FOX_HEADS, 1)
    nat_e = [(0, nq_w, "q"), (nq_w, fq_w, "q"), (nq_w + fq_w, LANES, "f32")]
    r_skv, r_wkv, r_fkv, r_fl = nkv_w, 2 * nkv_w, 3 * nkv_w, 3 * nkv_w + fkv_w
    tr_e_prompt = [(0, nkv_w, "f32"), (r_skv, nkv_w, "f32"), (r_skv, nkv_w, "bf16"), (r_wkv, nkv_w, "f32"),
                   (r_wkv, nkv_w, "bf16c"), (r_fkv, fkv_w, "f32"), (r_fkv, fkv_w, "bf16"), (r_fl, FOX_HEADS, "lf")]
    tr_e_dec = [(0, nkv_w, "f32"), (r_skv, nkv_w, "f32"), (r_wkv, nkv_w, "f32"), (r_fkv, fkv_w, "f32"),
                (r_fl, FOX_HEADS, "lf")]
    wo_t = jnp.transpose(w_in_odd[0])
    wn_o = w_in_odd[0][:, :mq_w].astype(BF16)
    wt_o = wo_t[mq_w:].astype(BF16)
    bias_o = jnp.zeros((8, 1), F32)
    nat_o = [(0, mq_w, "q")]
    tr_o_prompt = [(0, mkv_w, "f32"), (0, mkv_w, "bf16")]
    tr_o_dec = [(0, mkv_w, "f32")]
    woe = w_out_even[0].astype(BF16)
    woe_a, woe_b = woe[:nq_w], woe[nq_w:]
    woo = w_out_odd[0].astype(BF16)
    wup = w_up.astype(BF16)
    wdn = w_down.astype(BF16)
    g_mix = norm_mix.reshape(norm_mix.shape[0], 1, D_MODEL)
    g_mlp = norm_mlp.reshape(norm_mlp.shape[0], 1, D_MODEL)
    g_fin = norm_final.reshape(1, D_MODEL)
    cw = _compress_weights(cmp_pe[0], cmp_w1[0], cmp_w2[0])

    xp = x_prompt.reshape(nb_p * s_len, D_MODEL)
    tm_p = 512
    (nq, fq, gl, ckv_t, skv_t, skv_b, wkv_t, wkv_c, fkv_t, fkv_b, lf_t) = _proj(
        xp, g_mix[0], wn_e, wt_e, bias_e, nat_e, tr_e_prompt, nb_p, s_len, tm_p)
    cmp_tok = _compress(ckv_t, None, cw, nb_p, s_len // PAGE, s_len // PAGE, paged=False)
    o_nsa = _nsa_prompt(nq, gl, cmp_tok, skv_b, wkv_c, nb_p, s_len)
    o_fox = _fox_prompt(fq, fkv_b, lf_t, nb_p, s_len)
    xp = _post(xp, [o_nsa, o_fox], [woe_a, woe_b], g_mlp[0], wup[0], wdn[0], g_fin, False, tm_p)
    mq, mkv_t, mkv_b = _proj(xp, g_mix[1], wn_o, wt_o, bias_o, nat_o, tr_o_prompt, nb_p, s_len, tm_p)
    o_moba = _moba_prompt(mq, mkv_b, nb_p, s_len)
    yp = _post(xp, [o_moba], [woo], g_mlp[1], wup[1], wdn[1], g_fin, True, tm_p)
    y_prompt = yp.reshape(nb_p, s_len, D_MODEL)

    def kv_out(t, n_h):
        b, _, s = t.shape
        return jnp.transpose(t.reshape(b, 2, n_h, HEAD_DIM, s), (0, 4, 1, 2, 3))[None]

    p_cmp = kv_out(ckv_t, NSA_KV)
    p_slc = kv_out(skv_t, NSA_KV)
    wb = min(WINDOW, s_len)
    p_win = kv_out(wkv_t[:, :, s_len - wb:], NSA_KV)
    p_fox = kv_out(fkv_t, FOX_HEADS)
    p_lf = jnp.transpose(lf_t, (0, 2, 1))[None]
    p_moba = kv_out(mkv_t, MOBA_KV)

    td = nb_d * n_tok
    xd = x_sample.reshape(td, D_MODEL)
    (nq_d, fq_d, gl_d, ckv_d, skv_d, wkv_d, fkv_d, lf_d) = _proj(
        xd, g_mix[0], wn_e, wt_e, bias_e, nat_e, tr_e_dec, 1, td, td)

    def kv_out_dec(t, n_h):
        return jnp.transpose(t[0]).reshape(1, nb_d, n_tok, 2, n_h, HEAD_DIM)

    s_cmp = kv_out_dec(ckv_d, NSA_KV)
    s_slc = kv_out_dec(skv_d, NSA_KV)
    s_fox = kv_out_dec(fkv_d, FOX_HEADS)
    s_lf = jnp.transpose(lf_d[0]).reshape(1, nb_d, n_tok, FOX_HEADS)
    win_state_t = jnp.transpose(state_nsa_win_kv[0], (0, 2, 3, 4, 1)).reshape(nb_d, nkv_w, -1)
    wkv_new = wkv_d[0].reshape(nkv_w, nb_d, n_tok).transpose(1, 0, 2)
    win_all = jnp.concatenate([win_state_t, wkv_new], axis=2)
    wbuf = win_state_t.shape[2]
    s_win = jnp.transpose(win_all[:, :, -wbuf:].reshape(nb_d, 2, NSA_KV, HEAD_DIM, wbuf), (0, 4, 1, 2, 3))[None]
    assert wbuf == WINDOW
    wpad = -(-(wbuf + n_tok) // LANES) * LANES
    kwin_t = jnp.pad(win_all, ((0, 0), (0, 0), (0, wpad - wbuf - n_tok)))

    assert (past + n_tok) // CMP_STRIDE == past // CMP_STRIDE
    cmp_pool = _pool_t(cache_nsa_cmp_kv, 0)
    cmp_tok_d = _compress(cmp_pool, page_table, cw, nb_d, n_pages, 16, paged=True)
    n_c = past // CMP_STRIDE - CMP_LEN // CMP_STRIDE + 1
    n_s = -(-(past + n_tok) // SLC_BLOCK)
    tq_d = 8
    q8 = jnp.pad(nq_d.reshape(nb_d, n_tok, nq_w), ((0, 0), (0, tq_d - n_tok), (0, 0)))
    o_cmp8, o_win8, sel8 = _nsa_dec(q8, cmp_tok_d, kwin_t, nb_d, past, n_c, n_s)
    o_cmp_d = o_cmp8[:, :n_tok].reshape(td, nq_w)
    o_win_d = o_win8[:, :n_tok].reshape(td, nq_w)
    nbp = sel8.shape[2] // NSA_KV
    sel_rows = jnp.repeat(sel8[:, :n_tok].reshape(nb_d, n_tok, NSA_KV, 1, nbp), NSA_R, axis=3)
    sel_rows = sel_rows.reshape(nb_d, n_tok * NSA_HEADS, nbp)
    sl_nsa, t_nsa = _row_consts(n_tok, NSA_KV, NSA_R, NSA_SLOPES, past)
    qbd_s = _block_diag_q(nq_d.reshape(nb_d, n_tok, nq_w), n_tok, NSA_KV, NSA_R)
    new_s = _new_pages(skv_d[0][:, :], nb_d, n_tok)
    o_slc_bd = _stream("slc", qbd_s, sl_nsa, t_nsa, sel_rows, _pool_t(cache_nsa_slc_kv, 0), None, new_s, None,
                       page_table, past, 8)
    o_slc_d = _diag_heads(o_slc_bd, n_tok, NSA_KV, NSA_R)
    o_nsa_d = _nsa_gate(o_cmp_d, o_slc_d, o_win_d, gl_d)
    sl_fox, t_fox = _row_consts(n_tok, FOX_HEADS, 1, None, past)
    qbd_f = _block_diag_q(fq_d.reshape(nb_d, n_tok, fq_w), n_tok, FOX_HEADS, 1)
    new_f = _new_pages(fkv_d[0], nb_d, n_tok)
    newlf = _new_pages(lf_d[0], nb_d, n_tok)
    lf_pool = jnp.transpose(cache_fox_lf[0], (0, 2, 1))
    o_fox_bd = _stream("fox", qbd_f, sl_fox, t_fox, None, _pool_t(cache_fox_kv, 0), lf_pool, new_f, newlf,
                       page_table, past, 4)
    o_fox_d = _diag_heads(o_fox_bd, n_tok, FOX_HEADS, 1).astype(BF16)
    xd = _post(xd, [o_nsa_d, o_fox_d], [woe_a, woe_b], g_mlp[0], wup[0], wdn[0], g_fin, False, td)
    mq_d, mkv_d = _proj(xd, g_mix[1], wn_o, wt_o, bias_o, nat_o, tr_o_dec, 1, td, td)
    s_moba = kv_out_dec(mkv_d, MOBA_KV)
    assert past % MOBA_BLOCK == 0 and MOBA_BLOCK == 2 * PAGE
    sl_m, t_m = _row_consts(n_tok, MOBA_KV, MOBA_R, MOBA_SLOPES, past)
    qbd_m = _block_diag_q(mq_d.reshape(nb_d, n_tok, mq_w), n_tok, MOBA_KV, MOBA_R)
    new_m = _new_pages(mkv_d[0], nb_d, n_tok)
    o_moba_bd = _stream("moba", qbd_m, sl_m, t_m, None, _pool_t(cache_moba_kv, 0), None, new_m, None,
                        page_table, past, MOBA_BLOCK // PAGE)
    o_moba_d = _diag_heads(o_moba_bd, n_tok, MOBA_KV, MOBA_R).astype(BF16)
    yd = _post(xd, [o_moba_d], [woo], g_mlp[1], wup[1], wdn[1], g_fin, True, td)
    y_sample = yd.reshape(nb_d, n_tok, D_MODEL)

    return (y_prompt, y_sample, p_cmp, s_cmp, p_slc, s_slc, p_win, s_win, p_fox, s_fox, p_lf, s_lf, p_moba, s_moba)
```

```python
import functools
import math

import numpy as np
import jax
import jax.numpy as jnp
from jax import lax
from jax.experimental import pallas as pl
from jax.experimental.pallas import tpu as pltpu

F32 = jnp.float32
BF16 = jnp.bfloat16
I32 = jnp.int32

D_MODEL = 1024
HEAD_DIM = 64
NSA_HEADS = 8
NSA_KV = 2
NSA_R = NSA_HEADS // NSA_KV
CMP_LEN = 32
CMP_STRIDE = 16
CMP_HIDDEN = 128
SLC_BLOCK = 64
SLC_TOPN = 16
WINDOW = 512
FOX_HEADS = 8
MOBA_HEADS = 16
MOBA_KV = 4
MOBA_R = MOBA_HEADS // MOBA_KV
MOBA_BLOCK = 256
MOBA_TOPK = 3
D_FF = 4 * D_MODEL
PAGE = 128
RMS_EPS = 1e-6
NEG = -1e30
SCALE = HEAD_DIM ** -0.5
LANES = 128
KEY_STEP = 512
VMEM_LIMIT = 56 * 1024 * 1024

NT_DIMS = (((1,), (1,)), ((), ()))


def _alibi(n):
    return [float(np.float32(2.0 ** (-8.0 * (i + 1) / n))) for i in range(n)]


NSA_SLOPES = _alibi(NSA_HEADS)
MOBA_SLOPES = _alibi(MOBA_HEADS)


def _div(x, n):
    assert n & (n - 1) == 0
    return jnp.right_shift(x, int(math.log2(n)))


def _mod(x, n):
    assert n & (n - 1) == 0
    return jnp.bitwise_and(x, n - 1)


def _dot(a, b):
    return jnp.dot(a, b, preferred_element_type=F32)


def _dot_nt(a, b):
    return lax.dot_general(a, b, NT_DIMS, preferred_element_type=F32)


def _split3(x):
    hi = x.astype(BF16)
    r1 = x - hi.astype(F32)
    mid = r1.astype(BF16)
    lo = (r1 - mid.astype(F32)).astype(BF16)
    return hi, mid, lo


def _dot01(x, mat, nt=False):
    n = x.shape[0]
    st = jnp.concatenate(_split3(x), axis=0)
    y = _dot_nt(st, mat) if nt else _dot(st, mat)
    return y[0:n] + y[n:2 * n] + y[2 * n:3 * n]


def _softmax_parts(s, valid):
    s = jnp.where(valid, s, NEG)
    m = jnp.max(s, axis=-1, keepdims=True)
    p = jnp.where(valid, jnp.exp(s - m), 0.0)
    l = jnp.maximum(jnp.sum(p, axis=-1, keepdims=True), 1e-30)
    return p, l, m


def _softmax_bias(logits):
    m = jnp.max(logits, axis=-1, keepdims=True)
    p = jnp.exp(logits - m)
    return p, jnp.sum(p, axis=-1, keepdims=True), m


def _topk_mask(v, nsel, nb):
    lane = lax.broadcasted_iota(I32, v.shape, 1)
    cnt = jnp.zeros(v.shape, I32)
    for k in range(nb):
        col = v[:, k:k + 1]
        beats = (col > v) | ((col == v) & (lane > k))
        cnt = cnt + jnp.where(beats, 1, 0)
    return (cnt < nsel) & (v > -jnp.inf)


def _stack_heads(q, g, nr):
    return jnp.concatenate([q[:, (g * nr + r) * HEAD_DIM:(g * nr + r + 1) * HEAD_DIM] for r in range(nr)], axis=0)


def _row_slopes(r4, slopes, g, nr):
    out = jnp.zeros(r4.shape, F32)
    for r in range(nr):
        out = jnp.where(r4 == r, slopes[g * nr + r], out)
    return out


_CHUNKED = {"bf16c": LANES, "bf16c2": MOBA_BLOCK}


def _proj_kernel(x_ref, g_ref, wn_ref, wt_ref, b_ref, *outs, nat, tr):
    x = x_ref[...]
    ms = jnp.mean(x * x, axis=-1, keepdims=True)
    h = (x * lax.rsqrt(ms + RMS_EPS)) * g_ref[...]
    hb = h.astype(BF16)
    yn = _dot(hb, wn_ref[...])
    yt = _dot_nt(wt_ref[...], hb)
    k = 0
    for (c0, w, kind) in nat:
        v = yn[:, c0:c0 + w]
        if kind == "q":
            outs[k][...] = (v * SCALE).astype(BF16)
        else:
            outs[k][...] = v
        k += 1
    for (r0, n, kind) in tr:
        v = yt[r0:r0 + n, :]
        if kind == "f32":
            outs[k][...] = v
        elif kind == "bf16":
            outs[k][...] = v.astype(BF16)
        elif kind in _CHUNKED:
            cw = _CHUNKED[kind]
            vb = v.astype(BF16)
            for c in range(v.shape[1] // cw):
                outs[k][c] = vb[:, c * cw:(c + 1) * cw]
        else:
            z = v + b_ref[...]
            outs[k][...] = jnp.minimum(z, 0.0) - jnp.log(1.0 + jnp.exp(-jnp.abs(z)))
        k += 1


def _proj(x2d, gain, wn, wt, bias, nat, tr, nb, s_len, tm):
    t_tot = x2d.shape[0]
    tpb = s_len // tm
    assert nb * s_len == t_tot
    out_shape, out_specs = [], []
    for (c0, w, kind) in nat:
        out_shape.append(jax.ShapeDtypeStruct((t_tot, w), BF16 if kind == "q" else F32))
        out_specs.append(pl.BlockSpec((tm, w), lambda i: (i, 0)))
    for (r0, n, kind) in tr:
        if kind in _CHUNKED:
            cw = _CHUNKED[kind]
            out_shape.append(jax.ShapeDtypeStruct((nb, s_len // cw, n, cw), BF16))
            out_specs.append(pl.BlockSpec((None, tm // cw, n, cw), lambda i: (i // tpb, i % tpb, 0, 0)))
        else:
            out_shape.append(jax.ShapeDtypeStruct((nb, n, s_len), BF16 if kind == "bf16" else F32))
            out_specs.append(pl.BlockSpec((None, n, tm), lambda i: (i // tpb, 0, i % tpb)))
    return pl.pallas_call(
        functools.partial(_proj_kernel, nat=tuple(nat), tr=tuple(tr)),
        grid=(t_tot // tm,),
        in_specs=[pl.BlockSpec((tm, D_MODEL), lambda i: (i, 0)),
                  pl.BlockSpec((1, D_MODEL), lambda i: (0, 0)),
                  pl.BlockSpec(wn.shape, lambda i: (0, 0)),
                  pl.BlockSpec(wt.shape, lambda i: (0, 0)),
                  pl.BlockSpec(bias.shape, lambda i: (0, 0))],
        out_specs=out_specs,
        out_shape=out_shape,
        compiler_params=pltpu.CompilerParams(dimension_semantics=("parallel",), vmem_limit_bytes=VMEM_LIMIT),
        name="norm_proj",
    )(x2d, gain, wn, wt, bias)


def _gelu_tanh(x):
    return 0.5 * x * (1.0 + jnp.tanh(math.sqrt(2.0 / math.pi) * (x + 0.044715 * (x * x * x))))


def _compress_kernel(*refs, npg, n_prefetch):
    refs = refs[n_prefetch:]
    pages = refs[:npg]
    pt_ref, pea_ref, peb_ref, w1a_ref, w1b_ref, w2_ref, out_ref, t_scr, carry = refs[npg:]
    sub = PAGE // CMP_STRIDE
    m = npg * sub

    @pl.when(pl.program_id(1) == 0)
    def _():
        carry[...] = jnp.zeros(carry.shape, F32)

    for k in range(npg):
        xb = pages[k][...].astype(BF16)
        tt = _dot_nt(pt_ref[...], xb)
        for p in range(CMP_STRIDE):
            t_scr[p, k * sub:(k + 1) * sub, :] = tt[p * sub:(p + 1) * sub, :]
    acc_a = jnp.zeros((m, w1a_ref.shape[2]), F32)
    acc_b = jnp.zeros((m, w1a_ref.shape[2]), F32)
    for p in range(CMP_STRIDE):
        tp = t_scr[p]
        acc_a = acc_a + _dot((tp + pea_ref[p]).astype(BF16), w1a_ref[p])
        acc_b = acc_b + _dot((tp + peb_ref[p]).astype(BF16), w1b_ref[p])
    rolled = pltpu.roll(acc_a, 1, axis=0)
    row = lax.broadcasted_iota(I32, acc_a.shape, 0)
    hid = jnp.where(row == 0, carry[...], rolled) + acc_b
    carry[...] = acc_a[m - 1:m, :]
    act = _gelu_tanh(hid)
    out_ref[...] = _dot(act.astype(BF16), w2_ref[...])


def _compress_weights(cmp_pe, cmp_w1, cmp_w2):
    nkg = 2 * NSA_KV
    eye = jnp.eye(nkg, dtype=F32)
    w1 = cmp_w1.reshape(2, CMP_LEN, HEAD_DIM, CMP_HIDDEN)
    w1 = jnp.repeat(w1, NSA_KV, axis=0)
    bd = jnp.einsum("kpdh,kl->pkdlh", w1, eye).reshape(CMP_LEN, nkg * HEAD_DIM, nkg * CMP_HIDDEN).astype(BF16)
    w2 = jnp.repeat(cmp_w2, NSA_KV, axis=0)
    w2bd = jnp.einsum("khd,kl->khld", w2, eye).reshape(nkg * CMP_HIDDEN, nkg * HEAD_DIM).astype(BF16)
    pe = jnp.repeat(cmp_pe, NSA_KV, axis=0)
    pe = pe.transpose(1, 0, 2).reshape(CMP_LEN, 1, nkg * HEAD_DIM)
    return bd[:CMP_STRIDE], bd[CMP_STRIDE:], w2bd, pe[:CMP_STRIDE], pe[CMP_STRIDE:]


def _perm_matrix():
    sub = PAGE // CMP_STRIDE
    pt = np.zeros((PAGE, PAGE), np.float32)
    for p in range(CMP_STRIDE):
        for j in range(sub):
            pt[p * sub + j, CMP_STRIDE * j + p] = 1.0
    return jnp.asarray(pt, BF16)


def _compress(pages_arr, page_table, cw, nb, n_pages, npg, paged):
    w1a, w1b, w2bd, pea, peb = cw
    rows = 2 * NSA_KV * HEAD_DIM
    m = npg * (PAGE // CMP_STRIDE)
    n_chunks = n_pages // npg
    page_specs = []
    for k in range(npg):
        if paged:
            page_specs.append(pl.BlockSpec((None, rows, PAGE), lambda b, c, pt, k=k: (pt[b, c * npg + k], 0, 0)))
        else:
            page_specs.append(pl.BlockSpec((None, rows, PAGE), lambda b, c, k=k: (b, 0, c * npg + k)))

    def const(shape):
        nd = len(shape)
        if paged:
            return pl.BlockSpec(shape, lambda b, c, pt: (0,) * nd)
        return pl.BlockSpec(shape, lambda b, c: (0,) * nd)

    perm = _perm_matrix()
    consts = [perm, pea, peb, w1a, w1b, w2bd]
    in_specs = page_specs + [const(a.shape) for a in consts]
    if paged:
        out_spec = pl.BlockSpec((None, m, rows), lambda b, c, pt: (b, c, 0))
    else:
        out_spec = pl.BlockSpec((None, m, rows), lambda b, c: (b, c, 0))
    n_prefetch = 1 if paged else 0
    grid_spec = pltpu.PrefetchScalarGridSpec(
        num_scalar_prefetch=n_prefetch, grid=(nb, n_chunks), in_specs=in_specs, out_specs=out_spec,
        scratch_shapes=[pltpu.VMEM((CMP_STRIDE, m, rows), F32), pltpu.VMEM((1, w1a.shape[2]), F32)])
    args = ([page_table] if paged else []) + [pages_arr] * npg + consts
    return pl.pallas_call(
        functools.partial(_compress_kernel, npg=npg, n_prefetch=n_prefetch),
        grid_spec=grid_spec,
        out_shape=jax.ShapeDtypeStruct((nb, n_chunks * m, rows), F32),
        compiler_params=pltpu.CompilerParams(dimension_semantics=("parallel", "arbitrary"),
                                             vmem_limit_bytes=VMEM_LIMIT),
        name="nsa_compress",
    )(*args)


def _cmp_to_slc(n_rows, n_c, n_s, n_cols):
    m = np.zeros((n_rows, n_cols), np.float32)
    c = np.arange(n_c)[:, None]
    j = np.arange(n_s)[None, :]
    lo = c * CMP_STRIDE
    hi = lo + CMP_LEN
    m[1:n_c + 1, :n_s] = ((lo < (j + 1) * SLC_BLOCK) & (hi > j * SLC_BLOCK)).astype(np.float32)
    return jnp.asarray(m, BF16)


def _cmp_branch(qg4, cmp_tok, mcs, g, tq, t4, slope4, n_c):
    ck = cmp_tok[:, g * HEAD_DIM:(g + 1) * HEAD_DIM].astype(BF16)
    v0 = (NSA_KV + g) * HEAD_DIM
    cv = cmp_tok[:, v0:v0 + HEAD_DIM].astype(BF16)
    sc = _dot_nt(qg4, ck)
    col = lax.broadcasted_iota(I32, sc.shape, 1)
    dc = t4 - ((col - 1) * CMP_STRIDE + (CMP_LEN - 1))
    valid = (dc >= 0) & (col >= 1) & (col <= n_c)
    p, l, _ = _softmax_parts(sc - slope4 * dc.astype(F32), valid)
    pn = p / l
    o = _dot(pn.astype(BF16), cv)
    psum = pn[0:tq]
    for r in range(1, NSA_R):
        psum = psum + pn[r * tq:(r + 1) * tq]
    return o, _dot01(psum, mcs)


def _slc_select(imp, t1, n_s):
    blk = lax.broadcasted_iota(I32, imp.shape, 1)
    tb = _div(t1, SLC_BLOCK)
    avail = (blk <= tb) & (blk < n_s)
    forced = (blk == 0) | (blk == tb) | (blk == tb - 1)
    v = jnp.where(forced, jnp.inf, jnp.where(avail, imp, -jnp.inf))
    return _topk_mask(v, min(SLC_TOPN, n_s), n_s)


def _nsa_prompt_kernel(q_ref, gl_ref, cmp_ref, mcs_ref, ks_ref, kw_ref, o_ref, oslc_scr, *, tq, s_len, n_c, kstep):
    qi = pl.program_id(1)
    s0 = qi * tq
    q = q_ref[...]
    sig = 1.0 / (1.0 + jnp.exp(-gl_ref[...]))
    cmp_tok = cmp_ref[...]
    mcs = mcs_ref[...]
    n_s = s_len // SLC_BLOCK
    row4 = lax.broadcasted_iota(I32, (NSA_R * tq, 1), 0)
    t4 = s0 + _mod(row4, tq)
    r4 = _div(row4, tq)
    t1 = s0 + lax.broadcasted_iota(I32, (tq, 1), 0)
    nwc = WINDOW // LANES + 1
    wlen = nwc * LANES
    wp1 = (qi - (nwc - 1)) * LANES + lax.broadcasted_iota(I32, (tq, wlen), 1)
    dw1 = t1 - wp1
    wmask1 = jnp.where((wp1 >= 0) & (dw1 >= 0) & (dw1 < WINDOW), 0.0, NEG)
    dwf1 = dw1.astype(F32)
    chunks = [kw_ref[jnp.maximum(qi - (nwc - 1) + j, 0)] for j in range(nwc)]
    eb = lax.broadcasted_iota(I32, (LANES, s_len), 0)
    ep = lax.broadcasted_iota(I32, (LANES, s_len), 1)
    expand = jnp.where(_div(ep, SLC_BLOCK) == eb, 1.0, 0.0).astype(BF16)
    n_var = s_len // kstep
    outs = [None] * NSA_HEADS
    for g in range(NSA_KV):
        qg4 = _stack_heads(q, g, NSA_R)
        slope4 = _row_slopes(r4, NSA_SLOPES, g, NSA_R)
        o_cmp, imp = _cmp_branch(qg4, cmp_tok, mcs, g, tq, t4, slope4, n_c)
        sel = _slc_select(imp, t1, n_s)
        selb = jnp.where(sel, 1.0, 0.0).astype(BF16)

        def slc_body(kmax, g=g, qg4=qg4, selb=selb):
            mexp = _dot(selb, expand[:, :kmax])
            pos1 = lax.broadcasted_iota(I32, (tq, kmax), 1)
            ds1 = t1 - pos1
            bias1 = jnp.where((mexp > 0.5) & (ds1 >= 0), 0.0, NEG)
            dsf1 = ds1.astype(F32)
            bias4 = jnp.concatenate([bias1 - NSA_SLOPES[g * NSA_R + r] * dsf1 for r in range(NSA_R)], axis=0)
            kt = ks_ref[g * HEAD_DIM:(g + 1) * HEAD_DIM, 0:kmax]
            vt = ks_ref[(NSA_KV + g) * HEAD_DIM:(NSA_KV + g + 1) * HEAD_DIM, 0:kmax]
            p, l, _ = _softmax_bias(_dot(qg4, kt) + bias4)
            oslc_scr[...] = _dot_nt(p.astype(BF16), vt) / l

        for c in range(n_var):
            pl.when(_div(s0, kstep) == c)(functools.partial(slc_body, kstep * (c + 1)))
        o_slc = oslc_scr[...]
        kwin = jnp.concatenate([c[g * HEAD_DIM:(g + 1) * HEAD_DIM, :] for c in chunks], axis=1)
        vwin = jnp.concatenate([c[(NSA_KV + g) * HEAD_DIM:(NSA_KV + g + 1) * HEAD_DIM, :] for c in chunks], axis=1)
        wbias4 = jnp.concatenate([wmask1 - NSA_SLOPES[g * NSA_R + r] * dwf1 for r in range(NSA_R)], axis=0)
        p, l, _ = _softmax_bias(_dot(qg4, kwin) + wbias4)
        o_win = _dot_nt(p.astype(BF16), vwin) / l
        for r in range(NSA_R):
            h = g * NSA_R + r
            sl = slice(r * tq, (r + 1) * tq)
            outs[h] = (o_cmp[sl] * sig[:, 3 * h:3 * h + 1] + o_slc[sl] * sig[:, 3 * h + 1:3 * h + 2]
                       + o_win[sl] * sig[:, 3 * h + 2:3 * h + 3])
    o_ref[...] = jnp.concatenate(outs, axis=1).astype(BF16)


def _nsa_prompt(nq, gl, cmp_tok, ks_b, kw_c, nb, s_len):
    tq = LANES
    nqt = s_len // tq
    n_c = s_len // CMP_STRIDE - CMP_LEN // CMP_STRIDE + 1
    mcs = _cmp_to_slc(cmp_tok.shape[1], n_c, s_len // SLC_BLOCK, LANES)
    rows = 2 * NSA_KV * HEAD_DIM
    return pl.pallas_call(
        functools.partial(_nsa_prompt_kernel, tq=tq, s_len=s_len, n_c=n_c, kstep=KEY_STEP),
        scratch_shapes=[pltpu.VMEM((NSA_R * tq, HEAD_DIM), F32)],
        grid=(nb, nqt),
        in_specs=[pl.BlockSpec((tq, NSA_HEADS * HEAD_DIM), lambda b, i: (b * nqt + i, 0)),
                  pl.BlockSpec((tq, LANES), lambda b, i: (b * nqt + i, 0)),
                  pl.BlockSpec((None,) + cmp_tok.shape[1:], lambda b, i: (b, 0, 0)),
                  pl.BlockSpec(mcs.shape, lambda b, i: (0, 0)),
                  pl.BlockSpec((None, rows, s_len), lambda b, i: (b, 0, 0)),
                  pl.BlockSpec((None, s_len // LANES, rows, LANES), lambda b, i: (b, 0, 0, 0))],
        out_specs=pl.BlockSpec((tq, NSA_HEADS * HEAD_DIM), lambda b, i: (b * nqt + i, 0)),
        out_shape=jax.ShapeDtypeStruct((nb * s_len, NSA_HEADS * HEAD_DIM), BF16),
        compiler_params=pltpu.CompilerParams(dimension_semantics=("parallel", "arbitrary"),
                                             vmem_limit_bytes=VMEM_LIMIT),
        name="nsa_prompt",
    )(nq, gl, cmp_tok, mcs, ks_b, kw_c)


def _tri_matrix():
    i = np.arange(LANES)
    return jnp.asarray((i[:, None] <= i[None, :]).astype(np.float32), BF16)


def _fox_prompt_kernel(q_ref, kv_ref, lf_ref, u_ref, o_ref, c_scr, *, tq, s_len, kstep):
    qi = pl.program_id(1)

    @pl.when(qi == 0)
    def _():
        carry = jnp.zeros((FOX_HEADS, 1), F32)
        for blk in range(s_len // LANES):
            cs = _dot01(lf_ref[:, blk * LANES:(blk + 1) * LANES], u_ref[...]) + carry
            c_scr[:, blk * LANES:(blk + 1) * LANES] = cs
            carry = cs[:, LANES - 1:LANES]

    s0 = qi * tq
    t = s0 + lax.broadcasted_iota(I32, (tq, 1), 0)
    nh = FOX_HEADS

    def body(kmax):
        pos = lax.broadcasted_iota(I32, (tq, kmax), 1)
        causal = jnp.where(pos <= t, 0.0, NEG)
        outs = []
        for h in range(nh):
            qh = q_ref[:, h * HEAD_DIM:(h + 1) * HEAD_DIM]
            kt = kv_ref[h * HEAD_DIM:(h + 1) * HEAD_DIM, 0:kmax]
            vt = kv_ref[(nh + h) * HEAD_DIM:(nh + h + 1) * HEAD_DIM, 0:kmax]
            p, l, _ = _softmax_bias(_dot(qh, kt) + (causal - c_scr[h:h + 1, 0:kmax]))
            outs.append(_dot_nt(p.astype(BF16), vt) / l)
        o_ref[...] = jnp.concatenate(outs, axis=1).astype(BF16)

    for c in range(s_len // kstep):
        pl.when(_div(s0, kstep) == c)(functools.partial(body, kstep * (c + 1)))


def _fox_prompt(fq, fkv_b, lf_t, nb, s_len):
    tq = LANES
    nqt = s_len // tq
    rows = 2 * FOX_HEADS * HEAD_DIM
    tri = _tri_matrix()
    return pl.pallas_call(
        functools.partial(_fox_prompt_kernel, tq=tq, s_len=s_len, kstep=KEY_STEP),
        grid=(nb, nqt),
        in_specs=[pl.BlockSpec((tq, FOX_HEADS * HEAD_DIM), lambda b, i: (b * nqt + i, 0)),
                  pl.BlockSpec((None, rows, s_len), lambda b, i: (b, 0, 0)),
                  pl.BlockSpec((None, FOX_HEADS, s_len), lambda b, i: (b, 0, 0)),
                  pl.BlockSpec(tri.shape, lambda b, i: (0, 0))],
        out_specs=pl.BlockSpec((tq, FOX_HEADS * HEAD_DIM), lambda b, i: (b * nqt + i, 0)),
        out_shape=jax.ShapeDtypeStruct((nb * s_len, FOX_HEADS * HEAD_DIM), BF16),
        scratch_shapes=[pltpu.VMEM((FOX_HEADS, s_len), F32)],
        compiler_params=pltpu.CompilerParams(dimension_semantics=("parallel", "arbitrary"),
                                             vmem_limit_bytes=VMEM_LIMIT),
        name="fox_prompt",
    )(fq, fkv_b, lf_t, tri)


def _moba_prompt_kernel(q_ref, kv_ref, o_ref, km_scr, o_scr, m_scr, l_scr, num_scr, *, tq, s_len):
    qi = pl.program_id(1)
    n_b = s_len // MOBA_BLOCK
    nrow = MOBA_R * tq
    seg_lane = lax.broadcasted_iota(I32, (nrow, LANES), 1)
    own = n_b

    @pl.when(qi == 0)
    def _():
        ones = jnp.full((MOBA_BLOCK, LANES), 1.0 / MOBA_BLOCK, BF16)
        lane = lax.broadcasted_iota(I32, (HEAD_DIM, LANES), 1)
        for g in range(MOBA_KV):
            km = jnp.zeros((HEAD_DIM, LANES), F32)
            for j in range(n_b):
                km = jnp.where(lane == j, _dot(kv_ref[j, g * HEAD_DIM:(g + 1) * HEAD_DIM, :], ones), km)
            km_scr[g] = km.astype(BF16)

    s0 = qi * tq
    tb = _div(s0, MOBA_BLOCK)
    q = q_ref[...]
    row4 = lax.broadcasted_iota(I32, (nrow, 1), 0)
    i4 = _mod(row4, tq)
    t4 = s0 + i4
    tl4 = t4 - tb * MOBA_BLOCK
    r4 = _div(row4, tq)
    lpos = lax.broadcasted_iota(I32, (nrow, MOBA_BLOCK), 1)
    lposf = lpos.astype(F32)
    own_mask = jnp.where(lpos <= tl4, 0.0, NEG)
    outs = [None] * MOBA_HEADS
    for g in range(MOBA_KV):
        qg4 = _stack_heads(q, g, MOBA_R)
        slope4 = _row_slopes(r4, MOBA_SLOPES, g, MOBA_R)
        gate = _dot(qg4, km_scr[g])
        sel = _topk_mask(jnp.where(seg_lane < tb, gate, -jnp.inf), min(MOBA_TOPK, n_b), n_b)
        bias_loc = slope4 * lposf
        m_scr[...] = jnp.zeros(m_scr.shape, F32)
        l_scr[...] = jnp.zeros(l_scr.shape, F32)

        def partial_block(j, slot, extra, g=g, qg4=qg4, bias_loc=bias_loc):
            kt = kv_ref[j, g * HEAD_DIM:(g + 1) * HEAD_DIM, :]
            vt = kv_ref[j, (MOBA_KV + g) * HEAD_DIM:(MOBA_KV + g + 1) * HEAD_DIM, :]
            logits = _dot(qg4, kt) + bias_loc
            if extra is not None:
                logits = logits + extra
            p, l, m = _softmax_bias(logits)
            o_scr[slot] = _dot_nt(p.astype(BF16), vt)
            m_scr[...] = jnp.where(seg_lane == slot, m, m_scr[...])
            l_scr[...] = jnp.where(seg_lane == slot, l, l_scr[...])

        for j in range(n_b):
            pl.when(j < tb)(functools.partial(partial_block, j, j, None))
        partial_block(tb, own, own_mask)
        blk0 = jnp.where(seg_lane == own, tb, seg_lane) * MOBA_BLOCK
        madj = m_scr[...] + slope4 * (blk0 - t4).astype(F32)
        valid = sel | (seg_lane == own)
        mx = jnp.max(jnp.where(valid, madj, NEG), axis=-1, keepdims=True)
        w = jnp.where(valid, jnp.exp(madj - mx), 0.0)
        den = jnp.sum(w * l_scr[...], axis=-1, keepdims=True)
        num_scr[...] = w[:, own:own + 1] * o_scr[own]

        def add_block(j, w=w):
            num_scr[...] += w[:, j:j + 1] * o_scr[j]

        for j in range(n_b):
            pl.when(j < tb)(functools.partial(add_block, j))
        o = num_scr[...] / den
        for r in range(MOBA_R):
            outs[g * MOBA_R + r] = o[r * tq:(r + 1) * tq]
    o_ref[...] = jnp.concatenate(outs, axis=1).astype(BF16)


def _moba_prompt(mq, mkv_c, nb, s_len):
    tq = LANES
    nqt = s_len // tq
    rows = 2 * MOBA_KV * HEAD_DIM
    n_b = s_len // MOBA_BLOCK
    nrow = MOBA_R * tq
    return pl.pallas_call(
        functools.partial(_moba_prompt_kernel, tq=tq, s_len=s_len),
        grid=(nb, nqt),
        in_specs=[pl.BlockSpec((tq, MOBA_HEADS * HEAD_DIM), lambda b, i: (b * nqt + i, 0)),
                  pl.BlockSpec((None, n_b, rows, MOBA_BLOCK), lambda b, i: (b, 0, 0, 0))],
        out_specs=pl.BlockSpec((tq, MOBA_HEADS * HEAD_DIM), lambda b, i: (b * nqt + i, 0)),
        out_shape=jax.ShapeDtypeStruct((nb * s_len, MOBA_HEADS * HEAD_DIM), BF16),
        scratch_shapes=[pltpu.VMEM((MOBA_KV, HEAD_DIM, LANES), BF16),
                        pltpu.VMEM((n_b + 1, nrow, HEAD_DIM), F32),
                        pltpu.VMEM((nrow, LANES), F32), pltpu.VMEM((nrow, LANES), F32),
                        pltpu.VMEM((nrow, HEAD_DIM), F32)],
        compiler_params=pltpu.CompilerParams(dimension_semantics=("parallel", "arbitrary"),
                                             vmem_limit_bytes=VMEM_LIMIT),
        name="moba_prompt",
    )(mq, mkv_c)


def _post_kernel(*refs, n_o, final):
    x_ref = refs[0]
    o_refs = refs[1:1 + n_o]
    wo_refs = refs[1 + n_o:1 + 2 * n_o]
    g_ref, wup_ref, wdn_ref, gf_ref, out_ref, x1_scr, h_scr, acc_scr = refs[1 + 2 * n_o:]
    j = pl.program_id(1)

    @pl.when(j == 0)
    def _():
        x1 = x_ref[...]
        for o_ref, wo_ref in zip(o_refs, wo_refs):
            x1 = x1 + _dot(o_ref[...], wo_ref[...])
        x1_scr[...] = x1
        ms = jnp.mean(x1 * x1, axis=-1, keepdims=True)
        h_scr[...] = ((x1 * lax.rsqrt(ms + RMS_EPS)) * g_ref[...]).astype(BF16)
        acc_scr[...] = jnp.zeros(acc_scr.shape, F32)

    u = jnp.maximum(_dot(h_scr[...], wup_ref[...]), 0.0)
    acc_scr[...] += _dot((u * u).astype(BF16), wdn_ref[...])

    @pl.when(j == pl.num_programs(1) - 1)
    def _():
        y = x1_scr[...] + acc_scr[...]
        if final:
            ms = jnp.mean(y * y, axis=-1, keepdims=True)
            y = (y * lax.rsqrt(ms + RMS_EPS)) * gf_ref[...]
        out_ref[...] = y


def _post(x2d, o_list, wo_list, g_mlp, w_up, w_down, g_final, final, tm, tf=1024):
    t_tot = x2d.shape[0]
    n_o = len(o_list)
    in_specs = [pl.BlockSpec((tm, D_MODEL), lambda i, j: (i, 0))]
    in_specs += [pl.BlockSpec((tm, o.shape[1]), lambda i, j: (i, 0)) for o in o_list]
    in_specs += [pl.BlockSpec(w.shape, lambda i, j: (0, 0)) for w in wo_list]
    in_specs += [pl.BlockSpec((1, D_MODEL), lambda i, j: (0, 0)),
                 pl.BlockSpec((D_MODEL, tf), lambda i, j: (0, j)),
                 pl.BlockSpec((tf, D_MODEL), lambda i, j: (j, 0)),
                 pl.BlockSpec((1, D_MODEL), lambda i, j: (0, 0))]
    return pl.pallas_call(
        functools.partial(_post_kernel, n_o=n_o, final=final),
        grid=(t_tot // tm, D_FF // tf),
        in_specs=in_specs,
        out_specs=pl.BlockSpec((tm, D_MODEL), lambda i, j: (i, 0)),
        out_shape=jax.ShapeDtypeStruct((t_tot, D_MODEL), F32),
        scratch_shapes=[pltpu.VMEM((tm, D_MODEL), F32), pltpu.VMEM((tm, D_MODEL), BF16),
                        pltpu.VMEM((tm, D_MODEL), F32)],
        compiler_params=pltpu.CompilerParams(dimension_semantics=("parallel", "arbitrary"),
                                             vmem_limit_bytes=VMEM_LIMIT),
        name="post_mlp",
    )(x2d, *o_list, *wo_list, g_mlp, w_up, w_down, g_final)


def _nsa_dec_kernel(q_ref, cmp_ref, mcs_ref, kw_ref, ocmp_ref, owin_ref, sel_ref, *, tq, q0, n_c, n_s):
    q = q_ref[...]
    cmp_tok = cmp_ref[...]
    mcs = mcs_ref[...]
    row4 = lax.broadcasted_iota(I32, (NSA_R * tq, 1), 0)
    t4 = q0 + _mod(row4, tq)
    r4 = _div(row4, tq)
    t1 = q0 + lax.broadcasted_iota(I32, (tq, 1), 0)
    nbp = mcs.shape[1]
    wlen = kw_ref.shape[1]
    for g in range(NSA_KV):
        qg4 = _stack_heads(q, g, NSA_R)
        slope4 = _row_slopes(r4, NSA_SLOPES, g, NSA_R)
        o_cmp, imp = _cmp_branch(qg4, cmp_tok, mcs, g, tq, t4, slope4, n_c)
        sel = _slc_select(imp, t1, n_s)
        sel_ref[:, g * nbp:(g + 1) * nbp] = jnp.where(sel, 1.0, 0.0)
        kwin = kw_ref[g * HEAD_DIM:(g + 1) * HEAD_DIM, :].astype(BF16)
        vwin = kw_ref[(NSA_KV + g) * HEAD_DIM:(NSA_KV + g + 1) * HEAD_DIM, :].astype(BF16)
        wp = (q0 - WINDOW) + lax.broadcasted_iota(I32, (NSA_R * tq, wlen), 1)
        dw = t4 - wp
        valid = (wp >= 0) & (dw >= 0) & (dw < WINDOW)
        p, l, _ = _softmax_parts(_dot(qg4, kwin) - slope4 * dw.astype(F32), valid)
        o_win = _dot_nt(p.astype(BF16), vwin) / l
        for r in range(NSA_R):
            h = g * NSA_R + r
            ocmp_ref[:, h * HEAD_DIM:(h + 1) * HEAD_DIM] = o_cmp[r * tq:(r + 1) * tq]
            owin_ref[:, h * HEAD_DIM:(h + 1) * HEAD_DIM] = o_win[r * tq:(r + 1) * tq]


def _nsa_dec(q8, cmp_tok, kwin_t, nb, q0, n_c, n_s):
    tq = q8.shape[1]
    nbp = -(-n_s // LANES) * LANES
    mcs = _cmp_to_slc(cmp_tok.shape[1], n_c, n_s, nbp)
    width = NSA_HEADS * HEAD_DIM
    return pl.pallas_call(
        functools.partial(_nsa_dec_kernel, tq=tq, q0=q0, n_c=n_c, n_s=n_s),
        grid=(nb,),
        in_specs=[pl.BlockSpec((None, tq, width), lambda b: (b, 0, 0)),
                  pl.BlockSpec((None,) + cmp_tok.shape[1:], lambda b: (b, 0, 0)),
                  pl.BlockSpec(mcs.shape, lambda b: (0, 0)),
                  pl.BlockSpec((None,) + kwin_t.shape[1:], lambda b: (b, 0, 0))],
        out_specs=[pl.BlockSpec((None, tq, width), lambda b: (b, 0, 0)),
                   pl.BlockSpec((None, tq, width), lambda b: (b, 0, 0)),
                   pl.BlockSpec((None, tq, NSA_KV * nbp), lambda b: (b, 0, 0))],
        out_shape=[jax.ShapeDtypeStruct((nb, tq, width), F32),
                   jax.ShapeDtypeStruct((nb, tq, width), F32),
                   jax.ShapeDtypeStruct((nb, tq, NSA_KV * nbp), F32)],
        compiler_params=pltpu.CompilerParams(dimension_semantics=("parallel",), vmem_limit_bytes=VMEM_LIMIT),
        name="nsa_decode_cmp_win",
    )(q8, cmp_tok, mcs, kwin_t)


def _stream_kernel(*refs, mode, nseg, ppseg, n_rows, hk, q0, n_segs):
    pps = nseg * ppseg
    refs = refs[1:]
    qbd_ref, slope_ref, t_ref = refs[0:3]
    k = 3
    sel_ref = selnew_ref = em_ref = None
    if mode == "slc":
        sel_ref, selnew_ref, em_ref = refs[k:k + 3]
        k += 3
    pages = refs[k:k + pps]
    k += pps
    lf_pages = None
    if mode == "fox":
        lf_pages = refs[k:k + pps]
        k += pps
    new_ref = refs[k]
    k += 1
    newlf_ref = tri_ref = None
    if mode == "fox":
        newlf_ref, tri_ref = refs[k], refs[k + 1]
        k += 2
    out_ref = refs[k]
    oparts, m_s, l_s, x_s = refs[k + 1:k + 5]
    st = pl.program_id(1)
    nsteps = pl.num_programs(1)
    qbd = qbd_ref[...]
    slope = slope_ref[...]
    tcol = t_ref[...]
    lane = lax.broadcasted_iota(I32, (n_rows, LANES), 1)
    seg_lane = lax.broadcasted_iota(I32, m_s.shape, 1)
    reps = n_rows // FOX_HEADS

    @pl.when(st == 0)
    def _():
        m_s[...] = jnp.zeros(m_s.shape, F32)
        l_s[...] = jnp.zeros(l_s.shape, F32)
        x_s[...] = jnp.zeros(x_s.shape, F32)

    def scores(page, page_pos0, lf_page, run):
        kt = page[0:hk, :].astype(BF16)
        vt = page[hk:2 * hk, :].astype(BF16)
        s_raw = _dot(qbd, kt)
        if mode == "fox":
            cs = _dot01(lf_page, tri_ref[...]) + run
            run = cs[:, LANES - 1:LANES]
            s = s_raw - jnp.concatenate([cs] * reps, axis=0)
        else:
            s = s_raw - slope * (tcol - (page_pos0 + lane)).astype(F32)
        return s_raw, s, vt, run

    def sel_mask(sref, width):
        mexp = _dot(sref[...].astype(BF16), em_ref[:, 0:width])
        return jnp.where(mexp > 0.5, 0.0, NEG)

    m_all, l_all, x_all = m_s[...], l_s[...], x_s[...]
    mb = sel_mask(sel_ref, pps * PAGE) if mode == "slc" else None
    for sg in range(nseg):
        run = jnp.zeros((FOX_HEADS, 1), F32)
        ss, vts, raws = [], [], []
        for i in range(sg * ppseg, (sg + 1) * ppseg):
            s_raw, s, vt, run = scores(pages[i][...], (st * pps + i) * PAGE,
                                       lf_pages[i][...] if mode == "fox" else None, run)
            if mode == "slc":
                s = s + jnp.concatenate([mb[:, i * PAGE:(i + 1) * PAGE]] * (n_rows // mb.shape[0]), axis=0)
            ss.append(s)
            vts.append(vt)
            raws.append(s_raw)
        m = ss[0].max(axis=-1, keepdims=True)
        for s in ss[1:]:
            m = jnp.maximum(m, s.max(axis=-1, keepdims=True))
        l = jnp.zeros((n_rows, 1), F32)
        o = jnp.zeros((n_rows, hk), F32)
        for s, vt in zip(ss, vts):
            p = jnp.exp(s - m)
            l = l + p.sum(axis=-1, keepdims=True)
            o = o + _dot_nt(p.astype(BF16), vt)
        seg = st * nseg + sg
        oparts[seg] = o
        m_all = jnp.where(seg_lane == seg, m, m_all)
        l_all = jnp.where(seg_lane == seg, l, l_all)
        if mode == "fox":
            x_all = jnp.where(seg_lane == seg, jnp.concatenate([run] * reps, axis=0), x_all)
        elif mode == "moba":
            gsum = raws[0].sum(axis=-1, keepdims=True)
            for r_ in raws[1:]:
                gsum = gsum + r_.sum(axis=-1, keepdims=True)
            x_all = jnp.where(seg_lane == seg, gsum * (1.0 / MOBA_BLOCK), x_all)
    m_s[...] = m_all
    l_s[...] = l_all
    x_s[...] = x_all

    @pl.when(st == nsteps - 1)
    def _():
        last = n_segs - 1
        _, s, vt, _ = scores(new_ref[...], q0, newlf_ref[...] if mode == "fox" else None,
                             jnp.zeros((FOX_HEADS, 1), F32))
        if mode == "slc":
            mnew = sel_mask(selnew_ref, PAGE)
            s = s + jnp.concatenate([mnew] * (n_rows // mnew.shape[0]), axis=0)
        p, l, m = _softmax_parts(s, ((q0 + lane) <= tcol) & (s > 0.5 * NEG))
        oparts[last] = _dot_nt(p.astype(BF16), vt)
        mm = jnp.where(seg_lane == last, m, m_all)
        ll = jnp.where(seg_lane == last, l, l_all)
        if mode == "fox":
            mm = mm + _dot01(x_all, tri_ref[...], nt=True)
            valid = seg_lane < n_segs
        elif mode == "slc":
            valid = seg_lane < n_segs
        else:
            n_b = n_segs - 1
            gate = jnp.where(seg_lane < n_b, x_all, -jnp.inf)
            valid = _topk_mask(gate, min(MOBA_TOPK, n_b), n_b) | (seg_lane == last)
        mx = jnp.max(jnp.where(valid, mm, NEG), axis=-1, keepdims=True)
        w = jnp.where(valid, jnp.exp(mm - mx), 0.0)
        den = jnp.maximum(jnp.sum(w * ll, axis=-1, keepdims=True), 1e-30)
        num = jnp.zeros((n_rows, hk), F32)
        for seg in range(n_segs):
            num = num + w[:, seg:seg + 1] * oparts[seg]
        out_ref[...] = num / den


def _stream(mode, qbd, slope_col, t_col, sel_steps, pool_t, lf_pool_t, new_t, newlf_t, page_table, q0, nseg, ppseg):
    nb, n_rows, hk = qbd.shape
    n_pages = page_table.shape[1]
    pps = nseg * ppseg
    nsteps = n_pages // pps
    n_segs = nsteps * nseg + 1
    segp = LANES
    assert n_segs <= segp and n_pages % pps == 0

    def cst(shape):
        nd = len(shape)
        return pl.BlockSpec(shape, lambda b, s, pt: (0,) * nd)

    def per_b(shape):
        nd = len(shape)
        return pl.BlockSpec((None,) + shape, lambda b, s, pt: (b,) + (0,) * nd)

    in_specs = [per_b((n_rows, hk)), cst((n_rows, 1)), cst((n_rows, 1))]
    args = [qbd, slope_col, t_col]
    if mode == "slc":
        nsel = sel_steps.shape[2]
        blocks_per_step = pps * (PAGE // SLC_BLOCK)
        assert blocks_per_step <= LANES and sel_steps.shape[1] == nsteps + 1
        em = np.zeros((LANES, pps * PAGE), np.float32)
        em[np.arange(pps * PAGE) // SLC_BLOCK, np.arange(pps * PAGE)] = 1.0
        em = jnp.asarray(em, BF16)
        in_specs += [pl.BlockSpec((None, None, nsel, LANES), lambda b, s, pt: (b, s, 0, 0)),
                     pl.BlockSpec((None, None, nsel, LANES), lambda b, s, pt: (b, nsteps, 0, 0)),
                     cst(em.shape)]
        args += [sel_steps, sel_steps, em]
    for i in range(pps):
        in_specs.append(pl.BlockSpec((None, 2 * hk, PAGE), lambda b, s, pt, i=i: (pt[b, s * pps + i], 0, 0)))
        args.append(pool_t)
    if mode == "fox":
        for i in range(pps):
            in_specs.append(pl.BlockSpec((None, FOX_HEADS, PAGE), lambda b, s, pt, i=i: (pt[b, s * pps + i], 0, 0)))
            args.append(lf_pool_t)
    in_specs.append(per_b((2 * hk, PAGE)))
    args.append(new_t)
    if mode == "fox":
        tri = _tri_matrix()
        in_specs += [per_b((FOX_HEADS, PAGE)), cst(tri.shape)]
        args += [newlf_t, tri]
    grid_spec = pltpu.PrefetchScalarGridSpec(
        num_scalar_prefetch=1, grid=(nb, nsteps), in_specs=in_specs,
        out_specs=pl.BlockSpec((None, n_rows, hk), lambda b, s, pt: (b, 0, 0)),
        scratch_shapes=[pltpu.VMEM((n_segs, n_rows, hk), F32), pltpu.VMEM((n_rows, segp), F32),
                        pltpu.VMEM((n_rows, segp), F32), pltpu.VMEM((n_rows, segp), F32)])
    return pl.pallas_call(
        functools.partial(_stream_kernel, mode=mode, nseg=nseg, ppseg=ppseg, n_rows=n_rows, hk=hk, q0=q0,
                          n_segs=n_segs),
        grid_spec=grid_spec,
        out_shape=jax.ShapeDtypeStruct((nb, n_rows, hk), F32),
        compiler_params=pltpu.CompilerParams(dimension_semantics=("parallel", "arbitrary"),
                                             vmem_limit_bytes=VMEM_LIMIT),
        name="decode_stream_" + mode,
    )(page_table, *args)


def _nsa_gate_kernel(oc_ref, os_ref, ow_ref, gl_ref, o_ref):
    sig = 1.0 / (1.0 + jnp.exp(-gl_ref[...]))
    for h in range(NSA_HEADS):
        sl = slice(h * HEAD_DIM, (h + 1) * HEAD_DIM)
        o_ref[:, sl] = (oc_ref[:, sl] * sig[:, 3 * h:3 * h + 1] + os_ref[:, sl] * sig[:, 3 * h + 1:3 * h + 2]
                        + ow_ref[:, sl] * sig[:, 3 * h + 2:3 * h + 3]).astype(BF16)


def _nsa_gate(oc, os_, ow, gl):
    return pl.pallas_call(
        _nsa_gate_kernel,
        out_shape=jax.ShapeDtypeStruct(oc.shape, BF16),
        name="nsa_gate",
    )(oc, os_, ow, gl)


def _pool_t(cache, li):
    c = jnp.transpose(cache[li], (0, 2, 3, 4, 1))
    return c.reshape(c.shape[0], -1, c.shape[-1])


def _block_diag_q(q, n_tok, n_kv, n_r, r_major=False):
    b = q.shape[0]
    q5 = q.reshape(b, n_tok, n_kv, n_r, 1, HEAD_DIM)
    eye = jnp.eye(n_kv, dtype=q.dtype).reshape(1, 1, n_kv, 1, n_kv, 1)
    x = (q5 * eye).reshape(b, n_tok, n_kv, n_r, n_kv * HEAD_DIM)
    if r_major:
        x = x.transpose(0, 3, 1, 2, 4)
    return x.reshape(b, n_tok * n_kv * n_r, n_kv * HEAD_DIM)


def _diag_heads(o, n_tok, n_kv, n_r, r_major=False):
    b = o.shape[0]
    if r_major:
        o6 = o.reshape(b, n_r, n_tok, n_kv, n_kv, HEAD_DIM).transpose(0, 2, 3, 1, 4, 5)
    else:
        o6 = o.reshape(b, n_tok, n_kv, n_r, n_kv, HEAD_DIM)
    d = jnp.einsum("btgrgd->btgrd", o6)
    return d.reshape(b * n_tok, n_kv * n_r * HEAD_DIM)


def _row_consts(n_tok, n_kv, n_r, slopes, q0, r_major=False):
    t, g, r = np.meshgrid(np.arange(n_tok), np.arange(n_kv), np.arange(n_r), indexing="ij")
    h = g * n_r + r
    if r_major:
        t, h = t.transpose(2, 0, 1), h.transpose(2, 0, 1)
    t, h = t.reshape(-1), h.reshape(-1)
    sl = np.asarray(slopes, np.float32)[h] if slopes is not None else np.zeros(h.shape, np.float32)
    return jnp.asarray(sl.reshape(-1, 1), F32), jnp.asarray((q0 + t).reshape(-1, 1), I32)


def _new_pages(kv_t, nb, n_tok):
    rows = kv_t.shape[0]
    x = kv_t.reshape(rows, nb, n_tok).transpose(1, 0, 2)
    return jnp.pad(x, ((0, 0), (0, 0), (0, PAGE - n_tok)))


def kernel(x_prompt, x_sample, cache_nsa_cmp_kv, cache_nsa_slc_kv, state_nsa_win_kv, cache_fox_kv, cache_fox_lf,
           cache_moba_kv, page_table, norm_mix, norm_mlp, w_in_even, b_fgt, w_out_even, cmp_pe, cmp_w1, cmp_w2,
           w_in_odd, w_out_odd, w_up, w_down, norm_final):
    nb_p, s_len, _ = x_prompt.shape
    nb_d, n_tok, _ = x_sample.shape
    n_pages = page_table.shape[1]
    past = n_pages * PAGE
    nq_w = NSA_HEADS * HEAD_DIM
    nkv_w = 2 * NSA_KV * HEAD_DIM
    fq_w = FOX_HEADS * HEAD_DIM
    fkv_w = 2 * FOX_HEADS * HEAD_DIM
    mq_w = MOBA_HEADS * HEAD_DIM
    mkv_w = 2 * MOBA_KV * HEAD_DIM
    ngl = 3 * NSA_HEADS

    wte = jnp.transpose(w_in_even[0])
    o_nq, o_ckv, o_skv, o_wkv = 0, nq_w, nq_w + nkv_w, nq_w + 2 * nkv_w
    o_gl = nq_w + 3 * nkv_w
    o_fq = o_gl + ngl
    o_fkv = o_fq + fq_w
    o_fl = o_fkv + fkv_w
    wn_e = jnp.concatenate([wte[o_nq:o_nq + nq_w], wte[o_fq:o_fq + fq_w], wte[o_gl:o_gl + ngl],
                            jnp.zeros((LANES - ngl, D_MODEL), F32)], axis=0)
    wn_e = jnp.transpose(wn_e).astype(BF16)
    wt_e = jnp.concatenate([wte[o_ckv:o_ckv + 3 * nkv_w], wte[o_fkv:o_fkv + fkv_w], wte[o_fl:o_fl + FOX_HEADS]],
                           axis=0).astype(BF16)
    bias_e = b_fgt[0].reshape(FOX_HEADS, 1)
    nat_e = [(0, nq_w, "q"), (nq_w, fq_w, "q"), (nq_w + fq_w, LANES, "f32")]
    r_skv, r_wkv, r_fkv, r_fl = nkv_w, 2 * nkv_w, 3 * nkv_w, 3 * nkv_w + fkv_w
    tr_e_prompt = [(0, nkv_w, "f32"), (r_skv, nkv_w, "f32"), (r_skv, nkv_w, "bf16"), (r_wkv, nkv_w, "f32"),
                   (r_wkv, nkv_w, "bf16c"), (r_fkv, fkv_w, "f32"), (r_fkv, fkv_w, "bf16"), (r_fl, FOX_HEADS, "lf")]
    tr_e_dec = [(0, nkv_w, "f32"), (r_skv, nkv_w, "f32"), (r_wkv, nkv_w, "f32"), (r_fkv, fkv_w, "f32"),
                (r_fl, FOX_HEADS, "lf")]
    wo_t = jnp.transpose(w_in_odd[0])
    wn_o = w_in_odd[0][:, :mq_w].astype(BF16)
    wt_o = wo_t[mq_w:].astype(BF16)
    bias_o = jnp.zeros((8, 1), F32)
    nat_o = [(0, mq_w, "q")]
    tr_o_prompt = [(0, mkv_w, "f32"), (0, mkv_w, "bf16c2")]
    tr_o_dec = [(0, mkv_w, "f32")]
    woe = w_out_even[0].astype(BF16)
    woe_a, woe_b = woe[:nq_w], woe[nq_w:]
    woo = w_out_odd[0].astype(BF16)
    wup = w_up.astype(BF16)
    wdn = w_down.astype(BF16)
    g_mix = norm_mix.reshape(norm_mix.shape[0], 1, D_MODEL)
    g_mlp = norm_mlp.reshape(norm_mlp.shape[0], 1, D_MODEL)
    g_fin = norm_final.reshape(1, D_MODEL)
    cw = _compress_weights(cmp_pe[0], cmp_w1[0], cmp_w2[0])

    xp = x_prompt.reshape(nb_p * s_len, D_MODEL)
    tm_p = 512
    (nq, fq, gl, ckv_t, skv_t, skv_b, wkv_t, wkv_c, fkv_t, fkv_b, lf_t) = _proj(
        xp, g_mix[0], wn_e, wt_e, bias_e, nat_e, tr_e_prompt, nb_p, s_len, tm_p)
    cmp_tok = _compress(ckv_t, None, cw, nb_p, s_len // PAGE, s_len // PAGE, paged=False)
    o_nsa = _nsa_prompt(nq, gl, cmp_tok, skv_b, wkv_c, nb_p, s_len)
    o_fox = _fox_prompt(fq, fkv_b, lf_t, nb_p, s_len)
    xp = _post(xp, [o_nsa, o_fox], [woe_a, woe_b], g_mlp[0], wup[0], wdn[0], g_fin, False, tm_p)
    mq, mkv_t, mkv_b = _proj(xp, g_mix[1], wn_o, wt_o, bias_o, nat_o, tr_o_prompt, nb_p, s_len, tm_p)
    o_moba = _moba_prompt(mq, mkv_b, nb_p, s_len)
    yp = _post(xp, [o_moba], [woo], g_mlp[1], wup[1], wdn[1], g_fin, True, tm_p)
    y_prompt = yp.reshape(nb_p, s_len, D_MODEL)

    def kv_out(t, n_h):
        b, _, s = t.shape
        return jnp.transpose(t.reshape(b, 2, n_h, HEAD_DIM, s), (0, 4, 1, 2, 3))[None]

    p_cmp = kv_out(ckv_t, NSA_KV)
    p_slc = kv_out(skv_t, NSA_KV)
    wb = min(WINDOW, s_len)
    p_win = kv_out(wkv_t[:, :, s_len - wb:], NSA_KV)
    p_fox = kv_out(fkv_t, FOX_HEADS)
    p_lf = jnp.transpose(lf_t, (0, 2, 1))[None]
    p_moba = kv_out(mkv_t, MOBA_KV)

    td = nb_d * n_tok
    xd = x_sample.reshape(td, D_MODEL)
    (nq_d, fq_d, gl_d, ckv_d, skv_d, wkv_d, fkv_d, lf_d) = _proj(
        xd, g_mix[0], wn_e, wt_e, bias_e, nat_e, tr_e_dec, 1, td, td)

    def kv_out_dec(t, n_h):
        return jnp.transpose(t[0]).reshape(1, nb_d, n_tok, 2, n_h, HEAD_DIM)

    s_cmp = kv_out_dec(ckv_d, NSA_KV)
    s_slc = kv_out_dec(skv_d, NSA_KV)
    s_fox = kv_out_dec(fkv_d, FOX_HEADS)
    s_lf = jnp.transpose(lf_d[0]).reshape(1, nb_d, n_tok, FOX_HEADS)
    win_state_t = jnp.transpose(state_nsa_win_kv[0], (0, 2, 3, 4, 1)).reshape(nb_d, nkv_w, -1)
    wkv_new = wkv_d[0].reshape(nkv_w, nb_d, n_tok).transpose(1, 0, 2)
    win_all = jnp.concatenate([win_state_t, wkv_new], axis=2)
    wbuf = win_state_t.shape[2]
    s_win = jnp.transpose(win_all[:, :, -wbuf:].reshape(nb_d, 2, NSA_KV, HEAD_DIM, wbuf), (0, 4, 1, 2, 3))[None]
    assert wbuf == WINDOW
    wpad = -(-(wbuf + n_tok) // LANES) * LANES
    kwin_t = jnp.pad(win_all, ((0, 0), (0, 0), (0, wpad - wbuf - n_tok)))

    assert (past + n_tok) // CMP_STRIDE == past // CMP_STRIDE
    cmp_pool = _pool_t(cache_nsa_cmp_kv, 0)
    cmp_tok_d = _compress(cmp_pool, page_table, cw, nb_d, n_pages, 32, paged=True)
    n_c = past // CMP_STRIDE - CMP_LEN // CMP_STRIDE + 1
    n_s = -(-(past + n_tok) // SLC_BLOCK)
    tq_d = 8
    q8 = jnp.pad(nq_d.reshape(nb_d, n_tok, nq_w), ((0, 0), (0, tq_d - n_tok), (0, 0)))
    o_cmp8, o_win8, sel8 = _nsa_dec(q8, cmp_tok_d, kwin_t, nb_d, past, n_c, n_s)
    o_cmp_d = o_cmp8[:, :n_tok].reshape(td, nq_w)
    o_win_d = o_win8[:, :n_tok].reshape(td, nq_w)
    nbp = sel8.shape[2] // NSA_KV
    pps_d = 16
    nsteps_d = n_pages // pps_d
    bps = pps_d * (PAGE // SLC_BLOCK)
    sel_tg = sel8[:, :n_tok].reshape(nb_d, n_tok * NSA_KV, nbp)[:, :, :n_s]
    sel_tg = jnp.pad(sel_tg, ((0, 0), (0, 0), (0, (nsteps_d + 1) * bps - n_s)))
    sel_steps = sel_tg.reshape(nb_d, n_tok * NSA_KV, nsteps_d + 1, bps).transpose(0, 2, 1, 3)
    sel_steps = jnp.pad(sel_steps, ((0, 0), (0, 0), (0, 0), (0, LANES - bps)))
    sl_nsa, t_nsa = _row_consts(n_tok, NSA_KV, NSA_R, NSA_SLOPES, past, r_major=True)
    qbd_s = _block_diag_q(nq_d.reshape(nb_d, n_tok, nq_w), n_tok, NSA_KV, NSA_R, r_major=True)
    new_s = _new_pages(skv_d[0], nb_d, n_tok)
    o_slc_bd = _stream("slc", qbd_s, sl_nsa, t_nsa, sel_steps, _pool_t(cache_nsa_slc_kv, 0), None, new_s, None,
                       page_table, past, 1, pps_d)
    o_slc_d = _diag_heads(o_slc_bd, n_tok, NSA_KV, NSA_R, r_major=True)
    o_nsa_d = _nsa_gate(o_cmp_d, o_slc_d, o_win_d, gl_d)
    sl_fox, t_fox = _row_consts(n_tok, FOX_HEADS, 1, None, past)
    qbd_f = _block_diag_q(fq_d.reshape(nb_d, n_tok, fq_w), n_tok, FOX_HEADS, 1)
    new_f = _new_pages(fkv_d[0], nb_d, n_tok)
    newlf = _new_pages(lf_d[0], nb_d, n_tok)
    lf_pool = jnp.transpose(cache_fox_lf[0], (0, 2, 1))
    o_fox_bd = _stream("fox", qbd_f, sl_fox, t_fox, None, _pool_t(cache_fox_kv, 0), lf_pool, new_f, newlf,
                       page_table, past, 1, pps_d)
    o_fox_d = _diag_heads(o_fox_bd, n_tok, FOX_HEADS, 1).astype(BF16)
    xd = _post(xd, [o_nsa_d, o_fox_d], [woe_a, woe_b], g_mlp[0], wup[0], wdn[0], g_fin, False, td)
    mq_d, mkv_d = _proj(xd, g_mix[1], wn_o, wt_o, bias_o, nat_o, tr_o_dec, 1, td, td)
    s_moba = kv_out_dec(mkv_d, MOBA_KV)
    assert past % MOBA_BLOCK == 0 and MOBA_BLOCK == 2 * PAGE
    sl_m, t_m = _row_consts(n_tok, MOBA_KV, MOBA_R, MOBA_SLOPES, past)
    qbd_m = _block_diag_q(mq_d.reshape(nb_d, n_tok, mq_w), n_tok, MOBA_KV, MOBA_R)
    new_m = _new_pages(mkv_d[0], nb_d, n_tok)
    o_moba_bd = _stream("moba", qbd_m, sl_m, t_m, None, _pool_t(cache_moba_kv, 0), None, new_m, None,
                        page_table, past, pps_d // (MOBA_BLOCK // PAGE), MOBA_BLOCK // PAGE)
    o_moba_d = _diag_heads(o_moba_bd, n_tok, MOBA_KV, MOBA_R).astype(BF16)
    yd = _post(xd, [o_moba_d], [woo], g_mlp[1], wup[1], wdn[1], g_fin, True, td)
    y_sample = yd.reshape(nb_d, n_tok, D_MODEL)

    return (y_prompt, y_sample, p_cmp, s_cmp, p_slc, s_slc, p_win, s_win, p_fox, s_fox, p_lf, s_lf, p_moba, s_moba)
```

```python
import functools
import math

import numpy as np
import jax
import jax.numpy as jnp
from jax import lax
from jax.experimental import pallas as pl
from jax.experimental.pallas import tpu as pltpu

F32 = jnp.float32
BF16 = jnp.bfloat16
I32 = jnp.int32

D_MODEL = 1024
HEAD_DIM = 64
NSA_HEADS = 8
NSA_KV = 2
NSA_R = NSA_HEADS // NSA_KV
CMP_LEN = 32
CMP_STRIDE = 16
CMP_HIDDEN = 128
SLC_BLOCK = 64
SLC_TOPN = 16
WINDOW = 512
FOX_HEADS = 8
MOBA_HEADS = 16
MOBA_KV = 4
MOBA_R = MOBA_HEADS // MOBA_KV
MOBA_BLOCK = 256
MOBA_TOPK = 3
D_FF = 4 * D_MODEL
PAGE = 128
RMS_EPS = 1e-6
NEG = -1e30
SCALE = HEAD_DIM ** -0.5
LANES = 128
KEY_STEP = 512
VMEM_LIMIT = 56 * 1024 * 1024

NT_DIMS = (((1,), (1,)), ((), ()))


def _alibi(n):
    return [float(np.float32(2.0 ** (-8.0 * (i + 1) / n))) for i in range(n)]


NSA_SLOPES = _alibi(NSA_HEADS)
MOBA_SLOPES = _alibi(MOBA_HEADS)


def _div(x, n):
    assert n & (n - 1) == 0
    return jnp.right_shift(x, int(math.log2(n)))


def _mod(x, n):
    assert n & (n - 1) == 0
    return jnp.bitwise_and(x, n - 1)


def _dot(a, b):
    return jnp.dot(a, b, preferred_element_type=F32)


def _dot_nt(a, b):
    return lax.dot_general(a, b, NT_DIMS, preferred_element_type=F32)


def _split3(x):
    hi = x.astype(BF16)
    r1 = x - hi.astype(F32)
    mid = r1.astype(BF16)
    lo = (r1 - mid.astype(F32)).astype(BF16)
    return hi, mid, lo


def _dot01(x, mat, nt=False):
    n = x.shape[0]
    st = jnp.concatenate(_split3(x), axis=0)
    y = _dot_nt(st, mat) if nt else _dot(st, mat)
    return y[0:n] + y[n:2 * n] + y[2 * n:3 * n]


def _softmax_parts(s, valid):
    s = jnp.where(valid, s, NEG)
    m = jnp.max(s, axis=-1, keepdims=True)
    p = jnp.where(valid, jnp.exp(s - m), 0.0)
    l = jnp.maximum(jnp.sum(p, axis=-1, keepdims=True), 1e-30)
    return p, l, m


def _softmax_bias(logits):
    m = jnp.max(logits, axis=-1, keepdims=True)
    p = jnp.exp(logits - m)
    return p, jnp.sum(p, axis=-1, keepdims=True), m


def _topk_mask(v, nsel, nb):
    lane = lax.broadcasted_iota(I32, v.shape, 1)
    cnt = jnp.zeros(v.shape, I32)
    for k in range(nb):
        col = v[:, k:k + 1]
        beats = (col > v) | ((col == v) & (lane > k))
        cnt = cnt + jnp.where(beats, 1, 0)
    return (cnt < nsel) & (v > -jnp.inf)


def _dot01_rhs(mat, x):
    n = x.shape[1]
    y = _dot(mat, jnp.concatenate(_split3(x), axis=1))
    return y[:, 0:n] + y[:, n:2 * n] + y[:, 2 * n:3 * n]


def _bf16_parts(x):
    def rnd(v):
        u = np.float32(v).view(np.uint32)
        u = np.uint32((int(u) + 0x7FFF + ((int(u) >> 16) & 1)) & 0xFFFF0000)
        return float(u.view(np.float32))
    hi = rnd(x)
    mid = rnd(np.float32(np.float32(x) - np.float32(hi)))
    lo = rnd(np.float32(np.float32(x) - np.float32(hi) - np.float32(mid)))
    return hi, mid, lo


N_AUG = 6


def _aug_keys(k, pos):
    lane = lax.broadcasted_iota(I32, k.shape, 1)
    a = _div(pos, LANES).astype(F32)
    b = _mod(pos, LANES).astype(F32)
    aug = jnp.where(lane < 3, a, jnp.where(lane < N_AUG, b, 0.0)).astype(BF16)
    return jnp.concatenate([k, aug], axis=1)


def _aug_query(qt, slope):
    parts = _bf16_parts(slope)
    vals = [float(LANES) * p for p in parts] + list(parts)
    sub = lax.broadcasted_iota(I32, qt.shape, 0)
    aug = jnp.zeros(qt.shape, F32)
    for i, v in enumerate(vals):
        aug = jnp.where(sub == i, v, aug)
    return jnp.concatenate([qt, aug.astype(BF16)], axis=0)


def _softmax_cols(logits):
    m = jnp.max(logits, axis=0, keepdims=True)
    p = jnp.exp(logits - m)
    return p, jnp.sum(p, axis=0, keepdims=True), m


def _topk_rows(v, nsel, nb):
    row = lax.broadcasted_iota(I32, v.shape, 0)
    cnt = jnp.zeros(v.shape, I32)
    for k in range(nb):
        rk = v[k:k + 1, :]
        beats = (rk > v) | ((rk == v) & (row > k))
        cnt = cnt + jnp.where(beats, 1, 0)
    return (cnt < nsel) & (v > -jnp.inf)


def _flash_step(h, kc, vtc, qaug, bias, m_st, l_st, acc_st):
    s = _dot(kc, qaug)
    if bias is not None:
        s = s + bias
    m_old = m_st[h, 0:1, :]
    m_new = jnp.maximum(m_old, jnp.max(s, axis=0, keepdims=True))
    alpha = jnp.exp(m_old - m_new)
    p = jnp.exp(s - m_new)
    l_st[h, 0:1, :] = alpha * l_st[h, 0:1, :] + jnp.sum(p, axis=0, keepdims=True)
    acc_st[h] = alpha * acc_st[h] + _dot(vtc, p.astype(BF16))
    m_st[h, 0:1, :] = m_new


def _stack_heads(q, g, nr):
    return jnp.concatenate([q[:, (g * nr + r) * HEAD_DIM:(g * nr + r + 1) * HEAD_DIM] for r in range(nr)], axis=0)


def _row_slopes(r4, slopes, g, nr):
    out = jnp.zeros(r4.shape, F32)
    for r in range(nr):
        out = jnp.where(r4 == r, slopes[g * nr + r], out)
    return out


_CHUNKED = {"bf16c": LANES, "bf16c2": MOBA_BLOCK}


def _proj_kernel(x_ref, g_ref, wn_ref, wt_ref, b_ref, *outs, nat, tr):
    x = x_ref[...]
    ms = jnp.mean(x * x, axis=-1, keepdims=True)
    h = (x * lax.rsqrt(ms + RMS_EPS)) * g_ref[...]
    hb = h.astype(BF16)
    yn = _dot(hb, wn_ref[...])
    yt = _dot_nt(wt_ref[...], hb)
    k = 0
    for (c0, w, kind) in nat:
        v = yn[:, c0:c0 + w]
        if kind == "q":
            outs[k][...] = (v * SCALE).astype(BF16)
        elif kind == "kb":
            outs[k][...] = v.astype(BF16)
        else:
            outs[k][...] = v
        k += 1
    for (r0, n, kind) in tr:
        v = yt[r0:r0 + n, :]
        if kind == "f32":
            outs[k][...] = v
        elif kind == "bf16":
            outs[k][...] = v.astype(BF16)
        elif kind == "qt":
            outs[k][...] = (v * SCALE).astype(BF16)
        elif kind in _CHUNKED:
            cw = _CHUNKED[kind]
            vb = v.astype(BF16)
            for c in range(v.shape[1] // cw):
                outs[k][c] = vb[:, c * cw:(c + 1) * cw]
        else:
            z = v + b_ref[...]
            outs[k][...] = jnp.minimum(z, 0.0) - jnp.log(1.0 + jnp.exp(-jnp.abs(z)))
        k += 1


def _proj(x2d, gain, wn, wt, bias, nat, tr, nb, s_len, tm):
    t_tot = x2d.shape[0]
    tpb = s_len // tm
    assert nb * s_len == t_tot
    out_shape, out_specs = [], []
    for (c0, w, kind) in nat:
        out_shape.append(jax.ShapeDtypeStruct((t_tot, w), BF16 if kind in ("q", "kb") else F32))
        out_specs.append(pl.BlockSpec((tm, w), lambda i: (i, 0)))
    for (r0, n, kind) in tr:
        if kind in _CHUNKED:
            cw = _CHUNKED[kind]
            out_shape.append(jax.ShapeDtypeStruct((nb, s_len // cw, n, cw), BF16))
            out_specs.append(pl.BlockSpec((None, tm // cw, n, cw), lambda i: (i // tpb, i % tpb, 0, 0)))
        else:
            out_shape.append(jax.ShapeDtypeStruct((nb, n, s_len), BF16 if kind in ("bf16", "qt") else F32))
            out_specs.append(pl.BlockSpec((None, n, tm), lambda i: (i // tpb, 0, i % tpb)))
    return pl.pallas_call(
        functools.partial(_proj_kernel, nat=tuple(nat), tr=tuple(tr)),
        grid=(t_tot // tm,),
        in_specs=[pl.BlockSpec((tm, D_MODEL), lambda i: (i, 0)),
                  pl.BlockSpec((1, D_MODEL), lambda i: (0, 0)),
                  pl.BlockSpec(wn.shape, lambda i: (0, 0)),
                  pl.BlockSpec(wt.shape, lambda i: (0, 0)),
                  pl.BlockSpec(bias.shape, lambda i: (0, 0))],
        out_specs=out_specs,
        out_shape=out_shape,
        compiler_params=pltpu.CompilerParams(dimension_semantics=("parallel",), vmem_limit_bytes=VMEM_LIMIT),
        name="norm_proj",
    )(x2d, gain, wn, wt, bias)


def _gelu_tanh(x):
    return 0.5 * x * (1.0 + jnp.tanh(math.sqrt(2.0 / math.pi) * (x + 0.044715 * (x * x * x))))


def _compress_kernel(*refs, npg, n_prefetch):
    refs = refs[n_prefetch:]
    pages = refs[:npg]
    pt_ref, pea_ref, peb_ref, w1a_ref, w1b_ref, w2_ref, out_ref, t_scr, carry = refs[npg:]
    sub = PAGE // CMP_STRIDE
    m = npg * sub

    @pl.when(pl.program_id(1) == 0)
    def _():
        carry[...] = jnp.zeros(carry.shape, F32)

    for k in range(npg):
        xb = pages[k][...].astype(BF16)
        tt = _dot_nt(pt_ref[...], xb)
        for p in range(CMP_STRIDE):
            t_scr[p, k * sub:(k + 1) * sub, :] = tt[p * sub:(p + 1) * sub, :]
    acc_a = jnp.zeros((m, w1a_ref.shape[2]), F32)
    acc_b = jnp.zeros((m, w1a_ref.shape[2]), F32)
    for p in range(CMP_STRIDE):
        tp = t_scr[p]
        acc_a = acc_a + _dot((tp + pea_ref[p]).astype(BF16), w1a_ref[p])
        acc_b = acc_b + _dot((tp + peb_ref[p]).astype(BF16), w1b_ref[p])
    rolled = pltpu.roll(acc_a, 1, axis=0)
    row = lax.broadcasted_iota(I32, acc_a.shape, 0)
    hid = jnp.where(row == 0, carry[...], rolled) + acc_b
    carry[...] = acc_a[m - 1:m, :]
    act = _gelu_tanh(hid)
    out_ref[...] = _dot(act.astype(BF16), w2_ref[...])


def _compress_weights(cmp_pe, cmp_w1, cmp_w2):
    nkg = 2 * NSA_KV
    eye = jnp.eye(nkg, dtype=F32)
    w1 = cmp_w1.reshape(2, CMP_LEN, HEAD_DIM, CMP_HIDDEN)
    w1 = jnp.repeat(w1, NSA_KV, axis=0)
    bd = jnp.einsum("kpdh,kl->pkdlh", w1, eye).reshape(CMP_LEN, nkg * HEAD_DIM, nkg * CMP_HIDDEN).astype(BF16)
    w2 = jnp.repeat(cmp_w2, NSA_KV, axis=0)
    w2bd = jnp.einsum("khd,kl->khld", w2, eye).reshape(nkg * CMP_HIDDEN, nkg * HEAD_DIM).astype(BF16)
    pe = jnp.repeat(cmp_pe, NSA_KV, axis=0)
    pe = pe.transpose(1, 0, 2).reshape(CMP_LEN, 1, nkg * HEAD_DIM)
    return bd[:CMP_STRIDE], bd[CMP_STRIDE:], w2bd, pe[:CMP_STRIDE], pe[CMP_STRIDE:]


def _perm_matrix():
    sub = PAGE // CMP_STRIDE
    pt = np.zeros((PAGE, PAGE), np.float32)
    for p in range(CMP_STRIDE):
        for j in range(sub):
            pt[p * sub + j, CMP_STRIDE * j + p] = 1.0
    return jnp.asarray(pt, BF16)


def _compress(pages_arr, page_table, cw, nb, n_pages, npg, paged):
    w1a, w1b, w2bd, pea, peb = cw
    rows = 2 * NSA_KV * HEAD_DIM
    m = npg * (PAGE // CMP_STRIDE)
    n_chunks = n_pages // npg
    page_specs = []
    for k in range(npg):
        if paged:
            page_specs.append(pl.BlockSpec((None, rows, PAGE), lambda b, c, pt, k=k: (pt[b, c * npg + k], 0, 0)))
        else:
            page_specs.append(pl.BlockSpec((None, rows, PAGE), lambda b, c, k=k: (b, 0, c * npg + k)))

    def const(shape):
        nd = len(shape)
        if paged:
            return pl.BlockSpec(shape, lambda b, c, pt: (0,) * nd)
        return pl.BlockSpec(shape, lambda b, c: (0,) * nd)

    perm = _perm_matrix()
    consts = [perm, pea, peb, w1a, w1b, w2bd]
    in_specs = page_specs + [const(a.shape) for a in consts]
    if paged:
        out_spec = pl.BlockSpec((None, m, rows), lambda b, c, pt: (b, c, 0))
    else:
        out_spec = pl.BlockSpec((None, m, rows), lambda b, c: (b, c, 0))
    n_prefetch = 1 if paged else 0
    grid_spec = pltpu.PrefetchScalarGridSpec(
        num_scalar_prefetch=n_prefetch, grid=(nb, n_chunks), in_specs=in_specs, out_specs=out_spec,
        scratch_shapes=[pltpu.VMEM((CMP_STRIDE, m, rows), F32), pltpu.VMEM((1, w1a.shape[2]), F32)])
    args = ([page_table] if paged else []) + [pages_arr] * npg + consts
    return pl.pallas_call(
        functools.partial(_compress_kernel, npg=npg, n_prefetch=n_prefetch),
        grid_spec=grid_spec,
        out_shape=jax.ShapeDtypeStruct((nb, n_chunks * m, rows), F32),
        compiler_params=pltpu.CompilerParams(dimension_semantics=("parallel", "arbitrary"),
                                             vmem_limit_bytes=VMEM_LIMIT),
        name="nsa_compress",
    )(*args)


def _cmp_to_slc(n_rows, n_c, n_s, n_cols):
    m = np.zeros((n_rows, n_cols), np.float32)
    c = np.arange(n_c)[:, None]
    j = np.arange(n_s)[None, :]
    lo = c * CMP_STRIDE
    hi = lo + CMP_LEN
    m[1:n_c + 1, :n_s] = ((lo < (j + 1) * SLC_BLOCK) & (hi > j * SLC_BLOCK)).astype(np.float32)
    return jnp.asarray(m, BF16)


def _cmp_branch(qg4, cmp_tok, mcs, g, tq, t4, slope4, n_c):
    ck = cmp_tok[:, g * HEAD_DIM:(g + 1) * HEAD_DIM].astype(BF16)
    v0 = (NSA_KV + g) * HEAD_DIM
    cv = cmp_tok[:, v0:v0 + HEAD_DIM].astype(BF16)
    sc = _dot_nt(qg4, ck)
    col = lax.broadcasted_iota(I32, sc.shape, 1)
    dc = t4 - ((col - 1) * CMP_STRIDE + (CMP_LEN - 1))
    valid = (dc >= 0) & (col >= 1) & (col <= n_c)
    p, l, _ = _softmax_parts(sc - slope4 * dc.astype(F32), valid)
    pn = p / l
    o = _dot(pn.astype(BF16), cv)
    psum = pn[0:tq]
    for r in range(1, NSA_R):
        psum = psum + pn[r * tq:(r + 1) * tq]
    return o, _dot01(psum, mcs)


def _slc_select(imp, t1, n_s):
    blk = lax.broadcasted_iota(I32, imp.shape, 1)
    tb = _div(t1, SLC_BLOCK)
    avail = (blk <= tb) & (blk < n_s)
    forced = (blk == 0) | (blk == tb) | (blk == tb - 1)
    v = jnp.where(forced, jnp.inf, jnp.where(avail, imp, -jnp.inf))
    return _topk_mask(v, min(SLC_TOPN, n_s), n_s)


def _nsa_prompt_kernel(qt_ref, glt_ref, cmp_ref, mcst_ref, sk_ref, wk_ref, sv_ref, wv_ref, o_ref,
                       ksaug, kwaug, qaug_scr, selt_scr, ocmp_scr, oslc_scr, m_st, l_st, acc_st,
                       *, tq, s_len, n_c):
    qi = pl.program_id(1)
    s0 = qi * tq
    n_s = s_len // SLC_BLOCK
    kc_len = MOBA_BLOCK
    assert tq == LANES and kc_len == 2 * tq

    @pl.when(qi == 0)
    def _():
        pos = lax.broadcasted_iota(I32, (s_len, HEAD_DIM), 0)
        for g in range(NSA_KV):
            ksaug[g] = _aug_keys(sk_ref[:, g * HEAD_DIM:(g + 1) * HEAD_DIM], pos)
            kwaug[g] = _aug_keys(wk_ref[:, g * HEAD_DIM:(g + 1) * HEAD_DIM], pos)

    for g in range(NSA_KV):
        qaug_scr[g] = jnp.concatenate(
            [_aug_query(qt_ref[h * HEAD_DIM:(h + 1) * HEAD_DIM, :], NSA_SLOPES[h])
             for h in range(g * NSA_R, (g + 1) * NSA_R)], axis=1)
    tl = s0 + lax.broadcasted_iota(I32, (1, tq), 1)
    sig = 1.0 / (1.0 + jnp.exp(-glt_ref[...]))

    def lanes4(x):
        return jnp.concatenate([x] * NSA_R, axis=1)

    cmp_tok = cmp_ref[...]
    ncp = cmp_tok.shape[0]
    cmp_t = cmp_tok.T
    crow = lax.broadcasted_iota(I32, (ncp, tq), 0)
    c_end = (crow - 1) * CMP_STRIDE + (CMP_LEN - 1)
    cbias = lanes4(jnp.where((c_end <= tl) & (crow >= 1) & (crow <= n_c), 0.0, NEG))
    any_c = lanes4(jnp.where(tl >= (CMP_LEN - 1), 1.0, 0.0)) > 0.5
    cpos = jnp.maximum((lax.broadcasted_iota(I32, (ncp, HEAD_DIM), 0) - 1) * CMP_STRIDE + (CMP_LEN - 1), 0)
    blk = lax.broadcasted_iota(I32, (n_s, tq), 0)
    tb = _div(tl, SLC_BLOCK)
    avail = blk <= tb
    forced = (blk == 0) | (blk == tb) | (blk == tb - 1)
    for g in range(NSA_KV):
        ck_aug = _aug_keys(cmp_tok[:, g * HEAD_DIM:(g + 1) * HEAD_DIM].astype(BF16), cpos)
        cvt = cmp_t[(NSA_KV + g) * HEAD_DIM:(NSA_KV + g + 1) * HEAD_DIM, :].astype(BF16)
        p, l, _ = _softmax_cols(_dot(ck_aug, qaug_scr[g]) + cbias)
        pn = jnp.where(any_c, p / l, 0.0)
        ocmp_scr[g] = _dot(cvt, pn.astype(BF16))
        psum = pn[:, 0:tq]
        for r in range(1, NSA_R):
            psum = psum + pn[:, r * tq:(r + 1) * tq]
        imp = _dot01_rhs(mcst_ref[...], psum)[0:n_s]
        v = jnp.where(forced, jnp.inf, jnp.where(avail, imp, -jnp.inf))
        selt_scr[g] = jnp.where(_topk_rows(v, min(SLC_TOPN, n_s), n_s), 1.0, 0.0)

    def reset_state():
        m_st[...] = jnp.full(m_st.shape, NEG, F32)
        l_st[...] = jnp.zeros(l_st.shape, F32)
        acc_st[...] = jnp.zeros(acc_st.shape, F32)

    reset_state()
    cd = _div(s0, kc_len)
    bpc = kc_len // SLC_BLOCK
    ksub = lax.broadcasted_iota(I32, (kc_len, tq), 0)
    diag_bias = jnp.where(cd * kc_len + ksub <= tl, 0.0, NEG)

    def slc_chunk(c, extra):
        for g in range(NSA_KV):
            rows = selt_scr[g, pl.ds(c * bpc, bpc), :]
            sb = jnp.where(rows > 0.5, 0.0, NEG)
            bias = jnp.concatenate([jnp.broadcast_to(sb[i:i + 1, :], (SLC_BLOCK, tq)) for i in range(bpc)], axis=0)
            if extra is not None:
                bias = bias + extra
            kc = ksaug[g, pl.ds(c * kc_len, kc_len), :]
            vtc = sv_ref[c, (NSA_KV + g) * HEAD_DIM:(NSA_KV + g + 1) * HEAD_DIM, :]
            _flash_step(g, kc, vtc, qaug_scr[g], lanes4(bias), m_st, l_st, acc_st)

    def slc_loop(c, carry):
        slc_chunk(c, None)
        return carry

    lax.fori_loop(0, cd, slc_loop, 0)
    slc_chunk(cd, diag_bias)
    for g in range(NSA_KV):
        oslc_scr[g] = acc_st[g] / l_st[g, 0:1, :]

    reset_state()
    nwc = WINDOW // LANES + 1
    wsub = lax.broadcasted_iota(I32, (tq, tq), 0)
    wlane = lax.broadcasted_iota(I32, (tq, tq), 1)
    far_bias = lanes4(jnp.where(wsub > wlane, 0.0, NEG))
    near_bias = lanes4(jnp.where(wsub <= wlane, 0.0, NEG))

    def win_chunk(j):
        c = qi - (nwc - 1) + j
        bias = far_bias if j == 0 else (near_bias if j == nwc - 1 else None)
        for g in range(NSA_KV):
            kc = kwaug[g, pl.ds(c * tq, tq), :]
            vtc = wv_ref[c, (NSA_KV + g) * HEAD_DIM:(NSA_KV + g + 1) * HEAD_DIM, :]
            _flash_step(g, kc, vtc, qaug_scr[g], bias, m_st, l_st, acc_st)

    for j in range(nwc):
        pl.when(qi - (nwc - 1) + j >= 0)(functools.partial(win_chunk, j))

    outs = []
    for g in range(NSA_KV):
        o_win = acc_st[g] / l_st[g, 0:1, :]
        gates = [jnp.concatenate([sig[3 * h + k:3 * h + k + 1, :] for h in range(g * NSA_R, (g + 1) * NSA_R)], axis=1)
                 for k in range(3)]
        og = ocmp_scr[g] * gates[0] + oslc_scr[g] * gates[1] + o_win * gates[2]
        outs += [og[:, r * tq:(r + 1) * tq] for r in range(NSA_R)]
    o_ref[...] = jnp.concatenate(outs, axis=0).T.astype(BF16)


def _nsa_prompt(nq_t, gl_t, cmp_tok, sk, wk, sv_c, wv_c, nb, s_len):
    tq = LANES
    nqt = s_len // tq
    n_c = s_len // CMP_STRIDE - CMP_LEN // CMP_STRIDE + 1
    ncp = cmp_tok.shape[1]
    mcst = jnp.transpose(_cmp_to_slc(ncp, n_c, s_len // SLC_BLOCK, LANES))
    rows = 2 * NSA_KV * HEAD_DIM
    width = NSA_HEADS * HEAD_DIM
    kw = NSA_KV * HEAD_DIM
    return pl.pallas_call(
        functools.partial(_nsa_prompt_kernel, tq=tq, s_len=s_len, n_c=n_c),
        grid=(nb, nqt),
        in_specs=[pl.BlockSpec((None, width, tq), lambda b, i: (b, 0, i)),
                  pl.BlockSpec((None, 3 * NSA_HEADS, tq), lambda b, i: (b, 0, i)),
                  pl.BlockSpec((None,) + cmp_tok.shape[1:], lambda b, i: (b, 0, 0)),
                  pl.BlockSpec(mcst.shape, lambda b, i: (0, 0)),
                  pl.BlockSpec((s_len, kw), lambda b, i: (b, 0)),
                  pl.BlockSpec((s_len, kw), lambda b, i: (b, 0)),
                  pl.BlockSpec((None, s_len // MOBA_BLOCK, rows, MOBA_BLOCK), lambda b, i: (b, 0, 0, 0)),
                  pl.BlockSpec((None, s_len // LANES, rows, LANES), lambda b, i: (b, 0, 0, 0))],
        out_specs=pl.BlockSpec((tq, width), lambda b, i: (b * nqt + i, 0)),
        out_shape=jax.ShapeDtypeStruct((nb * s_len, width), BF16),
        scratch_shapes=[pltpu.VMEM((NSA_KV, s_len, LANES), BF16), pltpu.VMEM((NSA_KV, s_len, LANES), BF16),
                        pltpu.VMEM((NSA_KV, LANES, NSA_R * tq), BF16),
                        pltpu.VMEM((NSA_KV, s_len // SLC_BLOCK, tq), F32),
                        pltpu.VMEM((NSA_KV, HEAD_DIM, NSA_R * tq), F32), pltpu.VMEM((NSA_KV, HEAD_DIM, NSA_R * tq), F32),
                        pltpu.VMEM((NSA_KV, 8, NSA_R * tq), F32), pltpu.VMEM((NSA_KV, 8, NSA_R * tq), F32),
                        pltpu.VMEM((NSA_KV, HEAD_DIM, NSA_R * tq), F32)],
        compiler_params=pltpu.CompilerParams(dimension_semantics=("parallel", "arbitrary"),
                                             vmem_limit_bytes=VMEM_LIMIT),
        name="nsa_prompt",
    )(nq_t, gl_t, cmp_tok, mcst, sk, wk, sv_c, wv_c)


def _tri_matrix():
    i = np.arange(LANES)
    return jnp.asarray((i[:, None] <= i[None, :]).astype(np.float32), BF16)


def _fox_prompt_kernel(q_ref, kv_ref, lf_ref, u_ref, o_ref, c_scr, *, tq, s_len, kstep):
    qi = pl.program_id(1)

    @pl.when(qi == 0)
    def _():
        carry = jnp.zeros((FOX_HEADS, 1), F32)
        for blk in range(s_len // LANES):
            cs = _dot01(lf_ref[:, blk * LANES:(blk + 1) * LANES], u_ref[...]) + carry
            c_scr[:, blk * LANES:(blk + 1) * LANES] = cs
            carry = cs[:, LANES - 1:LANES]

    s0 = qi * tq
    t = s0 + lax.broadcasted_iota(I32, (tq, 1), 0)
    nh = FOX_HEADS

    def body(kmax):
        pos = lax.broadcasted_iota(I32, (tq, kmax), 1)
        causal = jnp.where(pos <= t, 0.0, NEG)
        outs = []
        for h in range(nh):
            qh = q_ref[:, h * HEAD_DIM:(h + 1) * HEAD_DIM]
            kt = kv_ref[h * HEAD_DIM:(h + 1) * HEAD_DIM, 0:kmax]
            vt = kv_ref[(nh + h) * HEAD_DIM:(nh + h + 1) * HEAD_DIM, 0:kmax]
            p, l, _ = _softmax_bias(_dot(qh, kt) + (causal - c_scr[h:h + 1, 0:kmax]))
            outs.append(_dot_nt(p.astype(BF16), vt) / l)
        o_ref[...] = jnp.concatenate(outs, axis=1).astype(BF16)

    for c in range(s_len // kstep):
        pl.when(_div(s0, kstep) == c)(functools.partial(body, kstep * (c + 1)))


def _fox_prompt(fq, fkv_b, lf_t, nb, s_len):
    tq = LANES
    nqt = s_len // tq
    rows = 2 * FOX_HEADS * HEAD_DIM
    tri = _tri_matrix()
    return pl.pallas_call(
        functools.partial(_fox_prompt_kernel, tq=tq, s_len=s_len, kstep=KEY_STEP),
        grid=(nb, nqt),
        in_specs=[pl.BlockSpec((tq, FOX_HEADS * HEAD_DIM), lambda b, i: (b * nqt + i, 0)),
                  pl.BlockSpec((None, rows, s_len), lambda b, i: (b, 0, 0)),
                  pl.BlockSpec((None, FOX_HEADS, s_len), lambda b, i: (b, 0, 0)),
                  pl.BlockSpec(tri.shape, lambda b, i: (0, 0))],
        out_specs=pl.BlockSpec((tq, FOX_HEADS * HEAD_DIM), lambda b, i: (b * nqt + i, 0)),
        out_shape=jax.ShapeDtypeStruct((nb * s_len, FOX_HEADS * HEAD_DIM), BF16),
        scratch_shapes=[pltpu.VMEM((FOX_HEADS, s_len), F32)],
        compiler_params=pltpu.CompilerParams(dimension_semantics=("parallel", "arbitrary"),
                                             vmem_limit_bytes=VMEM_LIMIT),
        name="fox_prompt",
    )(fq, fkv_b, lf_t, tri)


def _moba_prompt_kernel(qt_ref, k_ref, kv_ref, eavg_ref, o_ref, kaug, km_scr, qaug_scr, m_scr, l_scr, o_scr,
                        *, tq, s_len):
    qi = pl.program_id(1)
    n_b = s_len // MOBA_BLOCK
    nbp = m_scr.shape[1]
    assert tq == LANES and MOBA_BLOCK == 2 * tq

    @pl.when(qi == 0)
    def _():
        pos = lax.broadcasted_iota(I32, (s_len, HEAD_DIM), 0)
        for g in range(MOBA_KV):
            kaug[g] = _aug_keys(k_ref[:, g * HEAD_DIM:(g + 1) * HEAD_DIM], pos)
        km_scr[...] = _dot(eavg_ref[...], k_ref[...]).astype(BF16)
        m_scr[...] = jnp.zeros(m_scr.shape, F32)
        l_scr[...] = jnp.zeros(l_scr.shape, F32)
        o_scr[...] = jnp.zeros(o_scr.shape, F32)

    s0 = qi * tq
    tb = _div(s0, MOBA_BLOCK)
    tl = s0 + lax.broadcasted_iota(I32, (1, tq), 1)
    for g in range(MOBA_KV):
        qaug_scr[g] = jnp.concatenate(
            [_aug_query(qt_ref[h * HEAD_DIM:(h + 1) * HEAD_DIM, :], MOBA_SLOPES[h])
             for h in range(g * MOBA_R, (g + 1) * MOBA_R)], axis=1)

    def block_partial(j, bias):
        res = []
        for g in range(MOBA_KV):
            kc = kaug[g, pl.ds(j * MOBA_BLOCK, MOBA_BLOCK), :]
            vtc = kv_ref[j, (MOBA_KV + g) * HEAD_DIM:(MOBA_KV + g + 1) * HEAD_DIM, :]
            s = _dot(kc, qaug_scr[g])
            if bias is not None:
                s = s + bias
            p, l, m = _softmax_cols(s)
            res.append((m, l, _dot(vtc, p.astype(BF16))))
        return res

    def past_block(j, carry):
        for g, (m, l, o) in enumerate(block_partial(j, None)):
            m_scr[g, pl.ds(j, 1), :] = m
            l_scr[g, pl.ds(j, 1), :] = l
            o_scr[g, j] = o
        return carry

    lax.fori_loop(0, tb, past_block, 0)
    ksub = lax.broadcasted_iota(I32, (MOBA_BLOCK, tq), 0)
    own_bias = jnp.where(tb * MOBA_BLOCK + ksub <= tl, 0.0, NEG)
    own = block_partial(tb, jnp.concatenate([own_bias] * MOBA_R, axis=1))

    row = lax.broadcasted_iota(I32, (nbp, MOBA_R * tq), 0)
    outs = []
    for g in range(MOBA_KV):
        m_o, l_o, o_o = own[g]
        gate = _dot(km_scr[:, g * HEAD_DIM:(g + 1) * HEAD_DIM], qaug_scr[g, 0:HEAD_DIM, :])
        sel = _topk_rows(jnp.where(row < tb, gate, -jnp.inf), min(MOBA_TOPK, n_b), n_b)
        mm = m_scr[g]
        mx = jnp.maximum(jnp.max(jnp.where(sel, mm, NEG), axis=0, keepdims=True), m_o)
        w = jnp.where(sel, jnp.exp(mm - mx), 0.0)
        w_o = jnp.exp(m_o - mx)
        den = jnp.sum(w * l_scr[g], axis=0, keepdims=True) + w_o * l_o
        num = w_o * o_o
        for j in range(n_b - 1):
            num = num + w[j:j + 1, :] * o_scr[g, j]
        og = num / den
        outs += [og[:, r * tq:(r + 1) * tq] for r in range(MOBA_R)]
    o_ref[...] = jnp.concatenate(outs, axis=0).T.astype(BF16)


def _moba_prompt(mq_t, mk, mkv_c, nb, s_len):
    tq = LANES
    nqt = s_len // tq
    rows = 2 * MOBA_KV * HEAD_DIM
    width = MOBA_HEADS * HEAD_DIM
    kw = MOBA_KV * HEAD_DIM
    n_b = s_len // MOBA_BLOCK
    nbp = 16
    assert n_b <= nbp
    e = np.zeros((nbp, s_len), np.float32)
    e[np.arange(s_len) // MOBA_BLOCK, np.arange(s_len)] = 1.0 / MOBA_BLOCK
    eavg = jnp.asarray(e, BF16)
    return pl.pallas_call(
        functools.partial(_moba_prompt_kernel, tq=tq, s_len=s_len),
        grid=(nb, nqt),
        in_specs=[pl.BlockSpec((None, width, tq), lambda b, i: (b, 0, i)),
                  pl.BlockSpec((s_len, kw), lambda b, i: (b, 0)),
                  pl.BlockSpec((None, n_b, rows, MOBA_BLOCK), lambda b, i: (b, 0, 0, 0)),
                  pl.BlockSpec(eavg.shape, lambda b, i: (0, 0))],
        out_specs=pl.BlockSpec((tq, width), lambda b, i: (b * nqt + i, 0)),
        out_shape=jax.ShapeDtypeStruct((nb * s_len, width), BF16),
        scratch_shapes=[pltpu.VMEM((MOBA_KV, s_len, LANES), BF16),
                        pltpu.VMEM((nbp, kw), BF16),
                        pltpu.VMEM((MOBA_KV, LANES, MOBA_R * tq), BF16),
                        pltpu.VMEM((MOBA_KV, nbp, MOBA_R * tq), F32), pltpu.VMEM((MOBA_KV, nbp, MOBA_R * tq), F32),
                        pltpu.VMEM((MOBA_KV, n_b, HEAD_DIM, MOBA_R * tq), F32)],
        compiler_params=pltpu.CompilerParams(dimension_semantics=("parallel", "arbitrary"),
                                             vmem_limit_bytes=VMEM_LIMIT),
        name="moba_prompt",
    )(mq_t, mk, mkv_c, eavg)


def _post_kernel(*refs, n_o, final):
    x_ref = refs[0]
    o_refs = refs[1:1 + n_o]
    wo_refs = refs[1 + n_o:1 + 2 * n_o]
    g_ref, wup_ref, wdn_ref, gf_ref, out_ref, x1_scr, h_scr, acc_scr = refs[1 + 2 * n_o:]
    j = pl.program_id(1)

    @pl.when(j == 0)
    def _():
        x1 = x_ref[...]
        for o_ref, wo_ref in zip(o_refs, wo_refs):
            x1 = x1 + _dot(o_ref[...], wo_ref[...])
        x1_scr[...] = x1
        ms = jnp.mean(x1 * x1, axis=-1, keepdims=True)
        h_scr[...] = ((x1 * lax.rsqrt(ms + RMS_EPS)) * g_ref[...]).astype(BF16)
        acc_scr[...] = jnp.zeros(acc_scr.shape, F32)

    u = jnp.maximum(_dot(h_scr[...], wup_ref[...]), 0.0)
    acc_scr[...] += _dot((u * u).astype(BF16), wdn_ref[...])

    @pl.when(j == pl.num_programs(1) - 1)
    def _():
        y = x1_scr[...] + acc_scr[...]
        if final:
            ms = jnp.mean(y * y, axis=-1, keepdims=True)
            y = (y * lax.rsqrt(ms + RMS_EPS)) * gf_ref[...]
        out_ref[...] = y


def _post(x2d, o_list, wo_list, g_mlp, w_up, w_down, g_final, final, tm, tf=1024):
    t_tot = x2d.shape[0]
    n_o = len(o_list)
    in_specs = [pl.BlockSpec((tm, D_MODEL), lambda i, j: (i, 0))]
    in_specs += [pl.BlockSpec((tm, o.shape[1]), lambda i, j: (i, 0)) for o in o_list]
    in_specs += [pl.BlockSpec(w.shape, lambda i, j: (0, 0)) for w in wo_list]
    in_specs += [pl.BlockSpec((1, D_MODEL), lambda i, j: (0, 0)),
                 pl.BlockSpec((D_MODEL, tf), lambda i, j: (0, j)),
                 pl.BlockSpec((tf, D_MODEL), lambda i, j: (j, 0)),
                 pl.BlockSpec((1, D_MODEL), lambda i, j: (0, 0))]
    return pl.pallas_call(
        functools.partial(_post_kernel, n_o=n_o, final=final),
        grid=(t_tot // tm, D_FF // tf),
        in_specs=in_specs,
        out_specs=pl.BlockSpec((tm, D_MODEL), lambda i, j: (i, 0)),
        out_shape=jax.ShapeDtypeStruct((t_tot, D_MODEL), F32),
        scratch_shapes=[pltpu.VMEM((tm, D_MODEL), F32), pltpu.VMEM((tm, D_MODEL), BF16),
                        pltpu.VMEM((tm, D_MODEL), F32)],
        compiler_params=pltpu.CompilerParams(dimension_semantics=("parallel", "arbitrary"),
                                             vmem_limit_bytes=VMEM_LIMIT),
        name="post_mlp",
    )(x2d, *o_list, *wo_list, g_mlp, w_up, w_down, g_final)


def _nsa_dec_kernel(q_ref, cmp_ref, mcs_ref, kw_ref, ocmp_ref, owin_ref, sel_ref, *, tq, q0, n_c, n_s):
    q = q_ref[...]
    cmp_tok = cmp_ref[...]
    mcs = mcs_ref[...]
    row4 = lax.broadcasted_iota(I32, (NSA_R * tq, 1), 0)
    t4 = q0 + _mod(row4, tq)
    r4 = _div(row4, tq)
    t1 = q0 + lax.broadcasted_iota(I32, (tq, 1), 0)
    nbp = mcs.shape[1]
    wlen = kw_ref.shape[1]
    for g in range(NSA_KV):
        qg4 = _stack_heads(q, g, NSA_R)
        slope4 = _row_slopes(r4, NSA_SLOPES, g, NSA_R)
        o_cmp, imp = _cmp_branch(qg4, cmp_tok, mcs, g, tq, t4, slope4, n_c)
        sel = _slc_select(imp, t1, n_s)
        sel_ref[:, g * nbp:(g + 1) * nbp] = jnp.where(sel, 1.0, 0.0)
        kwin = kw_ref[g * HEAD_DIM:(g + 1) * HEAD_DIM, :].astype(BF16)
        vwin = kw_ref[(NSA_KV + g) * HEAD_DIM:(NSA_KV + g + 1) * HEAD_DIM, :].astype(BF16)
        wp = (q0 - WINDOW) + lax.broadcasted_iota(I32, (NSA_R * tq, wlen), 1)
        dw = t4 - wp
        valid = (wp >= 0) & (dw >= 0) & (dw < WINDOW)
        p, l, _ = _softmax_parts(_dot(qg4, kwin) - slope4 * dw.astype(F32), valid)
        o_win = _dot_nt(p.astype(BF16), vwin) / l
        for r in range(NSA_R):
            h = g * NSA_R + r
            ocmp_ref[:, h * HEAD_DIM:(h + 1) * HEAD_DIM] = o_cmp[r * tq:(r + 1) * tq]
            owin_ref[:, h * HEAD_DIM:(h + 1) * HEAD_DIM] = o_win[r * tq:(r + 1) * tq]


def _nsa_dec(q8, cmp_tok, kwin_t, nb, q0, n_c, n_s):
    tq = q8.shape[1]
    nbp = -(-n_s // LANES) * LANES
    mcs = _cmp_to_slc(cmp_tok.shape[1], n_c, n_s, nbp)
    width = NSA_HEADS * HEAD_DIM
    return pl.pallas_call(
        functools.partial(_nsa_dec_kernel, tq=tq, q0=q0, n_c=n_c, n_s=n_s),
        grid=(nb,),
        in_specs=[pl.BlockSpec((None, tq, width), lambda b: (b, 0, 0)),
                  pl.BlockSpec((None,) + cmp_tok.shape[1:], lambda b: (b, 0, 0)),
                  pl.BlockSpec(mcs.shape, lambda b: (0, 0)),
                  pl.BlockSpec((None,) + kwin_t.shape[1:], lambda b: (b, 0, 0))],
        out_specs=[pl.BlockSpec((None, tq, width), lambda b: (b, 0, 0)),
                   pl.BlockSpec((None, tq, width), lambda b: (b, 0, 0)),
                   pl.BlockSpec((None, tq, NSA_KV * nbp), lambda b: (b, 0, 0))],
        out_shape=[jax.ShapeDtypeStruct((nb, tq, width), F32),
                   jax.ShapeDtypeStruct((nb, tq, width), F32),
                   jax.ShapeDtypeStruct((nb, tq, NSA_KV * nbp), F32)],
        compiler_params=pltpu.CompilerParams(dimension_semantics=("parallel",), vmem_limit_bytes=VMEM_LIMIT),
        name="nsa_decode_cmp_win",
    )(q8, cmp_tok, mcs, kwin_t)


def _stream_kernel(*refs, mode, nseg, ppseg, n_rows, hk, q0, n_segs):
    pps = nseg * ppseg
    refs = refs[1:]
    qbd_ref, slope_ref, t_ref = refs[0:3]
    k = 3
    sel_ref = selnew_ref = em_ref = None
    if mode == "slc":
        sel_ref, selnew_ref, em_ref = refs[k:k + 3]
        k += 3
    pages = refs[k:k + pps]
    k += pps
    lf_pages = None
    if mode == "fox":
        lf_pages = refs[k:k + pps]
        k += pps
    new_ref = refs[k]
    k += 1
    newlf_ref = tri_ref = None
    if mode == "fox":
        newlf_ref, tri_ref = refs[k], refs[k + 1]
        k += 2
    out_ref = refs[k]
    oparts, m_s, l_s, x_s = refs[k + 1:k + 5]
    st = pl.program_id(1)
    nsteps = pl.num_programs(1)
    qbd = qbd_ref[...]
    slope = slope_ref[...]
    tcol = t_ref[...]
    lane = lax.broadcasted_iota(I32, (n_rows, LANES), 1)
    seg_lane = lax.broadcasted_iota(I32, m_s.shape, 1)
    reps = n_rows // FOX_HEADS

    @pl.when(st == 0)
    def _():
        m_s[...] = jnp.zeros(m_s.shape, F32)
        l_s[...] = jnp.zeros(l_s.shape, F32)
        x_s[...] = jnp.zeros(x_s.shape, F32)

    def scores(page, page_pos0, lf_page, run):
        kt = page[0:hk, :].astype(BF16)
        vt = page[hk:2 * hk, :].astype(BF16)
        s_raw = _dot(qbd, kt)
        if mode == "fox":
            cs = _dot01(lf_page, tri_ref[...]) + run
            run = cs[:, LANES - 1:LANES]
            s = s_raw - jnp.concatenate([cs] * reps, axis=0)
        else:
            s = s_raw - slope * (tcol - (page_pos0 + lane)).astype(F32)
        return s_raw, s, vt, run

    def sel_mask(sref, width):
        mexp = _dot(sref[...].astype(BF16), em_ref[:, 0:width])
        return jnp.where(mexp > 0.5, 0.0, NEG)

    m_all, l_all, x_all = m_s[...], l_s[...], x_s[...]
    mb = sel_mask(sel_ref, pps * PAGE) if mode == "slc" else None
    for sg in range(nseg):
        run = jnp.zeros((FOX_HEADS, 1), F32)
        ss, vts, raws = [], [], []
        for i in range(sg * ppseg, (sg + 1) * ppseg):
            s_raw, s, vt, run = scores(pages[i][...], (st * pps + i) * PAGE,
                                       lf_pages[i][...] if mode == "fox" else None, run)
            if mode == "slc":
                s = s + jnp.concatenate([mb[:, i * PAGE:(i + 1) * PAGE]] * (n_rows // mb.shape[0]), axis=0)
            ss.append(s)
            vts.append(vt)
            raws.append(s_raw)
        m = ss[0].max(axis=-1, keepdims=True)
        for s in ss[1:]:
            m = jnp.maximum(m, s.max(axis=-1, keepdims=True))
        l = jnp.zeros((n_rows, 1), F32)
        o = jnp.zeros((n_rows, hk), F32)
        for s, vt in zip(ss, vts):
            p = jnp.exp(s - m)
            l = l + p.sum(axis=-1, keepdims=True)
            o = o + _dot_nt(p.astype(BF16), vt)
        seg = st * nseg + sg
        oparts[seg] = o
        m_all = jnp.where(seg_lane == seg, m, m_all)
        l_all = jnp.where(seg_lane == seg, l, l_all)
        if mode == "fox":
            x_all = jnp.where(seg_lane == seg, jnp.concatenate([run] * reps, axis=0), x_all)
        elif mode == "moba":
            gsum = raws[0].sum(axis=-1, keepdims=True)
            for r_ in raws[1:]:
                gsum = gsum + r_.sum(axis=-1, keepdims=True)
            x_all = jnp.where(seg_lane == seg, gsum * (1.0 / MOBA_BLOCK), x_all)
    m_s[...] = m_all
    l_s[...] = l_all
    x_s[...] = x_all

    @pl.when(st == nsteps - 1)
    def _():
        last = n_segs - 1
        _, s, vt, _ = scores(new_ref[...], q0, newlf_ref[...] if mode == "fox" else None,
                             jnp.zeros((FOX_HEADS, 1), F32))
        if mode == "slc":
            mnew = sel_mask(selnew_ref, PAGE)
            s = s + jnp.concatenate([mnew] * (n_rows // mnew.shape[0]), axis=0)
        p, l, m = _softmax_parts(s, ((q0 + lane) <= tcol) & (s > 0.5 * NEG))
        oparts[last] = _dot_nt(p.astype(BF16), vt)
        mm = jnp.where(seg_lane == last, m, m_all)
        ll = jnp.where(seg_lane == last, l, l_all)
        if mode == "fox":
            mm = mm + _dot01(x_all, tri_ref[...], nt=True)
            valid = seg_lane < n_segs
        elif mode == "slc":
            valid = seg_lane < n_segs
        else:
            n_b = n_segs - 1
            gate = jnp.where(seg_lane < n_b, x_all, -jnp.inf)
            valid = _topk_mask(gate, min(MOBA_TOPK, n_b), n_b) | (seg_lane == last)
        mx = jnp.max(jnp.where(valid, mm, NEG), axis=-1, keepdims=True)
        w = jnp.where(valid, jnp.exp(mm - mx), 0.0)
        den = jnp.maximum(jnp.sum(w * ll, axis=-1, keepdims=True), 1e-30)
        num = jnp.zeros((n_rows, hk), F32)
        for seg in range(n_segs):
            num = num + w[:, seg:seg + 1] * oparts[seg]
        out_ref[...] = num / den


def _stream(mode, qbd, slope_col, t_col, sel_steps, pool_t, lf_pool_t, new_t, newlf_t, page_table, q0, nseg, ppseg):
    nb, n_rows, hk = qbd.shape
    n_pages = page_table.shape[1]
    pps = nseg * ppseg
    nsteps = n_pages // pps
    n_segs = nsteps * nseg + 1
    segp = LANES
    assert n_segs <= segp and n_pages % pps == 0

    def cst(shape):
        nd = len(shape)
        return pl.BlockSpec(shape, lambda b, s, pt: (0,) * nd)

    def per_b(shape):
        nd = len(shape)
        return pl.BlockSpec((None,) + shape, lambda b, s, pt: (b,) + (0,) * nd)

    in_specs = [per_b((n_rows, hk)), cst((n_rows, 1)), cst((n_rows, 1))]
    args = [qbd, slope_col, t_col]
    if mode == "slc":
        nsel = sel_steps.shape[2]
        blocks_per_step = pps * (PAGE // SLC_BLOCK)
        assert blocks_per_step <= LANES and sel_steps.shape[1] == nsteps + 1
        em = np.zeros((LANES, pps * PAGE), np.float32)
        em[np.arange(pps * PAGE) // SLC_BLOCK, np.arange(pps * PAGE)] = 1.0
        em = jnp.asarray(em, BF16)
        in_specs += [pl.BlockSpec((None, None, nsel, LANES), lambda b, s, pt: (b, s, 0, 0)),
                     pl.BlockSpec((None, None, nsel, LANES), lambda b, s, pt: (b, nsteps, 0, 0)),
                     cst(em.shape)]
        args += [sel_steps, sel_steps, em]
    for i in range(pps):
        in_specs.append(pl.BlockSpec((None, 2 * hk, PAGE), lambda b, s, pt, i=i: (pt[b, s * pps + i], 0, 0)))
        args.append(pool_t)
    if mode == "fox":
        for i in range(pps):
            in_specs.append(pl.BlockSpec((None, FOX_HEADS, PAGE), lambda b, s, pt, i=i: (pt[b, s * pps + i], 0, 0)))
            args.append(lf_pool_t)
    in_specs.append(per_b((2 * hk, PAGE)))
    args.append(new_t)
    if mode == "fox":
        tri = _tri_matrix()
        in_specs += [per_b((FOX_HEADS, PAGE)), cst(tri.shape)]
        args += [newlf_t, tri]
    grid_spec = pltpu.PrefetchScalarGridSpec(
        num_scalar_prefetch=1, grid=(nb, nsteps), in_specs=in_specs,
        out_specs=pl.BlockSpec((None, n_rows, hk), lambda b, s, pt: (b, 0, 0)),
        scratch_shapes=[pltpu.VMEM((n_segs, n_rows, hk), F32), pltpu.VMEM((n_rows, segp), F32),
                        pltpu.VMEM((n_rows, segp), F32), pltpu.VMEM((n_rows, segp), F32)])
    return pl.pallas_call(
        functools.partial(_stream_kernel, mode=mode, nseg=nseg, ppseg=ppseg, n_rows=n_rows, hk=hk, q0=q0,
                          n_segs=n_segs),
        grid_spec=grid_spec,
        out_shape=jax.ShapeDtypeStruct((nb, n_rows, hk), F32),
        compiler_params=pltpu.CompilerParams(dimension_semantics=("parallel", "arbitrary"),
                                             vmem_limit_bytes=VMEM_LIMIT),
        name="decode_stream_" + mode,
    )(page_table, *args)


def _nsa_gate_kernel(oc_ref, os_ref, ow_ref, gl_ref, o_ref):
    sig = 1.0 / (1.0 + jnp.exp(-gl_ref[...]))
    for h in range(NSA_HEADS):
        sl = slice(h * HEAD_DIM, (h + 1) * HEAD_DIM)
        o_ref[:, sl] = (oc_ref[:, sl] * sig[:, 3 * h:3 * h + 1] + os_ref[:, sl] * sig[:, 3 * h + 1:3 * h + 2]
                        + ow_ref[:, sl] * sig[:, 3 * h + 2:3 * h + 3]).astype(BF16)


def _nsa_gate(oc, os_, ow, gl):
    return pl.pallas_call(
        _nsa_gate_kernel,
        out_shape=jax.ShapeDtypeStruct(oc.shape, BF16),
        name="nsa_gate",
    )(oc, os_, ow, gl)


def _pool_t(cache, li):
    c = jnp.transpose(cache[li], (0, 2, 3, 4, 1))
    return c.reshape(c.shape[0], -1, c.shape[-1])


def _block_diag_q(q, n_tok, n_kv, n_r, r_major=False):
    b = q.shape[0]
    q5 = q.reshape(b, n_tok, n_kv, n_r, 1, HEAD_DIM)
    eye = jnp.eye(n_kv, dtype=q.dtype).reshape(1, 1, n_kv, 1, n_kv, 1)
    x = (q5 * eye).reshape(b, n_tok, n_kv, n_r, n_kv * HEAD_DIM)
    if r_major:
        x = x.transpose(0, 3, 1, 2, 4)
    return x.reshape(b, n_tok * n_kv * n_r, n_kv * HEAD_DIM)


def _diag_heads(o, n_tok, n_kv, n_r, r_major=False):
    b = o.shape[0]
    if r_major:
        o6 = o.reshape(b, n_r, n_tok, n_kv, n_kv, HEAD_DIM).transpose(0, 2, 3, 1, 4, 5)
    else:
        o6 = o.reshape(b, n_tok, n_kv, n_r, n_kv, HEAD_DIM)
    d = jnp.einsum("btgrgd->btgrd", o6)
    return d.reshape(b * n_tok, n_kv * n_r * HEAD_DIM)


def _row_consts(n_tok, n_kv, n_r, slopes, q0, r_major=False):
    t, g, r = np.meshgrid(np.arange(n_tok), np.arange(n_kv), np.arange(n_r), indexing="ij")
    h = g * n_r + r
    if r_major:
        t, h = t.transpose(2, 0, 1), h.transpose(2, 0, 1)
    t, h = t.reshape(-1), h.reshape(-1)
    sl = np.asarray(slopes, np.float32)[h] if slopes is not None else np.zeros(h.shape, np.float32)
    return jnp.asarray(sl.reshape(-1, 1), F32), jnp.asarray((q0 + t).reshape(-1, 1), I32)


def _new_pages(kv_t, nb, n_tok):
    rows = kv_t.shape[0]
    x = kv_t.reshape(rows, nb, n_tok).transpose(1, 0, 2)
    return jnp.pad(x, ((0, 0), (0, 0), (0, PAGE - n_tok)))


def kernel(x_prompt, x_sample, cache_nsa_cmp_kv, cache_nsa_slc_kv, state_nsa_win_kv, cache_fox_kv, cache_fox_lf,
           cache_moba_kv, page_table, norm_mix, norm_mlp, w_in_even, b_fgt, w_out_even, cmp_pe, cmp_w1, cmp_w2,
           w_in_odd, w_out_odd, w_up, w_down, norm_final):
    nb_p, s_len, _ = x_prompt.shape
    nb_d, n_tok, _ = x_sample.shape
    n_pages = page_table.shape[1]
    past = n_pages * PAGE
    nq_w = NSA_HEADS * HEAD_DIM
    nkv_w = 2 * NSA_KV * HEAD_DIM
    fq_w = FOX_HEADS * HEAD_DIM
    fkv_w = 2 * FOX_HEADS * HEAD_DIM
    mq_w = MOBA_HEADS * HEAD_DIM
    mkv_w = 2 * MOBA_KV * HEAD_DIM
    ngl = 3 * NSA_HEADS

    wte = jnp.transpose(w_in_even[0])
    o_nq, o_ckv, o_skv, o_wkv = 0, nq_w, nq_w + nkv_w, nq_w + 2 * nkv_w
    o_gl = nq_w + 3 * nkv_w
    o_fq = o_gl + ngl
    o_fkv = o_fq + fq_w
    o_fl = o_fkv + fkv_w
    wn_e = jnp.concatenate([wte[o_nq:o_nq + nq_w], wte[o_fq:o_fq + fq_w], wte[o_gl:o_gl + ngl],
                            jnp.zeros((LANES - ngl, D_MODEL), F32)], axis=0)
    wn_e = jnp.transpose(wn_e).astype(BF16)
    wt_e = jnp.concatenate([wte[o_ckv:o_ckv + 3 * nkv_w], wte[o_fkv:o_fkv + fkv_w], wte[o_fl:o_fl + FOX_HEADS]],
                           axis=0).astype(BF16)
    bias_e = b_fgt[0].reshape(FOX_HEADS, 1)
    nat_e = [(0, nq_w, "q"), (nq_w, fq_w, "q"), (nq_w + fq_w, LANES, "f32")]
    r_skv, r_wkv, r_fkv, r_fl = nkv_w, 2 * nkv_w, 3 * nkv_w, 3 * nkv_w + fkv_w
    tr_e_dec = [(0, nkv_w, "f32"), (r_skv, nkv_w, "f32"), (r_wkv, nkv_w, "f32"), (r_fkv, fkv_w, "f32"),
                (r_fl, FOX_HEADS, "lf")]
    wo_t = jnp.transpose(w_in_odd[0])
    wn_o = w_in_odd[0][:, :mq_w].astype(BF16)
    wt_o = wo_t[mq_w:].astype(BF16)
    bias_o = jnp.zeros((8, 1), F32)
    nat_o = [(0, mq_w, "q")]
    tr_o_dec = [(0, mkv_w, "f32")]
    hk_n = NSA_KV * HEAD_DIM
    wn_ep = jnp.transpose(jnp.concatenate([wte[o_fq:o_fq + fq_w], wte[o_skv:o_skv + hk_n],
                                           wte[o_wkv:o_wkv + hk_n]], axis=0)).astype(BF16)
    wt_ep = jnp.concatenate([wte[o_ckv:o_ckv + 3 * nkv_w], wte[o_fkv:o_fkv + fkv_w], wte[o_fl:o_fl + FOX_HEADS],
                             wte[o_nq:o_nq + nq_w], wte[o_gl:o_gl + ngl]], axis=0).astype(BF16)
    nat_ep = [(0, fq_w, "q"), (fq_w, hk_n, "kb"), (fq_w + hk_n, hk_n, "kb")]
    r_nq = r_fl + FOX_HEADS
    r_gl = r_nq + nq_w
    tr_e_prompt = [(0, nkv_w, "f32"), (r_skv, nkv_w, "f32"), (r_skv, nkv_w, "bf16c2"), (r_wkv, nkv_w, "f32"),
                   (r_wkv, nkv_w, "bf16c"), (r_fkv, fkv_w, "f32"), (r_fkv, fkv_w, "bf16"), (r_fl, FOX_HEADS, "lf"),
                   (r_nq, nq_w, "qt"), (r_gl, ngl, "f32")]
    hk_m = MOBA_KV * HEAD_DIM
    wn_op = w_in_odd[0][:, mq_w:mq_w + hk_m].astype(BF16)
    wt_op = jnp.concatenate([wo_t[mq_w:], wo_t[:mq_w]], axis=0).astype(BF16)
    nat_op = [(0, hk_m, "kb")]
    tr_o_prompt = [(0, mkv_w, "f32"), (0, mkv_w, "bf16c2"), (mkv_w, mq_w, "qt")]
    woe = w_out_even[0].astype(BF16)
    woe_a, woe_b = woe[:nq_w], woe[nq_w:]
    woo = w_out_odd[0].astype(BF16)
    wup = w_up.astype(BF16)
    wdn = w_down.astype(BF16)
    g_mix = norm_mix.reshape(norm_mix.shape[0], 1, D_MODEL)
    g_mlp = norm_mlp.reshape(norm_mlp.shape[0], 1, D_MODEL)
    g_fin = norm_final.reshape(1, D_MODEL)
    cw = _compress_weights(cmp_pe[0], cmp_w1[0], cmp_w2[0])

    xp = x_prompt.reshape(nb_p * s_len, D_MODEL)
    tm_p = 512
    (fq, sk_n, wk_n, ckv_t, skv_t, skv_c, wkv_t, wkv_c, fkv_t, fkv_b, lf_t, nq_t, gl_t) = _proj(
        xp, g_mix[0], wn_ep, wt_ep, bias_e, nat_ep, tr_e_prompt, nb_p, s_len, tm_p)
    cmp_tok = _compress(ckv_t, None, cw, nb_p, s_len // PAGE, s_len // PAGE, paged=False)
    o_nsa = _nsa_prompt(nq_t, gl_t, cmp_tok, sk_n, wk_n, skv_c, wkv_c, nb_p, s_len)
    o_fox = _fox_prompt(fq, fkv_b, lf_t, nb_p, s_len)
    xp = _post(xp, [o_nsa, o_fox], [woe_a, woe_b], g_mlp[0], wup[0], wdn[0], g_fin, False, tm_p)
    mk_n, mkv_t, mkv_c, mq_t = _proj(xp, g_mix[1], wn_op, wt_op, bias_o, nat_op, tr_o_prompt, nb_p, s_len, tm_p)
    o_moba = _moba_prompt(mq_t, mk_n, mkv_c, nb_p, s_len)
    yp = _post(xp, [o_moba], [woo], g_mlp[1], wup[1], wdn[1], g_fin, True, tm_p)
    y_prompt = yp.reshape(nb_p, s_len, D_MODEL)

    def kv_out(t, n_h):
        b, _, s = t.shape
        return jnp.transpose(t.reshape(b, 2, n_h, HEAD_DIM, s), (0, 4, 1, 2, 3))[None]

    p_cmp = kv_out(ckv_t, NSA_KV)
    p_slc = kv_out(skv_t, NSA_KV)
    wb = min(WINDOW, s_len)
    p_win = kv_out(wkv_t[:, :, s_len - wb:], NSA_KV)
    p_fox = kv_out(fkv_t, FOX_HEADS)
    p_lf = jnp.transpose(lf_t, (0, 2, 1))[None]
    p_moba = kv_out(mkv_t, MOBA_KV)

    td = nb_d * n_tok
    xd = x_sample.reshape(td, D_MODEL)
    (nq_d, fq_d, gl_d, ckv_d, skv_d, wkv_d, fkv_d, lf_d) = _proj(
        xd, g_mix[0], wn_e, wt_e, bias_e, nat_e, tr_e_dec, 1, td, td)

    def kv_out_dec(t, n_h):
        return jnp.transpose(t[0]).reshape(1, nb_d, n_tok, 2, n_h, HEAD_DIM)

    s_cmp = kv_out_dec(ckv_d, NSA_KV)
    s_slc = kv_out_dec(skv_d, NSA_KV)
    s_fox = kv_out_dec(fkv_d, FOX_HEADS)
    s_lf = jnp.transpose(lf_d[0]).reshape(1, nb_d, n_tok, FOX_HEADS)
    win_state_t = jnp.transpose(state_nsa_win_kv[0], (0, 2, 3, 4, 1)).reshape(nb_d, nkv_w, -1)
    wkv_new = wkv_d[0].reshape(nkv_w, nb_d, n_tok).transpose(1, 0, 2)
    win_all = jnp.concatenate([win_state_t, wkv_new], axis=2)
    wbuf = win_state_t.shape[2]
    s_win = jnp.transpose(win_all[:, :, -wbuf:].reshape(nb_d, 2, NSA_KV, HEAD_DIM, wbuf), (0, 4, 1, 2, 3))[None]
    assert wbuf == WINDOW
    wpad = -(-(wbuf + n_tok) // LANES) * LANES
    kwin_t = jnp.pad(win_all, ((0, 0), (0, 0), (0, wpad - wbuf - n_tok)))

    assert (past + n_tok) // CMP_STRIDE == past // CMP_STRIDE
    cmp_pool = _pool_t(cache_nsa_cmp_kv, 0)
    cmp_tok_d = _compress(cmp_pool, page_table, cw, nb_d, n_pages, 32, paged=True)
    n_c = past // CMP_STRIDE - CMP_LEN // CMP_STRIDE + 1
    n_s = -(-(past + n_tok) // SLC_BLOCK)
    tq_d = 8
    q8 = jnp.pad(nq_d.reshape(nb_d, n_tok, nq_w), ((0, 0), (0, tq_d - n_tok), (0, 0)))
    o_cmp8, o_win8, sel8 = _nsa_dec(q8, cmp_tok_d, kwin_t, nb_d, past, n_c, n_s)
    o_cmp_d = o_cmp8[:, :n_tok].reshape(td, nq_w)
    o_win_d = o_win8[:, :n_tok].reshape(td, nq_w)
    nbp = sel8.shape[2] // NSA_KV
    pps_d = 16
    nsteps_d = n_pages // pps_d
    bps = pps_d * (PAGE // SLC_BLOCK)
    sel_tg = sel8[:, :n_tok].reshape(nb_d, n_tok * NSA_KV, nbp)[:, :, :n_s]
    sel_tg = jnp.pad(sel_tg, ((0, 0), (0, 0), (0, (nsteps_d + 1) * bps - n_s)))
    sel_steps = sel_tg.reshape(nb_d, n_tok * NSA_KV, nsteps_d + 1, bps).transpose(0, 2, 1, 3)
    sel_steps = jnp.pad(sel_steps, ((0, 0), (0, 0), (0, 0), (0, LANES - bps)))
    sl_nsa, t_nsa = _row_consts(n_tok, NSA_KV, NSA_R, NSA_SLOPES, past, r_major=True)
    qbd_s = _block_diag_q(nq_d.reshape(nb_d, n_tok, nq_w), n_tok, NSA_KV, NSA_R, r_major=True)
    new_s = _new_pages(skv_d[0], nb_d, n_tok)
    o_slc_bd = _stream("slc", qbd_s, sl_nsa, t_nsa, sel_steps, _pool_t(cache_nsa_slc_kv, 0), None, new_s, None,
                       page_table, past, 1, pps_d)
    o_slc_d = _diag_heads(o_slc_bd, n_tok, NSA_KV, NSA_R, r_major=True)
    o_nsa_d = _nsa_gate(o_cmp_d, o_slc_d, o_win_d, gl_d)
    sl_fox, t_fox = _row_consts(n_tok, FOX_HEADS, 1, None, past)
    qbd_f = _block_diag_q(fq_d.reshape(nb_d, n_tok, fq_w), n_tok, FOX_HEADS, 1)
    new_f = _new_pages(fkv_d[0], nb_d, n_tok)
    newlf = _new_pages(lf_d[0], nb_d, n_tok)
    lf_pool = jnp.transpose(cache_fox_lf[0], (0, 2, 1))
    o_fox_bd = _stream("fox", qbd_f, sl_fox, t_fox, None, _pool_t(cache_fox_kv, 0), lf_pool, new_f, newlf,
                       page_table, past, 1, pps_d)
    o_fox_d = _diag_heads(o_fox_bd, n_tok, FOX_HEADS, 1).astype(BF16)
    xd = _post(xd, [o_nsa_d, o_fox_d], [woe_a, woe_b], g_mlp[0], wup[0], wdn[0], g_fin, False, td)
    mq_d, mkv_d = _proj(xd, g_mix[1], wn_o, wt_o, bias_o, nat_o, tr_o_dec, 1, td, td)
    s_moba = kv_out_dec(mkv_d, MOBA_KV)
    assert past % MOBA_BLOCK == 0 and MOBA_BLOCK == 2 * PAGE
    sl_m, t_m = _row_consts(n_tok, MOBA_KV, MOBA_R, MOBA_SLOPES, past)
    qbd_m = _block_diag_q(mq_d.reshape(nb_d, n_tok, mq_w), n_tok, MOBA_KV, MOBA_R)
    new_m = _new_pages(mkv_d[0], nb_d, n_tok)
    o_moba_bd = _stream("moba", qbd_m, sl_m, t_m, None, _pool_t(cache_moba_kv, 0), None, new_m, None,
                        page_table, past, pps_d // (MOBA_BLOCK // PAGE), MOBA_BLOCK // PAGE)
    o_moba_d = _diag_heads(o_moba_bd, n_tok, MOBA_KV, MOBA_R).astype(BF16)
    yd = _post(xd, [o_moba_d], [woo], g_mlp[1], wup[1], wdn[1], g_fin, True, td)
    y_sample = yd.reshape(nb_d, n_tok, D_MODEL)

    return (y_prompt, y_sample, p_cmp, s_cmp, p_slc, s_slc, p_win, s_win, p_fox, s_fox, p_lf, s_lf, p_moba, s_moba)
```

```python
import functools
import math

import numpy as np
import jax
import jax.numpy as jnp
from jax import lax
from jax.experimental import pallas as pl
from jax.experimental.pallas import tpu as pltpu

F32 = jnp.float32
BF16 = jnp.bfloat16
I32 = jnp.int32

D_MODEL = 1024
HEAD_DIM = 64
NSA_HEADS = 8
NSA_KV = 2
NSA_R = NSA_HEADS // NSA_KV
CMP_LEN = 32
CMP_STRIDE = 16
CMP_HIDDEN = 128
SLC_BLOCK = 64
SLC_TOPN = 16
WINDOW = 512
FOX_HEADS = 8
MOBA_HEADS = 16
MOBA_KV = 4
MOBA_R = MOBA_HEADS // MOBA_KV
MOBA_BLOCK = 256
MOBA_TOPK = 3
D_FF = 4 * D_MODEL
PAGE = 128
RMS_EPS = 1e-6
NEG = -1e30
SCALE = HEAD_DIM ** -0.5
LANES = 128
KEY_STEP = 512
VMEM_LIMIT = 56 * 1024 * 1024

NT_DIMS = (((1,), (1,)), ((), ()))


def _alibi(n):
    return [float(np.float32(2.0 ** (-8.0 * (i + 1) / n))) for i in range(n)]


NSA_SLOPES = _alibi(NSA_HEADS)
MOBA_SLOPES = _alibi(MOBA_HEADS)


def _div(x, n):
    assert n & (n - 1) == 0
    return jnp.right_shift(x, int(math.log2(n)))


def _mod(x, n):
    assert n & (n - 1) == 0
    return jnp.bitwise_and(x, n - 1)


def _dot(a, b):
    return jnp.dot(a, b, preferred_element_type=F32)


def _dot_nt(a, b):
    return lax.dot_general(a, b, NT_DIMS, preferred_element_type=F32)


def _split3(x):
    hi = x.astype(BF16)
    r1 = x - hi.astype(F32)
    mid = r1.astype(BF16)
    lo = (r1 - mid.astype(F32)).astype(BF16)
    return hi, mid, lo


def _dot01(x, mat, nt=False):
    n = x.shape[0]
    st = jnp.concatenate(_split3(x), axis=0)
    y = _dot_nt(st, mat) if nt else _dot(st, mat)
    return y[0:n] + y[n:2 * n] + y[2 * n:3 * n]


def _softmax_parts(s, valid):
    s = jnp.where(valid, s, NEG)
    m = jnp.max(s, axis=-1, keepdims=True)
    p = jnp.where(valid, jnp.exp(s - m), 0.0)
    l = jnp.maximum(jnp.sum(p, axis=-1, keepdims=True), 1e-30)
    return p, l, m


def _softmax_bias(logits):
    m = jnp.max(logits, axis=-1, keepdims=True)
    p = jnp.exp(logits - m)
    return p, jnp.sum(p, axis=-1, keepdims=True), m


def _topk_mask(v, nsel, nb):
    lane = lax.broadcasted_iota(I32, v.shape, 1)
    cnt = jnp.zeros(v.shape, I32)
    for k in range(nb):
        col = v[:, k:k + 1]
        beats = (col > v) | ((col == v) & (lane > k))
        cnt = cnt + jnp.where(beats, 1, 0)
    return (cnt < nsel) & (v > -jnp.inf)


def _dot01_rhs(mat, x):
    n = x.shape[1]
    y = _dot(mat, jnp.concatenate(_split3(x), axis=1))
    return y[:, 0:n] + y[:, n:2 * n] + y[:, 2 * n:3 * n]


def _bf16_parts(x):
    def rnd(v):
        u = np.float32(v).view(np.uint32)
        u = np.uint32((int(u) + 0x7FFF + ((int(u) >> 16) & 1)) & 0xFFFF0000)
        return float(u.view(np.float32))
    hi = rnd(x)
    mid = rnd(np.float32(np.float32(x) - np.float32(hi)))
    lo = rnd(np.float32(np.float32(x) - np.float32(hi) - np.float32(mid)))
    return hi, mid, lo


N_AUG = 6


def _aug_keys(k, pos):
    lane = lax.broadcasted_iota(I32, k.shape, 1)
    a = _div(pos, LANES).astype(F32)
    b = _mod(pos, LANES).astype(F32)
    aug = jnp.where(lane < 3, a, jnp.where(lane < N_AUG, b, 0.0)).astype(BF16)
    return jnp.concatenate([k, aug], axis=1)


def _aug_query(qt, slope):
    parts = _bf16_parts(slope)
    vals = [float(LANES) * p for p in parts] + list(parts)
    sub = lax.broadcasted_iota(I32, qt.shape, 0)
    aug = jnp.zeros(qt.shape, F32)
    for i, v in enumerate(vals):
        aug = jnp.where(sub == i, v, aug)
    return jnp.concatenate([qt, aug.astype(BF16)], axis=0)


def _softmax_cols(logits):
    m = jnp.max(logits, axis=0, keepdims=True)
    p = jnp.exp(logits - m)
    return p, jnp.sum(p, axis=0, keepdims=True), m


def _topk_rows(v, nsel, nb):
    row = lax.broadcasted_iota(I32, v.shape, 0)
    cnt = jnp.zeros(v.shape, I32)
    for k in range(nb):
        rk = v[k:k + 1, :]
        beats = (rk > v) | ((rk == v) & (row > k))
        cnt = cnt + jnp.where(beats, 1, 0)
    return (cnt < nsel) & (v > -jnp.inf)


def _flash_step(h, kc, vtc, qaug, bias, m_st, l_st, acc_st):
    s = _dot(kc, qaug)
    if bias is not None:
        s = s + bias
    m_old = m_st[h, 0:1, :]
    m_new = jnp.maximum(m_old, jnp.max(s, axis=0, keepdims=True))
    alpha = jnp.exp(m_old - m_new)
    p = jnp.exp(s - m_new)
    l_st[h, 0:1, :] = alpha * l_st[h, 0:1, :] + jnp.sum(p, axis=0, keepdims=True)
    acc_st[h] = alpha * acc_st[h] + _dot(vtc, p.astype(BF16))
    m_st[h, 0:1, :] = m_new


def _stack_heads(q, g, nr):
    return jnp.concatenate([q[:, (g * nr + r) * HEAD_DIM:(g * nr + r + 1) * HEAD_DIM] for r in range(nr)], axis=0)


def _row_slopes(r4, slopes, g, nr):
    out = jnp.zeros(r4.shape, F32)
    for r in range(nr):
        out = jnp.where(r4 == r, slopes[g * nr + r], out)
    return out


_CHUNKED = {"bf16c": LANES, "bf16c2": MOBA_BLOCK}


def _proj_kernel(x_ref, g_ref, wn_ref, wt_ref, b_ref, *outs, nat, tr):
    x = x_ref[...]
    ms = jnp.mean(x * x, axis=-1, keepdims=True)
    h = (x * lax.rsqrt(ms + RMS_EPS)) * g_ref[...]
    hb = h.astype(BF16)
    yn = _dot(hb, wn_ref[...])
    yt = _dot_nt(wt_ref[...], hb)
    k = 0
    for (c0, w, kind) in nat:
        v = yn[:, c0:c0 + w]
        if kind == "q":
            outs[k][...] = (v * SCALE).astype(BF16)
        elif kind == "kb":
            outs[k][...] = v.astype(BF16)
        else:
            outs[k][...] = v
        k += 1
    for (r0, n, kind) in tr:
        v = yt[r0:r0 + n, :]
        if kind == "f32":
            outs[k][...] = v
        elif kind == "bf16":
            outs[k][...] = v.astype(BF16)
        elif kind == "qt":
            outs[k][...] = (v * SCALE).astype(BF16)
        elif kind in _CHUNKED:
            cw = _CHUNKED[kind]
            vb = v.astype(BF16)
            for c in range(v.shape[1] // cw):
                outs[k][c] = vb[:, c * cw:(c + 1) * cw]
        else:
            z = v + b_ref[...]
            outs[k][...] = jnp.minimum(z, 0.0) - jnp.log(1.0 + jnp.exp(-jnp.abs(z)))
        k += 1


def _proj(x2d, gain, wn, wt, bias, nat, tr, nb, s_len, tm):
    t_tot = x2d.shape[0]
    tpb = s_len // tm
    assert nb * s_len == t_tot
    out_shape, out_specs = [], []
    for (c0, w, kind) in nat:
        out_shape.append(jax.ShapeDtypeStruct((t_tot, w), BF16 if kind in ("q", "kb") else F32))
        out_specs.append(pl.BlockSpec((tm, w), lambda i: (i, 0)))
    for (r0, n, kind) in tr:
        if kind in _CHUNKED:
            cw = _CHUNKED[kind]
            out_shape.append(jax.ShapeDtypeStruct((nb, s_len // cw, n, cw), BF16))
            out_specs.append(pl.BlockSpec((None, tm // cw, n, cw), lambda i: (i // tpb, i % tpb, 0, 0)))
        else:
            out_shape.append(jax.ShapeDtypeStruct((nb, n, s_len), BF16 if kind in ("bf16", "qt") else F32))
            out_specs.append(pl.BlockSpec((None, n, tm), lambda i: (i // tpb, 0, i % tpb)))
    return pl.pallas_call(
        functools.partial(_proj_kernel, nat=tuple(nat), tr=tuple(tr)),
        grid=(t_tot // tm,),
        in_specs=[pl.BlockSpec((tm, D_MODEL), lambda i: (i, 0)),
                  pl.BlockSpec((1, D_MODEL), lambda i: (0, 0)),
                  pl.BlockSpec(wn.shape, lambda i: (0, 0)),
                  pl.BlockSpec(wt.shape, lambda i: (0, 0)),
                  pl.BlockSpec(bias.shape, lambda i: (0, 0))],
        out_specs=out_specs,
        out_shape=out_shape,
        compiler_params=pltpu.CompilerParams(dimension_semantics=("parallel",), vmem_limit_bytes=VMEM_LIMIT),
        name="norm_proj",
    )(x2d, gain, wn, wt, bias)


def _gelu_tanh(x):
    return 0.5 * x * (1.0 + jnp.tanh(math.sqrt(2.0 / math.pi) * (x + 0.044715 * (x * x * x))))


def _compress_kernel(*refs, npg, n_prefetch):
    refs = refs[n_prefetch:]
    pages = refs[:npg]
    pt_ref, pea_ref, peb_ref, w1a_ref, w1b_ref, w2_ref, out_ref, t_scr, carry = refs[npg:]
    sub = PAGE // CMP_STRIDE
    m = npg * sub

    @pl.when(pl.program_id(1) == 0)
    def _():
        carry[...] = jnp.zeros(carry.shape, F32)

    for k in range(npg):
        xb = pages[k][...].astype(BF16)
        tt = _dot_nt(pt_ref[...], xb)
        for p in range(CMP_STRIDE):
            t_scr[p, k * sub:(k + 1) * sub, :] = tt[p * sub:(p + 1) * sub, :]
    acc_a = jnp.zeros((m, w1a_ref.shape[2]), F32)
    acc_b = jnp.zeros((m, w1a_ref.shape[2]), F32)
    for p in range(CMP_STRIDE):
        tp = t_scr[p]
        acc_a = acc_a + _dot((tp + pea_ref[p]).astype(BF16), w1a_ref[p])
        acc_b = acc_b + _dot((tp + peb_ref[p]).astype(BF16), w1b_ref[p])
    rolled = pltpu.roll(acc_a, 1, axis=0)
    row = lax.broadcasted_iota(I32, acc_a.shape, 0)
    hid = jnp.where(row == 0, carry[...], rolled) + acc_b
    carry[...] = acc_a[m - 1:m, :]
    act = _gelu_tanh(hid)
    out_ref[...] = _dot(act.astype(BF16), w2_ref[...])


def _compress_weights(cmp_pe, cmp_w1, cmp_w2):
    nkg = 2 * NSA_KV
    eye = jnp.eye(nkg, dtype=F32)
    w1 = cmp_w1.reshape(2, CMP_LEN, HEAD_DIM, CMP_HIDDEN)
    w1 = jnp.repeat(w1, NSA_KV, axis=0)
    bd = jnp.einsum("kpdh,kl->pkdlh", w1, eye).reshape(CMP_LEN, nkg * HEAD_DIM, nkg * CMP_HIDDEN).astype(BF16)
    w2 = jnp.repeat(cmp_w2, NSA_KV, axis=0)
    w2bd = jnp.einsum("khd,kl->khld", w2, eye).reshape(nkg * CMP_HIDDEN, nkg * HEAD_DIM).astype(BF16)
    pe = jnp.repeat(cmp_pe, NSA_KV, axis=0)
    pe = pe.transpose(1, 0, 2).reshape(CMP_LEN, 1, nkg * HEAD_DIM)
    return bd[:CMP_STRIDE], bd[CMP_STRIDE:], w2bd, pe[:CMP_STRIDE], pe[CMP_STRIDE:]


def _perm_matrix():
    sub = PAGE // CMP_STRIDE
    pt = np.zeros((PAGE, PAGE), np.float32)
    for p in range(CMP_STRIDE):
        for j in range(sub):
            pt[p * sub + j, CMP_STRIDE * j + p] = 1.0
    return jnp.asarray(pt, BF16)


def _compress(pages_arr, page_table, cw, nb, n_pages, npg, paged):
    w1a, w1b, w2bd, pea, peb = cw
    rows = 2 * NSA_KV * HEAD_DIM
    m = npg * (PAGE // CMP_STRIDE)
    n_chunks = n_pages // npg
    page_specs = []
    for k in range(npg):
        if paged:
            page_specs.append(pl.BlockSpec((None, rows, PAGE), lambda b, c, pt, k=k: (pt[b, c * npg + k], 0, 0)))
        else:
            page_specs.append(pl.BlockSpec((None, rows, PAGE), lambda b, c, k=k: (b, 0, c * npg + k)))

    def const(shape):
        nd = len(shape)
        if paged:
            return pl.BlockSpec(shape, lambda b, c, pt: (0,) * nd)
        return pl.BlockSpec(shape, lambda b, c: (0,) * nd)

    perm = _perm_matrix()
    consts = [perm, pea, peb, w1a, w1b, w2bd]
    in_specs = page_specs + [const(a.shape) for a in consts]
    if paged:
        out_spec = pl.BlockSpec((None, m, rows), lambda b, c, pt: (b, c, 0))
    else:
        out_spec = pl.BlockSpec((None, m, rows), lambda b, c: (b, c, 0))
    n_prefetch = 1 if paged else 0
    grid_spec = pltpu.PrefetchScalarGridSpec(
        num_scalar_prefetch=n_prefetch, grid=(nb, n_chunks), in_specs=in_specs, out_specs=out_spec,
        scratch_shapes=[pltpu.VMEM((CMP_STRIDE, m, rows), F32), pltpu.VMEM((1, w1a.shape[2]), F32)])
    args = ([page_table] if paged else []) + [pages_arr] * npg + consts
    return pl.pallas_call(
        functools.partial(_compress_kernel, npg=npg, n_prefetch=n_prefetch),
        grid_spec=grid_spec,
        out_shape=jax.ShapeDtypeStruct((nb, n_chunks * m, rows), F32),
        compiler_params=pltpu.CompilerParams(dimension_semantics=("parallel", "arbitrary"),
                                             vmem_limit_bytes=VMEM_LIMIT),
        name="nsa_compress",
    )(*args)


def _cmp_to_slc(n_rows, n_c, n_s, n_cols):
    m = np.zeros((n_rows, n_cols), np.float32)
    c = np.arange(n_c)[:, None]
    j = np.arange(n_s)[None, :]
    lo = c * CMP_STRIDE
    hi = lo + CMP_LEN
    m[1:n_c + 1, :n_s] = ((lo < (j + 1) * SLC_BLOCK) & (hi > j * SLC_BLOCK)).astype(np.float32)
    return jnp.asarray(m, BF16)


def _cmp_branch(qg4, cmp_tok, mcs, g, tq, t4, slope4, n_c):
    ck = cmp_tok[:, g * HEAD_DIM:(g + 1) * HEAD_DIM].astype(BF16)
    v0 = (NSA_KV + g) * HEAD_DIM
    cv = cmp_tok[:, v0:v0 + HEAD_DIM].astype(BF16)
    sc = _dot_nt(qg4, ck)
    col = lax.broadcasted_iota(I32, sc.shape, 1)
    dc = t4 - ((col - 1) * CMP_STRIDE + (CMP_LEN - 1))
    valid = (dc >= 0) & (col >= 1) & (col <= n_c)
    p, l, _ = _softmax_parts(sc - slope4 * dc.astype(F32), valid)
    pn = p / l
    o = _dot(pn.astype(BF16), cv)
    psum = pn[0:tq]
    for r in range(1, NSA_R):
        psum = psum + pn[r * tq:(r + 1) * tq]
    return o, _dot01(psum, mcs)


def _slc_select(imp, t1, n_s):
    blk = lax.broadcasted_iota(I32, imp.shape, 1)
    tb = _div(t1, SLC_BLOCK)
    avail = (blk <= tb) & (blk < n_s)
    forced = (blk == 0) | (blk == tb) | (blk == tb - 1)
    v = jnp.where(forced, jnp.inf, jnp.where(avail, imp, -jnp.inf))
    return _topk_mask(v, min(SLC_TOPN, n_s), n_s)


def _nsa_prompt_kernel(qt_ref, glt_ref, cmp_ref, mcst_ref, sk_ref, wk_ref, sv_ref, wv_ref, o_ref,
                       ksaug, kwaug, qaug_scr, selt_scr, ocmp_scr, oslc_scr, m_st, l_st, acc_st,
                       *, tq, s_len, n_c):
    qi = pl.program_id(1)
    s0 = qi * tq
    n_s = s_len // SLC_BLOCK
    kc_len = MOBA_BLOCK
    assert tq == LANES and kc_len == 2 * tq

    @pl.when(qi == 0)
    def _():
        pos = lax.broadcasted_iota(I32, (s_len, HEAD_DIM), 0)
        for g in range(NSA_KV):
            ksaug[g] = _aug_keys(sk_ref[:, g * HEAD_DIM:(g + 1) * HEAD_DIM], pos)
            kwaug[g] = _aug_keys(wk_ref[:, g * HEAD_DIM:(g + 1) * HEAD_DIM], pos)

    for g in range(NSA_KV):
        qaug_scr[g] = jnp.concatenate(
            [_aug_query(qt_ref[h * HEAD_DIM:(h + 1) * HEAD_DIM, :], NSA_SLOPES[h])
             for h in range(g * NSA_R, (g + 1) * NSA_R)], axis=1)
    tl = s0 + lax.broadcasted_iota(I32, (1, tq), 1)
    sig = 1.0 / (1.0 + jnp.exp(-glt_ref[...]))

    def lanes4(x):
        return jnp.concatenate([x] * NSA_R, axis=1)

    cmp_tok = cmp_ref[...]
    ncp = cmp_tok.shape[0]
    cmp_t = cmp_tok.T
    crow = lax.broadcasted_iota(I32, (ncp, tq), 0)
    c_end = (crow - 1) * CMP_STRIDE + (CMP_LEN - 1)
    cbias = lanes4(jnp.where((c_end <= tl) & (crow >= 1) & (crow <= n_c), 0.0, NEG))
    any_c = lanes4(jnp.where(tl >= (CMP_LEN - 1), 1.0, 0.0)) > 0.5
    cpos = jnp.maximum((lax.broadcasted_iota(I32, (ncp, HEAD_DIM), 0) - 1) * CMP_STRIDE + (CMP_LEN - 1), 0)
    blk = lax.broadcasted_iota(I32, (n_s, tq), 0)
    tb = _div(tl, SLC_BLOCK)
    avail = blk <= tb
    forced = (blk == 0) | (blk == tb) | (blk == tb - 1)
    for g in range(NSA_KV):
        ck_aug = _aug_keys(cmp_tok[:, g * HEAD_DIM:(g + 1) * HEAD_DIM].astype(BF16), cpos)
        cvt = cmp_t[(NSA_KV + g) * HEAD_DIM:(NSA_KV + g + 1) * HEAD_DIM, :].astype(BF16)
        p, l, _ = _softmax_cols(_dot(ck_aug, qaug_scr[g]) + cbias)
        pn = jnp.where(any_c, p / l, 0.0)
        ocmp_scr[g] = _dot(cvt, pn.astype(BF16))
        psum = pn[:, 0:tq]
        for r in range(1, NSA_R):
            psum = psum + pn[:, r * tq:(r + 1) * tq]
        imp = _dot01_rhs(mcst_ref[...], psum)[0:n_s]
        v = jnp.where(forced, jnp.inf, jnp.where(avail, imp, -jnp.inf))
        selt_scr[g] = jnp.where(_topk_rows(v, min(SLC_TOPN, n_s), n_s), 1.0, 0.0)

    def reset_state():
        m_st[...] = jnp.full(m_st.shape, NEG, F32)
        l_st[...] = jnp.zeros(l_st.shape, F32)
        acc_st[...] = jnp.zeros(acc_st.shape, F32)

    reset_state()
    cd = _div(s0, kc_len)
    bpc = kc_len // SLC_BLOCK
    ksub = lax.broadcasted_iota(I32, (kc_len, tq), 0)
    diag_bias = jnp.where(cd * kc_len + ksub <= tl, 0.0, NEG)

    def slc_chunk(c, extra):
        for g in range(NSA_KV):
            rows = selt_scr[g, pl.ds(c * bpc, bpc), :]
            sb = jnp.where(rows > 0.5, 0.0, NEG)
            bias = jnp.concatenate([jnp.broadcast_to(sb[i:i + 1, :], (SLC_BLOCK, tq)) for i in range(bpc)], axis=0)
            if extra is not None:
                bias = bias + extra
            kc = ksaug[g, pl.ds(c * kc_len, kc_len), :]
            vtc = sv_ref[c, (NSA_KV + g) * HEAD_DIM:(NSA_KV + g + 1) * HEAD_DIM, :]
            _flash_step(g, kc, vtc, qaug_scr[g], lanes4(bias), m_st, l_st, acc_st)

    def slc_loop(c, carry):
        slc_chunk(c, None)
        return carry

    lax.fori_loop(0, cd, slc_loop, 0)
    slc_chunk(cd, diag_bias)
    for g in range(NSA_KV):
        oslc_scr[g] = acc_st[g] / l_st[g, 0:1, :]

    nwc = WINDOW // LANES + 1
    wsub = lax.broadcasted_iota(I32, (tq, tq), 0)
    wlane = lax.broadcasted_iota(I32, (tq, tq), 1)
    cidx = [qi - (nwc - 1) + j for j in range(nwc)]
    parts = []
    for j in range(nwc):
        if j == 0:
            base = jnp.where(wsub > wlane, 0.0, NEG)
        elif j == nwc - 1:
            base = jnp.where(wsub <= wlane, 0.0, NEG)
        else:
            base = jnp.zeros((tq, tq), F32)
        parts.append(base + jnp.where(cidx[j] >= 0, 0.0, NEG))
    wbias = lanes4(jnp.concatenate(parts, axis=0))
    cclamp = [jnp.maximum(c, 0) for c in cidx]
    o_wins = []
    for g in range(NSA_KV):
        kc = jnp.concatenate([kwaug[g, pl.ds(c * tq, tq), :] for c in cclamp], axis=0)
        vtc = jnp.concatenate([wv_ref[c, (NSA_KV + g) * HEAD_DIM:(NSA_KV + g + 1) * HEAD_DIM, :] for c in cclamp],
                              axis=1)
        p, l, _ = _softmax_cols(_dot(kc, qaug_scr[g]) + wbias)
        o_wins.append(_dot(vtc, p.astype(BF16)) / l)

    outs = []
    for g in range(NSA_KV):
        o_win = o_wins[g]
        gates = [jnp.concatenate([sig[3 * h + k:3 * h + k + 1, :] for h in range(g * NSA_R, (g + 1) * NSA_R)], axis=1)
                 for k in range(3)]
        og = ocmp_scr[g] * gates[0] + oslc_scr[g] * gates[1] + o_win * gates[2]
        outs += [og[:, r * tq:(r + 1) * tq] for r in range(NSA_R)]
    o_ref[...] = jnp.concatenate(outs, axis=0).T.astype(BF16)


def _nsa_prompt(nq_t, gl_t, cmp_tok, sk, wk, sv_c, wv_c, nb, s_len):
    tq = LANES
    nqt = s_len // tq
    n_c = s_len // CMP_STRIDE - CMP_LEN // CMP_STRIDE + 1
    ncp = cmp_tok.shape[1]
    mcst = jnp.transpose(_cmp_to_slc(ncp, n_c, s_len // SLC_BLOCK, LANES))
    rows = 2 * NSA_KV * HEAD_DIM
    width = NSA_HEADS * HEAD_DIM
    kw = NSA_KV * HEAD_DIM
    return pl.pallas_call(
        functools.partial(_nsa_prompt_kernel, tq=tq, s_len=s_len, n_c=n_c),
        grid=(nb, nqt),
        in_specs=[pl.BlockSpec((None, width, tq), lambda b, i: (b, 0, i)),
                  pl.BlockSpec((None, 3 * NSA_HEADS, tq), lambda b, i: (b, 0, i)),
                  pl.BlockSpec((None,) + cmp_tok.shape[1:], lambda b, i: (b, 0, 0)),
                  pl.BlockSpec(mcst.shape, lambda b, i: (0, 0)),
                  pl.BlockSpec((s_len, kw), lambda b, i: (b, 0)),
                  pl.BlockSpec((s_len, kw), lambda b, i: (b, 0)),
                  pl.BlockSpec((None, s_len // MOBA_BLOCK, rows, MOBA_BLOCK), lambda b, i: (b, 0, 0, 0)),
                  pl.BlockSpec((None, s_len // LANES, rows, LANES), lambda b, i: (b, 0, 0, 0))],
        out_specs=pl.BlockSpec((tq, width), lambda b, i: (b * nqt + i, 0)),
        out_shape=jax.ShapeDtypeStruct((nb * s_len, width), BF16),
        scratch_shapes=[pltpu.VMEM((NSA_KV, s_len, LANES), BF16), pltpu.VMEM((NSA_KV, s_len, LANES), BF16),
                        pltpu.VMEM((NSA_KV, LANES, NSA_R * tq), BF16),
                        pltpu.VMEM((NSA_KV, s_len // SLC_BLOCK, tq), F32),
                        pltpu.VMEM((NSA_KV, HEAD_DIM, NSA_R * tq), F32), pltpu.VMEM((NSA_KV, HEAD_DIM, NSA_R * tq), F32),
                        pltpu.VMEM((NSA_KV, 8, NSA_R * tq), F32), pltpu.VMEM((NSA_KV, 8, NSA_R * tq), F32),
                        pltpu.VMEM((NSA_KV, HEAD_DIM, NSA_R * tq), F32)],
        compiler_params=pltpu.CompilerParams(dimension_semantics=("parallel", "arbitrary"),
                                             vmem_limit_bytes=VMEM_LIMIT),
        name="nsa_prompt",
    )(nq_t, gl_t, cmp_tok, mcst, sk, wk, sv_c, wv_c)


def _tri_matrix():
    i = np.arange(LANES)
    return jnp.asarray((i[:, None] <= i[None, :]).astype(np.float32), BF16)


def _fox_prompt_kernel(q_ref, kv_ref, lf_ref, u_ref, o_ref, c_scr, *, tq, s_len, kstep):
    qi = pl.program_id(1)

    @pl.when(qi == 0)
    def _():
        carry = jnp.zeros((FOX_HEADS, 1), F32)
        for blk in range(s_len // LANES):
            cs = _dot01(lf_ref[:, blk * LANES:(blk + 1) * LANES], u_ref[...]) + carry
            c_scr[:, blk * LANES:(blk + 1) * LANES] = cs
            carry = cs[:, LANES - 1:LANES]

    s0 = qi * tq
    t = s0 + lax.broadcasted_iota(I32, (tq, 1), 0)
    nh = FOX_HEADS

    def body(kmax):
        pos = lax.broadcasted_iota(I32, (tq, kmax), 1)
        causal = jnp.where(pos <= t, 0.0, NEG)
        outs = []
        for h in range(nh):
            qh = q_ref[:, h * HEAD_DIM:(h + 1) * HEAD_DIM]
            kt = kv_ref[h * HEAD_DIM:(h + 1) * HEAD_DIM, 0:kmax]
            vt = kv_ref[(nh + h) * HEAD_DIM:(nh + h + 1) * HEAD_DIM, 0:kmax]
            p, l, _ = _softmax_bias(_dot(qh, kt) + (causal - c_scr[h:h + 1, 0:kmax]))
            outs.append(_dot_nt(p.astype(BF16), vt) / l)
        o_ref[...] = jnp.concatenate(outs, axis=1).astype(BF16)

    for c in range(s_len // kstep):
        pl.when(_div(s0, kstep) == c)(functools.partial(body, kstep * (c + 1)))


def _fox_prompt(fq, fkv_b, lf_t, nb, s_len):
    tq = LANES
    nqt = s_len // tq
    rows = 2 * FOX_HEADS * HEAD_DIM
    tri = _tri_matrix()
    return pl.pallas_call(
        functools.partial(_fox_prompt_kernel, tq=tq, s_len=s_len, kstep=KEY_STEP),
        grid=(nb, nqt),
        in_specs=[pl.BlockSpec((tq, FOX_HEADS * HEAD_DIM), lambda b, i: (b * nqt + i, 0)),
                  pl.BlockSpec((None, rows, s_len), lambda b, i: (b, 0, 0)),
                  pl.BlockSpec((None, FOX_HEADS, s_len), lambda b, i: (b, 0, 0)),
                  pl.BlockSpec(tri.shape, lambda b, i: (0, 0))],
        out_specs=pl.BlockSpec((tq, FOX_HEADS * HEAD_DIM), lambda b, i: (b * nqt + i, 0)),
        out_shape=jax.ShapeDtypeStruct((nb * s_len, FOX_HEADS * HEAD_DIM), BF16),
        scratch_shapes=[pltpu.VMEM((FOX_HEADS, s_len), F32)],
        compiler_params=pltpu.CompilerParams(dimension_semantics=("parallel", "arbitrary"),
                                             vmem_limit_bytes=VMEM_LIMIT),
        name="fox_prompt",
    )(fq, fkv_b, lf_t, tri)


def _moba_prompt_kernel(qt_ref, k_ref, kv_ref, eavg_ref, o_ref, kaug, km_scr, qaug_scr, m_scr, l_scr, o_scr,
                        *, tq, s_len):
    qi = pl.program_id(1)
    n_b = s_len // MOBA_BLOCK
    nbp = m_scr.shape[1]
    assert tq == LANES and MOBA_BLOCK == 2 * tq

    @pl.when(qi == 0)
    def _():
        pos = lax.broadcasted_iota(I32, (s_len, HEAD_DIM), 0)
        for g in range(MOBA_KV):
            kaug[g] = _aug_keys(k_ref[:, g * HEAD_DIM:(g + 1) * HEAD_DIM], pos)
        km_scr[...] = _dot(eavg_ref[...], k_ref[...]).astype(BF16)
        m_scr[...] = jnp.zeros(m_scr.shape, F32)
        l_scr[...] = jnp.zeros(l_scr.shape, F32)
        o_scr[...] = jnp.zeros(o_scr.shape, F32)

    s0 = qi * tq
    tb = _div(s0, MOBA_BLOCK)
    tl = s0 + lax.broadcasted_iota(I32, (1, tq), 1)
    for g in range(MOBA_KV):
        qaug_scr[g] = jnp.concatenate(
            [_aug_query(qt_ref[h * HEAD_DIM:(h + 1) * HEAD_DIM, :], MOBA_SLOPES[h])
             for h in range(g * MOBA_R, (g + 1) * MOBA_R)], axis=1)

    def block_partial(j, bias):
        res = []
        for g in range(MOBA_KV):
            kc = kaug[g, pl.ds(j * MOBA_BLOCK, MOBA_BLOCK), :]
            vtc = kv_ref[j, (MOBA_KV + g) * HEAD_DIM:(MOBA_KV + g + 1) * HEAD_DIM, :]
            s = _dot(kc, qaug_scr[g])
            if bias is not None:
                s = s + bias
            p, l, m = _softmax_cols(s)
            res.append((m, l, _dot(vtc, p.astype(BF16))))
        return res

    def past_block(j, carry):
        for g, (m, l, o) in enumerate(block_partial(j, None)):
            m_scr[g, pl.ds(j, 1), :] = m
            l_scr[g, pl.ds(j, 1), :] = l
            o_scr[g, j] = o
        return carry

    lax.fori_loop(0, tb, past_block, 0)
    ksub = lax.broadcasted_iota(I32, (MOBA_BLOCK, tq), 0)
    own_bias = jnp.where(tb * MOBA_BLOCK + ksub <= tl, 0.0, NEG)
    own = block_partial(tb, jnp.concatenate([own_bias] * MOBA_R, axis=1))

    row = lax.broadcasted_iota(I32, (nbp, MOBA_R * tq), 0)
    outs = []
    for g in range(MOBA_KV):
        m_o, l_o, o_o = own[g]
        gate = _dot(km_scr[:, g * HEAD_DIM:(g + 1) * HEAD_DIM], qaug_scr[g, 0:HEAD_DIM, :])
        sel = _topk_rows(jnp.where(row < tb, gate, -jnp.inf), min(MOBA_TOPK, n_b), n_b)
        mm = m_scr[g]
        mx = jnp.maximum(jnp.max(jnp.where(sel, mm, NEG), axis=0, keepdims=True), m_o)
        w = jnp.where(sel, jnp.exp(mm - mx), 0.0)
        w_o = jnp.exp(m_o - mx)
        den = jnp.sum(w * l_scr[g], axis=0, keepdims=True) + w_o * l_o
        num = w_o * o_o
        for j in range(n_b - 1):
            num = num + w[j:j + 1, :] * o_scr[g, j]
        og = num / den
        outs += [og[:, r * tq:(r + 1) * tq] for r in range(MOBA_R)]
    o_ref[...] = jnp.concatenate(outs, axis=0).T.astype(BF16)


def _moba_prompt(mq_t, mk, mkv_c, nb, s_len):
    tq = LANES
    nqt = s_len // tq
    rows = 2 * MOBA_KV * HEAD_DIM
    width = MOBA_HEADS * HEAD_DIM
    kw = MOBA_KV * HEAD_DIM
    n_b = s_len // MOBA_BLOCK
    nbp = 16
    assert n_b <= nbp
    e = np.zeros((nbp, s_len), np.float32)
    e[np.arange(s_len) // MOBA_BLOCK, np.arange(s_len)] = 1.0 / MOBA_BLOCK
    eavg = jnp.asarray(e, BF16)
    return pl.pallas_call(
        functools.partial(_moba_prompt_kernel, tq=tq, s_len=s_len),
        grid=(nb, nqt),
        in_specs=[pl.BlockSpec((None, width, tq), lambda b, i: (b, 0, i)),
                  pl.BlockSpec((s_len, kw), lambda b, i: (b, 0)),
                  pl.BlockSpec((None, n_b, rows, MOBA_BLOCK), lambda b, i: (b, 0, 0, 0)),
                  pl.BlockSpec(eavg.shape, lambda b, i: (0, 0))],
        out_specs=pl.BlockSpec((tq, width), lambda b, i: (b * nqt + i, 0)),
        out_shape=jax.ShapeDtypeStruct((nb * s_len, width), BF16),
        scratch_shapes=[pltpu.VMEM((MOBA_KV, s_len, LANES), BF16),
                        pltpu.VMEM((nbp, kw), BF16),
                        pltpu.VMEM((MOBA_KV, LANES, MOBA_R * tq), BF16),
                        pltpu.VMEM((MOBA_KV, nbp, MOBA_R * tq), F32), pltpu.VMEM((MOBA_KV, nbp, MOBA_R * tq), F32),
                        pltpu.VMEM((MOBA_KV, n_b, HEAD_DIM, MOBA_R * tq), F32)],
        compiler_params=pltpu.CompilerParams(dimension_semantics=("parallel", "arbitrary"),
                                             vmem_limit_bytes=VMEM_LIMIT),
        name="moba_prompt",
    )(mq_t, mk, mkv_c, eavg)


def _post_kernel(*refs, n_o, final):
    x_ref = refs[0]
    o_refs = refs[1:1 + n_o]
    wo_refs = refs[1 + n_o:1 + 2 * n_o]
    g_ref, wup_ref, wdn_ref, gf_ref, out_ref, x1_scr, h_scr, acc_scr = refs[1 + 2 * n_o:]
    j = pl.program_id(1)

    @pl.when(j == 0)
    def _():
        x1 = x_ref[...]
        for o_ref, wo_ref in zip(o_refs, wo_refs):
            x1 = x1 + _dot(o_ref[...], wo_ref[...])
        x1_scr[...] = x1
        ms = jnp.mean(x1 * x1, axis=-1, keepdims=True)
        h_scr[...] = ((x1 * lax.rsqrt(ms + RMS_EPS)) * g_ref[...]).astype(BF16)
        acc_scr[...] = jnp.zeros(acc_scr.shape, F32)

    u = jnp.maximum(_dot(h_scr[...], wup_ref[...]), 0.0)
    acc_scr[...] += _dot((u * u).astype(BF16), wdn_ref[...])

    @pl.when(j == pl.num_programs(1) - 1)
    def _():
        y = x1_scr[...] + acc_scr[...]
        if final:
            ms = jnp.mean(y * y, axis=-1, keepdims=True)
            y = (y * lax.rsqrt(ms + RMS_EPS)) * gf_ref[...]
        out_ref[...] = y


def _post(x2d, o_list, wo_list, g_mlp, w_up, w_down, g_final, final, tm, tf=1024):
    t_tot = x2d.shape[0]
    n_o = len(o_list)
    in_specs = [pl.BlockSpec((tm, D_MODEL), lambda i, j: (i, 0))]
    in_specs += [pl.BlockSpec((tm, o.shape[1]), lambda i, j: (i, 0)) for o in o_list]
    in_specs += [pl.BlockSpec(w.shape, lambda i, j: (0, 0)) for w in wo_list]
    in_specs += [pl.BlockSpec((1, D_MODEL), lambda i, j: (0, 0)),
                 pl.BlockSpec((D_MODEL, tf), lambda i, j: (0, j)),
                 pl.BlockSpec((tf, D_MODEL), lambda i, j: (j, 0)),
                 pl.BlockSpec((1, D_MODEL), lambda i, j: (0, 0))]
    return pl.pallas_call(
        functools.partial(_post_kernel, n_o=n_o, final=final),
        grid=(t_tot // tm, D_FF // tf),
        in_specs=in_specs,
        out_specs=pl.BlockSpec((tm, D_MODEL), lambda i, j: (i, 0)),
        out_shape=jax.ShapeDtypeStruct((t_tot, D_MODEL), F32),
        scratch_shapes=[pltpu.VMEM((tm, D_MODEL), F32), pltpu.VMEM((tm, D_MODEL), BF16),
                        pltpu.VMEM((tm, D_MODEL), F32)],
        compiler_params=pltpu.CompilerParams(dimension_semantics=("parallel", "arbitrary"),
                                             vmem_limit_bytes=VMEM_LIMIT),
        name="post_mlp",
    )(x2d, *o_list, *wo_list, g_mlp, w_up, w_down, g_final)


def _nsa_dec_kernel(q_ref, cmp_ref, mcs_ref, kw_ref, ocmp_ref, owin_ref, sel_ref, *, tq, q0, n_c, n_s):
    q = q_ref[...]
    cmp_tok = cmp_ref[...]
    mcs = mcs_ref[...]
    row4 = lax.broadcasted_iota(I32, (NSA_R * tq, 1), 0)
    t4 = q0 + _mod(row4, tq)
    r4 = _div(row4, tq)
    t1 = q0 + lax.broadcasted_iota(I32, (tq, 1), 0)
    nbp = mcs.shape[1]
    wlen = kw_ref.shape[1]
    for g in range(NSA_KV):
        qg4 = _stack_heads(q, g, NSA_R)
        slope4 = _row_slopes(r4, NSA_SLOPES, g, NSA_R)
        o_cmp, imp = _cmp_branch(qg4, cmp_tok, mcs, g, tq, t4, slope4, n_c)
        sel = _slc_select(imp, t1, n_s)
        sel_ref[:, g * nbp:(g + 1) * nbp] = jnp.where(sel, 1.0, 0.0)
        kwin = kw_ref[g * HEAD_DIM:(g + 1) * HEAD_DIM, :].astype(BF16)
        vwin = kw_ref[(NSA_KV + g) * HEAD_DIM:(NSA_KV + g + 1) * HEAD_DIM, :].astype(BF16)
        wp = (q0 - WINDOW) + lax.broadcasted_iota(I32, (NSA_R * tq, wlen), 1)
        dw = t4 - wp
        valid = (wp >= 0) & (dw >= 0) & (dw < WINDOW)
        p, l, _ = _softmax_parts(_dot(qg4, kwin) - slope4 * dw.astype(F32), valid)
        o_win = _dot_nt(p.astype(BF16), vwin) / l
        for r in range(NSA_R):
            h = g * NSA_R + r
            ocmp_ref[:, h * HEAD_DIM:(h + 1) * HEAD_DIM] = o_cmp[r * tq:(r + 1) * tq]
            owin_ref[:, h * HEAD_DIM:(h + 1) * HEAD_DIM] = o_win[r * tq:(r + 1) * tq]


def _nsa_dec(q8, cmp_tok, kwin_t, nb, q0, n_c, n_s):
    tq = q8.shape[1]
    nbp = -(-n_s // LANES) * LANES
    mcs = _cmp_to_slc(cmp_tok.shape[1], n_c, n_s, nbp)
    width = NSA_HEADS * HEAD_DIM
    return pl.pallas_call(
        functools.partial(_nsa_dec_kernel, tq=tq, q0=q0, n_c=n_c, n_s=n_s),
        grid=(nb,),
        in_specs=[pl.BlockSpec((None, tq, width), lambda b: (b, 0, 0)),
                  pl.BlockSpec((None,) + cmp_tok.shape[1:], lambda b: (b, 0, 0)),
                  pl.BlockSpec(mcs.shape, lambda b: (0, 0)),
                  pl.BlockSpec((None,) + kwin_t.shape[1:], lambda b: (b, 0, 0))],
        out_specs=[pl.BlockSpec((None, tq, width), lambda b: (b, 0, 0)),
                   pl.BlockSpec((None, tq, width), lambda b: (b, 0, 0)),
                   pl.BlockSpec((None, tq, NSA_KV * nbp), lambda b: (b, 0, 0))],
        out_shape=[jax.ShapeDtypeStruct((nb, tq, width), F32),
                   jax.ShapeDtypeStruct((nb, tq, width), F32),
                   jax.ShapeDtypeStruct((nb, tq, NSA_KV * nbp), F32)],
        compiler_params=pltpu.CompilerParams(dimension_semantics=("parallel",), vmem_limit_bytes=VMEM_LIMIT),
        name="nsa_decode_cmp_win",
    )(q8, cmp_tok, mcs, kwin_t)


def _stream_kernel(*refs, mode, nseg, ppseg, n_rows, hk, q0, n_segs):
    pps = nseg * ppseg
    refs = refs[1:]
    qbd_ref, slope_ref, t_ref = refs[0:3]
    k = 3
    sel_ref = selnew_ref = em_ref = None
    if mode == "slc":
        sel_ref, selnew_ref, em_ref = refs[k:k + 3]
        k += 3
    pages = refs[k:k + pps]
    k += pps
    lf_pages = None
    if mode == "fox":
        lf_pages = refs[k:k + pps]
        k += pps
    new_ref = refs[k]
    k += 1
    newlf_ref = tri_ref = None
    if mode == "fox":
        newlf_ref, tri_ref = refs[k], refs[k + 1]
        k += 2
    out_ref = refs[k]
    oparts, m_s, l_s, x_s = refs[k + 1:k + 5]
    st = pl.program_id(1)
    nsteps = pl.num_programs(1)
    qbd = qbd_ref[...]
    slope = slope_ref[...]
    tcol = t_ref[...]
    lane = lax.broadcasted_iota(I32, (n_rows, LANES), 1)
    seg_lane = lax.broadcasted_iota(I32, m_s.shape, 1)
    reps = n_rows // FOX_HEADS

    @pl.when(st == 0)
    def _():
        m_s[...] = jnp.zeros(m_s.shape, F32)
        l_s[...] = jnp.zeros(l_s.shape, F32)
        x_s[...] = jnp.zeros(x_s.shape, F32)

    def scores(page, page_pos0, lf_page, run):
        kt = page[0:hk, :].astype(BF16)
        vt = page[hk:2 * hk, :].astype(BF16)
        s_raw = _dot(qbd, kt)
        if mode == "fox":
            cs = _dot01(lf_page, tri_ref[...]) + run
            run = cs[:, LANES - 1:LANES]
            s = s_raw - jnp.concatenate([cs] * reps, axis=0)
        else:
            s = s_raw - slope * (tcol - (page_pos0 + lane)).astype(F32)
        return s_raw, s, vt, run

    def sel_mask(sref, width):
        mexp = _dot(sref[...].astype(BF16), em_ref[:, 0:width])
        return jnp.where(mexp > 0.5, 0.0, NEG)

    m_all, l_all, x_all = m_s[...], l_s[...], x_s[...]
    mb = sel_mask(sel_ref, pps * PAGE) if mode == "slc" else None
    for sg in range(nseg):
        run = jnp.zeros((FOX_HEADS, 1), F32)
        ss, vts, raws = [], [], []
        for i in range(sg * ppseg, (sg + 1) * ppseg):
            s_raw, s, vt, run = scores(pages[i][...], (st * pps + i) * PAGE,
                                       lf_pages[i][...] if mode == "fox" else None, run)
            if mode == "slc":
                s = s + jnp.concatenate([mb[:, i * PAGE:(i + 1) * PAGE]] * (n_rows // mb.shape[0]), axis=0)
            ss.append(s)
            vts.append(vt)
            raws.append(s_raw)
        smax = ss[0]
        for s in ss[1:]:
            smax = jnp.maximum(smax, s)
        m = smax.max(axis=-1, keepdims=True)
        psum = jnp.zeros((n_rows, LANES), F32)
        o = jnp.zeros((n_rows, hk), F32)
        for s, vt in zip(ss, vts):
            p = jnp.exp(s - m)
            psum = psum + p
            o = o + _dot_nt(p.astype(BF16), vt)
        l = psum.sum(axis=-1, keepdims=True)
        seg = st * nseg + sg
        oparts[seg] = o
        m_all = jnp.where(seg_lane == seg, m, m_all)
        l_all = jnp.where(seg_lane == seg, l, l_all)
        if mode == "fox":
            x_all = jnp.where(seg_lane == seg, jnp.concatenate([run] * reps, axis=0), x_all)
        elif mode == "moba":
            rsum = raws[0]
            for r_ in raws[1:]:
                rsum = rsum + r_
            gsum = rsum.sum(axis=-1, keepdims=True)
            x_all = jnp.where(seg_lane == seg, gsum * (1.0 / MOBA_BLOCK), x_all)
    m_s[...] = m_all
    l_s[...] = l_all
    x_s[...] = x_all

    @pl.when(st == nsteps - 1)
    def _():
        last = n_segs - 1
        _, s, vt, _ = scores(new_ref[...], q0, newlf_ref[...] if mode == "fox" else None,
                             jnp.zeros((FOX_HEADS, 1), F32))
        if mode == "slc":
            mnew = sel_mask(selnew_ref, PAGE)
            s = s + jnp.concatenate([mnew] * (n_rows // mnew.shape[0]), axis=0)
        p, l, m = _softmax_parts(s, ((q0 + lane) <= tcol) & (s > 0.5 * NEG))
        oparts[last] = _dot_nt(p.astype(BF16), vt)
        mm = jnp.where(seg_lane == last, m, m_all)
        ll = jnp.where(seg_lane == last, l, l_all)
        if mode == "fox":
            mm = mm + _dot01(x_all, tri_ref[...], nt=True)
            valid = seg_lane < n_segs
        elif mode == "slc":
            valid = seg_lane < n_segs
        else:
            n_b = n_segs - 1
            gate = jnp.where(seg_lane < n_b, x_all, -jnp.inf)
            valid = _topk_mask(gate, min(MOBA_TOPK, n_b), n_b) | (seg_lane == last)
        mx = jnp.max(jnp.where(valid, mm, NEG), axis=-1, keepdims=True)
        w = jnp.where(valid, jnp.exp(mm - mx), 0.0)
        den = jnp.maximum(jnp.sum(w * ll, axis=-1, keepdims=True), 1e-30)
        num = jnp.zeros((n_rows, hk), F32)
        for seg in range(n_segs):
            num = num + w[:, seg:seg + 1] * oparts[seg]
        out_ref[...] = num / den


def _stream(mode, qbd, slope_col, t_col, sel_steps, pool_t, lf_pool_t, new_t, newlf_t, page_table, q0, nseg, ppseg):
    nb, n_rows, hk = qbd.shape
    n_pages = page_table.shape[1]
    pps = nseg * ppseg
    nsteps = n_pages // pps
    n_segs = nsteps * nseg + 1
    segp = LANES
    assert n_segs <= segp and n_pages % pps == 0

    def cst(shape):
        nd = len(shape)
        return pl.BlockSpec(shape, lambda b, s, pt: (0,) * nd)

    def per_b(shape):
        nd = len(shape)
        return pl.BlockSpec((None,) + shape, lambda b, s, pt: (b,) + (0,) * nd)

    in_specs = [per_b((n_rows, hk)), cst((n_rows, 1)), cst((n_rows, 1))]
    args = [qbd, slope_col, t_col]
    if mode == "slc":
        nsel = sel_steps.shape[2]
        blocks_per_step = pps * (PAGE // SLC_BLOCK)
        assert blocks_per_step <= LANES and sel_steps.shape[1] == nsteps + 1
        em = np.zeros((LANES, pps * PAGE), np.float32)
        em[np.arange(pps * PAGE) // SLC_BLOCK, np.arange(pps * PAGE)] = 1.0
        em = jnp.asarray(em, BF16)
        in_specs += [pl.BlockSpec((None, None, nsel, LANES), lambda b, s, pt: (b, s, 0, 0)),
                     pl.BlockSpec((None, None, nsel, LANES), lambda b, s, pt: (b, nsteps, 0, 0)),
                     cst(em.shape)]
        args += [sel_steps, sel_steps, em]
    for i in range(pps):
        in_specs.append(pl.BlockSpec((None, 2 * hk, PAGE), lambda b, s, pt, i=i: (pt[b, s * pps + i], 0, 0)))
        args.append(pool_t)
    if mode == "fox":
        for i in range(pps):
            in_specs.append(pl.BlockSpec((None, FOX_HEADS, PAGE), lambda b, s, pt, i=i: (pt[b, s * pps + i], 0, 0)))
            args.append(lf_pool_t)
    in_specs.append(per_b((2 * hk, PAGE)))
    args.append(new_t)
    if mode == "fox":
        tri = _tri_matrix()
        in_specs += [per_b((FOX_HEADS, PAGE)), cst(tri.shape)]
        args += [newlf_t, tri]
    grid_spec = pltpu.PrefetchScalarGridSpec(
        num_scalar_prefetch=1, grid=(nb, nsteps), in_specs=in_specs,
        out_specs=pl.BlockSpec((None, n_rows, hk), lambda b, s, pt: (b, 0, 0)),
        scratch_shapes=[pltpu.VMEM((n_segs, n_rows, hk), F32), pltpu.VMEM((n_rows, segp), F32),
                        pltpu.VMEM((n_rows, segp), F32), pltpu.VMEM((n_rows, segp), F32)])
    return pl.pallas_call(
        functools.partial(_stream_kernel, mode=mode, nseg=nseg, ppseg=ppseg, n_rows=n_rows, hk=hk, q0=q0,
                          n_segs=n_segs),
        grid_spec=grid_spec,
        out_shape=jax.ShapeDtypeStruct((nb, n_rows, hk), F32),
        compiler_params=pltpu.CompilerParams(dimension_semantics=("parallel", "arbitrary"),
                                             vmem_limit_bytes=VMEM_LIMIT),
        name="decode_stream_" + mode,
    )(page_table, *args)


def _nsa_gate_kernel(oc_ref, os_ref, ow_ref, gl_ref, o_ref):
    sig = 1.0 / (1.0 + jnp.exp(-gl_ref[...]))
    for h in range(NSA_HEADS):
        sl = slice(h * HEAD_DIM, (h + 1) * HEAD_DIM)
        o_ref[:, sl] = (oc_ref[:, sl] * sig[:, 3 * h:3 * h + 1] + os_ref[:, sl] * sig[:, 3 * h + 1:3 * h + 2]
                        + ow_ref[:, sl] * sig[:, 3 * h + 2:3 * h + 3]).astype(BF16)


def _nsa_gate(oc, os_, ow, gl):
    return pl.pallas_call(
        _nsa_gate_kernel,
        out_shape=jax.ShapeDtypeStruct(oc.shape, BF16),
        name="nsa_gate",
    )(oc, os_, ow, gl)


def _pool_t(cache, li):
    c = jnp.transpose(cache[li], (0, 2, 3, 4, 1))
    return c.reshape(c.shape[0], -1, c.shape[-1])


def _block_diag_q(q, n_tok, n_kv, n_r, r_major=False):
    b = q.shape[0]
    q5 = q.reshape(b, n_tok, n_kv, n_r, 1, HEAD_DIM)
    eye = jnp.eye(n_kv, dtype=q.dtype).reshape(1, 1, n_kv, 1, n_kv, 1)
    x = (q5 * eye).reshape(b, n_tok, n_kv, n_r, n_kv * HEAD_DIM)
    if r_major:
        x = x.transpose(0, 3, 1, 2, 4)
    return x.reshape(b, n_tok * n_kv * n_r, n_kv * HEAD_DIM)


def _diag_heads(o, n_tok, n_kv, n_r, r_major=False):
    b = o.shape[0]
    if r_major:
        o6 = o.reshape(b, n_r, n_tok, n_kv, n_kv, HEAD_DIM).transpose(0, 2, 3, 1, 4, 5)
    else:
        o6 = o.reshape(b, n_tok, n_kv, n_r, n_kv, HEAD_DIM)
    d = jnp.einsum("btgrgd->btgrd", o6)
    return d.reshape(b * n_tok, n_kv * n_r * HEAD_DIM)


def _row_consts(n_tok, n_kv, n_r, slopes, q0, r_major=False):
    t, g, r = np.meshgrid(np.arange(n_tok), np.arange(n_kv), np.arange(n_r), indexing="ij")
    h = g * n_r + r
    if r_major:
        t, h = t.transpose(2, 0, 1), h.transpose(2, 0, 1)
    t, h = t.reshape(-1), h.reshape(-1)
    sl = np.asarray(slopes, np.float32)[h] if slopes is not None else np.zeros(h.shape, np.float32)
    return jnp.asarray(sl.reshape(-1, 1), F32), jnp.asarray((q0 + t).reshape(-1, 1), I32)


def _new_pages(kv_t, nb, n_tok):
    rows = kv_t.shape[0]
    x = kv_t.reshape(rows, nb, n_tok).transpose(1, 0, 2)
    return jnp.pad(x, ((0, 0), (0, 0), (0, PAGE - n_tok)))


def kernel(x_prompt, x_sample, cache_nsa_cmp_kv, cache_nsa_slc_kv, state_nsa_win_kv, cache_fox_kv, cache_fox_lf,
           cache_moba_kv, page_table, norm_mix, norm_mlp, w_in_even, b_fgt, w_out_even, cmp_pe, cmp_w1, cmp_w2,
           w_in_odd, w_out_odd, w_up, w_down, norm_final):
    nb_p, s_len, _ = x_prompt.shape
    nb_d, n_tok, _ = x_sample.shape
    n_pages = page_table.shape[1]
    past = n_pages * PAGE
    nq_w = NSA_HEADS * HEAD_DIM
    nkv_w = 2 * NSA_KV * HEAD_DIM
    fq_w = FOX_HEADS * HEAD_DIM
    fkv_w = 2 * FOX_HEADS * HEAD_DIM
    mq_w = MOBA_HEADS * HEAD_DIM
    mkv_w = 2 * MOBA_KV * HEAD_DIM
    ngl = 3 * NSA_HEADS

    wte = jnp.transpose(w_in_even[0])
    o_nq, o_ckv, o_skv, o_wkv = 0, nq_w, nq_w + nkv_w, nq_w + 2 * nkv_w
    o_gl = nq_w + 3 * nkv_w
    o_fq = o_gl + ngl
    o_fkv = o_fq + fq_w
    o_fl = o_fkv + fkv_w
    wn_e = jnp.concatenate([wte[o_nq:o_nq + nq_w], wte[o_fq:o_fq + fq_w], wte[o_gl:o_gl + ngl],
                            jnp.zeros((LANES - ngl, D_MODEL), F32)], axis=0)
    wn_e = jnp.transpose(wn_e).astype(BF16)
    wt_e = jnp.concatenate([wte[o_ckv:o_ckv + 3 * nkv_w], wte[o_fkv:o_fkv + fkv_w], wte[o_fl:o_fl + FOX_HEADS]],
                           axis=0).astype(BF16)
    bias_e = b_fgt[0].reshape(FOX_HEADS, 1)
    nat_e = [(0, nq_w, "q"), (nq_w, fq_w, "q"), (nq_w + fq_w, LANES, "f32")]
    r_skv, r_wkv, r_fkv, r_fl = nkv_w, 2 * nkv_w, 3 * nkv_w, 3 * nkv_w + fkv_w
    tr_e_dec = [(0, nkv_w, "f32"), (r_skv, nkv_w, "f32"), (r_wkv, nkv_w, "f32"), (r_fkv, fkv_w, "f32"),
                (r_fl, FOX_HEADS, "lf")]
    wo_t = jnp.transpose(w_in_odd[0])
    wn_o = w_in_odd[0][:, :mq_w].astype(BF16)
    wt_o = wo_t[mq_w:].astype(BF16)
    bias_o = jnp.zeros((8, 1), F32)
    nat_o = [(0, mq_w, "q")]
    tr_o_dec = [(0, mkv_w, "f32")]
    hk_n = NSA_KV * HEAD_DIM
    wn_ep = jnp.transpose(jnp.concatenate([wte[o_fq:o_fq + fq_w], wte[o_skv:o_skv + hk_n],
                                           wte[o_wkv:o_wkv + hk_n]], axis=0)).astype(BF16)
    wt_ep = jnp.concatenate([wte[o_ckv:o_ckv + 3 * nkv_w], wte[o_fkv:o_fkv + fkv_w], wte[o_fl:o_fl + FOX_HEADS],
                             wte[o_nq:o_nq + nq_w], wte[o_gl:o_gl + ngl]], axis=0).astype(BF16)
    nat_ep = [(0, fq_w, "q"), (fq_w, hk_n, "kb"), (fq_w + hk_n, hk_n, "kb")]
    r_nq = r_fl + FOX_HEADS
    r_gl = r_nq + nq_w
    tr_e_prompt = [(0, nkv_w, "f32"), (r_skv, nkv_w, "f32"), (r_skv, nkv_w, "bf16c2"), (r_wkv, nkv_w, "f32"),
                   (r_wkv, nkv_w, "bf16c"), (r_fkv, fkv_w, "f32"), (r_fkv, fkv_w, "bf16"), (r_fl, FOX_HEADS, "lf"),
                   (r_nq, nq_w, "qt"), (r_gl, ngl, "f32")]
    hk_m = MOBA_KV * HEAD_DIM
    wn_op = w_in_odd[0][:, mq_w:mq_w + hk_m].astype(BF16)
    wt_op = jnp.concatenate([wo_t[mq_w:], wo_t[:mq_w]], axis=0).astype(BF16)
    nat_op = [(0, hk_m, "kb")]
    tr_o_prompt = [(0, mkv_w, "f32"), (0, mkv_w, "bf16c2"), (mkv_w, mq_w, "qt")]
    woe = w_out_even[0].astype(BF16)
    woe_a, woe_b = woe[:nq_w], woe[nq_w:]
    woo = w_out_odd[0].astype(BF16)
    wup = w_up.astype(BF16)
    wdn = w_down.astype(BF16)
    g_mix = norm_mix.reshape(norm_mix.shape[0], 1, D_MODEL)
    g_mlp = norm_mlp.reshape(norm_mlp.shape[0], 1, D_MODEL)
    g_fin = norm_final.reshape(1, D_MODEL)
    cw = _compress_weights(cmp_pe[0], cmp_w1[0], cmp_w2[0])

    xp = x_prompt.reshape(nb_p * s_len, D_MODEL)
    tm_p = 512
    (fq, sk_n, wk_n, ckv_t, skv_t, skv_c, wkv_t, wkv_c, fkv_t, fkv_b, lf_t, nq_t, gl_t) = _proj(
        xp, g_mix[0], wn_ep, wt_ep, bias_e, nat_ep, tr_e_prompt, nb_p, s_len, tm_p)
    cmp_tok = _compress(ckv_t, None, cw, nb_p, s_len // PAGE, s_len // PAGE, paged=False)
    o_nsa = _nsa_prompt(nq_t, gl_t, cmp_tok, sk_n, wk_n, skv_c, wkv_c, nb_p, s_len)
    o_fox = _fox_prompt(fq, fkv_b, lf_t, nb_p, s_len)
    tm_post = tm_p
    xp = _post(xp, [o_nsa, o_fox], [woe_a, woe_b], g_mlp[0], wup[0], wdn[0], g_fin, False, tm_post)
    mk_n, mkv_t, mkv_c, mq_t = _proj(xp, g_mix[1], wn_op, wt_op, bias_o, nat_op, tr_o_prompt, nb_p, s_len, tm_p)
    o_moba = _moba_prompt(mq_t, mk_n, mkv_c, nb_p, s_len)
    yp = _post(xp, [o_moba], [woo], g_mlp[1], wup[1], wdn[1], g_fin, True, tm_post)
    y_prompt = yp.reshape(nb_p, s_len, D_MODEL)

    def kv_out(t, n_h):
        b, _, s = t.shape
        return jnp.transpose(t.reshape(b, 2, n_h, HEAD_DIM, s), (0, 4, 1, 2, 3))[None]

    p_cmp = kv_out(ckv_t, NSA_KV)
    p_slc = kv_out(skv_t, NSA_KV)
    wb = min(WINDOW, s_len)
    p_win = kv_out(wkv_t[:, :, s_len - wb:], NSA_KV)
    p_fox = kv_out(fkv_t, FOX_HEADS)
    p_lf = jnp.transpose(lf_t, (0, 2, 1))[None]
    p_moba = kv_out(mkv_t, MOBA_KV)

    td = nb_d * n_tok
    xd = x_sample.reshape(td, D_MODEL)
    (nq_d, fq_d, gl_d, ckv_d, skv_d, wkv_d, fkv_d, lf_d) = _proj(
        xd, g_mix[0], wn_e, wt_e, bias_e, nat_e, tr_e_dec, 1, td, td)

    def kv_out_dec(t, n_h):
        return jnp.transpose(t[0]).reshape(1, nb_d, n_tok, 2, n_h, HEAD_DIM)

    s_cmp = kv_out_dec(ckv_d, NSA_KV)
    s_slc = kv_out_dec(skv_d, NSA_KV)
    s_fox = kv_out_dec(fkv_d, FOX_HEADS)
    s_lf = jnp.transpose(lf_d[0]).reshape(1, nb_d, n_tok, FOX_HEADS)
    win_state_t = jnp.transpose(state_nsa_win_kv[0], (0, 2, 3, 4, 1)).reshape(nb_d, nkv_w, -1)
    wkv_new = wkv_d[0].reshape(nkv_w, nb_d, n_tok).transpose(1, 0, 2)
    win_all = jnp.concatenate([win_state_t, wkv_new], axis=2)
    wbuf = win_state_t.shape[2]
    s_win = jnp.transpose(win_all[:, :, -wbuf:].reshape(nb_d, 2, NSA_KV, HEAD_DIM, wbuf), (0, 4, 1, 2, 3))[None]
    assert wbuf == WINDOW
    wpad = -(-(wbuf + n_tok) // LANES) * LANES
    kwin_t = jnp.pad(win_all, ((0, 0), (0, 0), (0, wpad - wbuf - n_tok)))

    assert (past + n_tok) // CMP_STRIDE == past // CMP_STRIDE
    cmp_pool = _pool_t(cache_nsa_cmp_kv, 0)
    cmp_tok_d = _compress(cmp_pool, page_table, cw, nb_d, n_pages, 32, paged=True)
    n_c = past // CMP_STRIDE - CMP_LEN // CMP_STRIDE + 1
    n_s = -(-(past + n_tok) // SLC_BLOCK)
    tq_d = 8
    q8 = jnp.pad(nq_d.reshape(nb_d, n_tok, nq_w), ((0, 0), (0, tq_d - n_tok), (0, 0)))
    o_cmp8, o_win8, sel8 = _nsa_dec(q8, cmp_tok_d, kwin_t, nb_d, past, n_c, n_s)
    o_cmp_d = o_cmp8[:, :n_tok].reshape(td, nq_w)
    o_win_d = o_win8[:, :n_tok].reshape(td, nq_w)
    nbp = sel8.shape[2] // NSA_KV
    pps_d = 16
    nsteps_d = n_pages // pps_d
    bps = pps_d * (PAGE // SLC_BLOCK)
    sel_tg = sel8[:, :n_tok].reshape(nb_d, n_tok * NSA_KV, nbp)[:, :, :n_s]
    sel_tg = jnp.pad(sel_tg, ((0, 0), (0, 0), (0, (nsteps_d + 1) * bps - n_s)))
    sel_steps = sel_tg.reshape(nb_d, n_tok * NSA_KV, nsteps_d + 1, bps).transpose(0, 2, 1, 3)
    sel_steps = jnp.pad(sel_steps, ((0, 0), (0, 0), (0, 0), (0, LANES - bps)))
    sl_nsa, t_nsa = _row_consts(n_tok, NSA_KV, NSA_R, NSA_SLOPES, past, r_major=True)
    qbd_s = _block_diag_q(nq_d.reshape(nb_d, n_tok, nq_w), n_tok, NSA_KV, NSA_R, r_major=True)
    new_s = _new_pages(skv_d[0], nb_d, n_tok)
    o_slc_bd = _stream("slc", qbd_s, sl_nsa, t_nsa, sel_steps, _pool_t(cache_nsa_slc_kv, 0), None, new_s, None,
                       page_table, past, 1, pps_d)
    o_slc_d = _diag_heads(o_slc_bd, n_tok, NSA_KV, NSA_R, r_major=True)
    o_nsa_d = _nsa_gate(o_cmp_d, o_slc_d, o_win_d, gl_d)
    sl_fox, t_fox = _row_consts(n_tok, FOX_HEADS, 1, None, past)
    qbd_f = _block_diag_q(fq_d.reshape(nb_d, n_tok, fq_w), n_tok, FOX_HEADS, 1)
    new_f = _new_pages(fkv_d[0], nb_d, n_tok)
    newlf = _new_pages(lf_d[0], nb_d, n_tok)
    lf_pool = jnp.transpose(cache_fox_lf[0], (0, 2, 1))
    o_fox_bd = _stream("fox", qbd_f, sl_fox, t_fox, None, _pool_t(cache_fox_kv, 0), lf_pool, new_f, newlf,
                       page_table, past, 1, pps_d)
    o_fox_d = _diag_heads(o_fox_bd, n_tok, FOX_HEADS, 1).astype(BF16)
    xd = _post(xd, [o_nsa_d, o_fox_d], [woe_a, woe_b], g_mlp[0], wup[0], wdn[0], g_fin, False, td)
    mq_d, mkv_d = _proj(xd, g_mix[1], wn_o, wt_o, bias_o, nat_o, tr_o_dec, 1, td, td)
    s_moba = kv_out_dec(mkv_d, MOBA_KV)
    assert past % MOBA_BLOCK == 0 and MOBA_BLOCK == 2 * PAGE
    sl_m, t_m = _row_consts(n_tok, MOBA_KV, MOBA_R, MOBA_SLOPES, past)
    qbd_m = _block_diag_q(mq_d.reshape(nb_d, n_tok, mq_w), n_tok, MOBA_KV, MOBA_R)
    new_m = _new_pages(mkv_d[0], nb_d, n_tok)
    o_moba_bd = _stream("moba", qbd_m, sl_m, t_m, None, _pool_t(cache_moba_kv, 0), None, new_m, None,
                        page_table, past, pps_d // (MOBA_BLOCK // PAGE), MOBA_BLOCK // PAGE)
    o_moba_d = _diag_heads(o_moba_bd, n_tok, MOBA_KV, MOBA_R).astype(BF16)
    yd = _post(xd, [o_moba_d], [woo], g_mlp[1], wup[1], wdn[1], g_fin, True, td)
    y_sample = yd.reshape(nb_d, n_tok, D_MODEL)

    return (y_prompt, y_sample, p_cmp, s_cmp, p_slc, s_slc, p_win, s_win, p_fox, s_fox, p_lf, s_lf, p_moba, s_moba)
```

```python
import functools
import math

import numpy as np
import jax
import jax.numpy as jnp
from jax import lax
from jax.experimental import pallas as pl
from jax.experimental.pallas import tpu as pltpu

F32 = jnp.float32
BF16 = jnp.bfloat16
I32 = jnp.int32

D_MODEL = 1024
HEAD_DIM = 64
NSA_HEADS = 8
NSA_KV = 2
NSA_R = NSA_HEADS // NSA_KV
CMP_LEN = 32
CMP_STRIDE = 16
CMP_HIDDEN = 128
SLC_BLOCK = 64
SLC_TOPN = 16
WINDOW = 512
FOX_HEADS = 8
MOBA_HEADS = 16
MOBA_KV = 4
MOBA_R = MOBA_HEADS // MOBA_KV
MOBA_BLOCK = 256
MOBA_TOPK = 3
D_FF = 4 * D_MODEL
PAGE = 128
RMS_EPS = 1e-6
NEG = -1e30
SCALE = HEAD_DIM ** -0.5
LANES = 128
KEY_STEP = 512
VMEM_LIMIT = 56 * 1024 * 1024

NT_DIMS = (((1,), (1,)), ((), ()))


def _alibi(n):
    return [float(np.float32(2.0 ** (-8.0 * (i + 1) / n))) for i in range(n)]


NSA_SLOPES = _alibi(NSA_HEADS)
MOBA_SLOPES = _alibi(MOBA_HEADS)


def _div(x, n):
    assert n & (n - 1) == 0
    return jnp.right_shift(x, int(math.log2(n)))


def _mod(x, n):
    assert n & (n - 1) == 0
    return jnp.bitwise_and(x, n - 1)


def _dot(a, b):
    return jnp.dot(a, b, preferred_element_type=F32)


def _dot_nt(a, b):
    return lax.dot_general(a, b, NT_DIMS, preferred_element_type=F32)


def _split3(x):
    hi = x.astype(BF16)
    r1 = x - hi.astype(F32)
    mid = r1.astype(BF16)
    lo = (r1 - mid.astype(F32)).astype(BF16)
    return hi, mid, lo


def _dot01(x, mat, nt=False):
    n = x.shape[0]
    st = jnp.concatenate(_split3(x), axis=0)
    y = _dot_nt(st, mat) if nt else _dot(st, mat)
    return y[0:n] + y[n:2 * n] + y[2 * n:3 * n]


def _softmax_parts(s, valid):
    s = jnp.where(valid, s, NEG)
    m = jnp.max(s, axis=-1, keepdims=True)
    p = jnp.where(valid, jnp.exp(s - m), 0.0)
    l = jnp.maximum(jnp.sum(p, axis=-1, keepdims=True), 1e-30)
    return p, l, m


def _softmax_bias(logits):
    m = jnp.max(logits, axis=-1, keepdims=True)
    p = jnp.exp(logits - m)
    return p, jnp.sum(p, axis=-1, keepdims=True), m


def _topk_mask(v, nsel, nb):
    lane = lax.broadcasted_iota(I32, v.shape, 1)
    cnt = jnp.zeros(v.shape, I32)
    for k in range(nb):
        col = v[:, k:k + 1]
        beats = (col > v) | ((col == v) & (lane > k))
        cnt = cnt + jnp.where(beats, 1, 0)
    return (cnt < nsel) & (v > -jnp.inf)


def _dot01_rhs(mat, x):
    n = x.shape[1]
    y = _dot(mat, jnp.concatenate(_split3(x), axis=1))
    return y[:, 0:n] + y[:, n:2 * n] + y[:, 2 * n:3 * n]


def _bf16_parts(x):
    def rnd(v):
        u = np.float32(v).view(np.uint32)
        u = np.uint32((int(u) + 0x7FFF + ((int(u) >> 16) & 1)) & 0xFFFF0000)
        return float(u.view(np.float32))
    hi = rnd(x)
    mid = rnd(np.float32(np.float32(x) - np.float32(hi)))
    lo = rnd(np.float32(np.float32(x) - np.float32(hi) - np.float32(mid)))
    return hi, mid, lo


N_AUG = 6


def _aug_keys(k, pos):
    lane = lax.broadcasted_iota(I32, k.shape, 1)
    a = _div(pos, LANES).astype(F32)
    b = _mod(pos, LANES).astype(F32)
    aug = jnp.where(lane < 3, a, jnp.where(lane < N_AUG, b, 0.0)).astype(BF16)
    return jnp.concatenate([k, aug], axis=1)


def _aug_query(qt, slope):
    parts = _bf16_parts(slope)
    vals = [float(LANES) * p for p in parts] + list(parts)
    sub = lax.broadcasted_iota(I32, qt.shape, 0)
    aug = jnp.zeros(qt.shape, F32)
    for i, v in enumerate(vals):
        aug = jnp.where(sub == i, v, aug)
    return jnp.concatenate([qt, aug.astype(BF16)], axis=0)


def _softmax_cols(logits):
    m = jnp.max(logits, axis=0, keepdims=True)
    p = jnp.exp(logits - m)
    return p, jnp.sum(p, axis=0, keepdims=True), m


def _topk_rows(v, nsel, nb):
    row = lax.broadcasted_iota(I32, v.shape, 0)
    cnt = jnp.zeros(v.shape, I32)
    for k in range(nb):
        rk = v[k:k + 1, :]
        beats = (rk > v) | ((rk == v) & (row > k))
        cnt = cnt + jnp.where(beats, 1, 0)
    return (cnt < nsel) & (v > -jnp.inf)


def _stack_heads(q, g, nr):
    return jnp.concatenate([q[:, (g * nr + r) * HEAD_DIM:(g * nr + r + 1) * HEAD_DIM] for r in range(nr)], axis=0)


def _row_slopes(r4, slopes, g, nr):
    out = jnp.zeros(r4.shape, F32)
    for r in range(nr):
        out = jnp.where(r4 == r, slopes[g * nr + r], out)
    return out


_CHUNKED = {"bf16c": LANES, "bf16c2": MOBA_BLOCK}


def _proj_kernel(x_ref, g_ref, wn_ref, wt_ref, b_ref, *outs, nat, tr):
    x = x_ref[...]
    ms = jnp.mean(x * x, axis=-1, keepdims=True)
    h = (x * lax.rsqrt(ms + RMS_EPS)) * g_ref[...]
    hb = h.astype(BF16)
    yn = _dot(hb, wn_ref[...])
    yt = _dot_nt(wt_ref[...], hb)
    k = 0
    for (c0, w, kind) in nat:
        v = yn[:, c0:c0 + w]
        if kind == "q":
            outs[k][...] = (v * SCALE).astype(BF16)
        elif kind == "kb":
            outs[k][...] = v.astype(BF16)
        else:
            outs[k][...] = v
        k += 1
    for (r0, n, kind) in tr:
        v = yt[r0:r0 + n, :]
        if kind == "f32":
            outs[k][...] = v
        elif kind == "bf16":
            outs[k][...] = v.astype(BF16)
        elif kind == "qt":
            outs[k][...] = (v * SCALE).astype(BF16)
        elif kind in _CHUNKED:
            cw = _CHUNKED[kind]
            vb = v.astype(BF16)
            for c in range(v.shape[1] // cw):
                outs[k][c] = vb[:, c * cw:(c + 1) * cw]
        else:
            z = v + b_ref[...]
            outs[k][...] = jnp.minimum(z, 0.0) - jnp.log(1.0 + jnp.exp(-jnp.abs(z)))
        k += 1


def _proj(x2d, gain, wn, wt, bias, nat, tr, nb, s_len, tm):
    t_tot = x2d.shape[0]
    tpb = s_len // tm
    assert nb * s_len == t_tot
    out_shape, out_specs = [], []
    for (c0, w, kind) in nat:
        out_shape.append(jax.ShapeDtypeStruct((t_tot, w), BF16 if kind in ("q", "kb") else F32))
        out_specs.append(pl.BlockSpec((tm, w), lambda i: (i, 0)))
    for (r0, n, kind) in tr:
        if kind in _CHUNKED:
            cw = _CHUNKED[kind]
            out_shape.append(jax.ShapeDtypeStruct((nb, s_len // cw, n, cw), BF16))
            out_specs.append(pl.BlockSpec((None, tm // cw, n, cw), lambda i: (i // tpb, i % tpb, 0, 0)))
        else:
            out_shape.append(jax.ShapeDtypeStruct((nb, n, s_len), BF16 if kind in ("bf16", "qt") else F32))
            out_specs.append(pl.BlockSpec((None, n, tm), lambda i: (i // tpb, 0, i % tpb)))
    return pl.pallas_call(
        functools.partial(_proj_kernel, nat=tuple(nat), tr=tuple(tr)),
        grid=(t_tot // tm,),
        in_specs=[pl.BlockSpec((tm, D_MODEL), lambda i: (i, 0)),
                  pl.BlockSpec((1, D_MODEL), lambda i: (0, 0)),
                  pl.BlockSpec(wn.shape, lambda i: (0, 0)),
                  pl.BlockSpec(wt.shape, lambda i: (0, 0)),
                  pl.BlockSpec(bias.shape, lambda i: (0, 0))],
        out_specs=out_specs,
        out_shape=out_shape,
        compiler_params=pltpu.CompilerParams(dimension_semantics=("parallel",), vmem_limit_bytes=VMEM_LIMIT),
        name="norm_proj",
    )(x2d, gain, wn, wt, bias)


def _gelu_tanh(x):
    return 0.5 * x * (1.0 + jnp.tanh(math.sqrt(2.0 / math.pi) * (x + 0.044715 * (x * x * x))))


def _compress_kernel(*refs, npg, n_prefetch):
    refs = refs[n_prefetch:]
    pages = refs[:npg]
    pt_ref, pea_ref, peb_ref, w1a_ref, w1b_ref, w2_ref, out_ref, t_scr, carry = refs[npg:]
    sub = PAGE // CMP_STRIDE
    m = npg * sub

    @pl.when(pl.program_id(1) == 0)
    def _():
        carry[...] = jnp.zeros(carry.shape, F32)

    for k in range(npg):
        xb = pages[k][...].astype(BF16)
        tt = _dot_nt(pt_ref[...], xb)
        for p in range(CMP_STRIDE):
            t_scr[p, k * sub:(k + 1) * sub, :] = tt[p * sub:(p + 1) * sub, :]
    acc_a = jnp.zeros((m, w1a_ref.shape[2]), F32)
    acc_b = jnp.zeros((m, w1a_ref.shape[2]), F32)
    for p in range(CMP_STRIDE):
        tp = t_scr[p]
        acc_a = acc_a + _dot((tp + pea_ref[p]).astype(BF16), w1a_ref[p])
        acc_b = acc_b + _dot((tp + peb_ref[p]).astype(BF16), w1b_ref[p])
    rolled = pltpu.roll(acc_a, 1, axis=0)
    row = lax.broadcasted_iota(I32, acc_a.shape, 0)
    hid = jnp.where(row == 0, carry[...], rolled) + acc_b
    carry[...] = acc_a[m - 1:m, :]
    act = _gelu_tanh(hid)
    out_ref[...] = _dot(act.astype(BF16), w2_ref[...])


def _compress_weights(cmp_pe, cmp_w1, cmp_w2):
    nkg = 2 * NSA_KV
    eye = jnp.eye(nkg, dtype=F32)
    w1 = cmp_w1.reshape(2, CMP_LEN, HEAD_DIM, CMP_HIDDEN)
    w1 = jnp.repeat(w1, NSA_KV, axis=0)
    bd = jnp.einsum("kpdh,kl->pkdlh", w1, eye).reshape(CMP_LEN, nkg * HEAD_DIM, nkg * CMP_HIDDEN).astype(BF16)
    w2 = jnp.repeat(cmp_w2, NSA_KV, axis=0)
    w2bd = jnp.einsum("khd,kl->khld", w2, eye).reshape(nkg * CMP_HIDDEN, nkg * HEAD_DIM).astype(BF16)
    pe = jnp.repeat(cmp_pe, NSA_KV, axis=0)
    pe = pe.transpose(1, 0, 2).reshape(CMP_LEN, 1, nkg * HEAD_DIM)
    return bd[:CMP_STRIDE], bd[CMP_STRIDE:], w2bd, pe[:CMP_STRIDE], pe[CMP_STRIDE:]


def _perm_matrix():
    sub = PAGE // CMP_STRIDE
    pt = np.zeros((PAGE, PAGE), np.float32)
    for p in range(CMP_STRIDE):
        for j in range(sub):
            pt[p * sub + j, CMP_STRIDE * j + p] = 1.0
    return jnp.asarray(pt, BF16)


def _compress(pages_arr, page_table, cw, nb, n_pages, npg, paged):
    w1a, w1b, w2bd, pea, peb = cw
    rows = 2 * NSA_KV * HEAD_DIM
    m = npg * (PAGE // CMP_STRIDE)
    n_chunks = n_pages // npg
    page_specs = []
    for k in range(npg):
        if paged:
            page_specs.append(pl.BlockSpec((None, rows, PAGE), lambda b, c, pt, k=k: (pt[b, c * npg + k], 0, 0)))
        else:
            page_specs.append(pl.BlockSpec((None, rows, PAGE), lambda b, c, k=k: (b, 0, c * npg + k)))

    def const(shape):
        nd = len(shape)
        if paged:
            return pl.BlockSpec(shape, lambda b, c, pt: (0,) * nd)
        return pl.BlockSpec(shape, lambda b, c: (0,) * nd)

    perm = _perm_matrix()
    consts = [perm, pea, peb, w1a, w1b, w2bd]
    in_specs = page_specs + [const(a.shape) for a in consts]
    if paged:
        out_spec = pl.BlockSpec((None, m, rows), lambda b, c, pt: (b, c, 0))
    else:
        out_spec = pl.BlockSpec((None, m, rows), lambda b, c: (b, c, 0))
    n_prefetch = 1 if paged else 0
    grid_spec = pltpu.PrefetchScalarGridSpec(
        num_scalar_prefetch=n_prefetch, grid=(nb, n_chunks), in_specs=in_specs, out_specs=out_spec,
        scratch_shapes=[pltpu.VMEM((CMP_STRIDE, m, rows), F32), pltpu.VMEM((1, w1a.shape[2]), F32)])
    args = ([page_table] if paged else []) + [pages_arr] * npg + consts
    return pl.pallas_call(
        functools.partial(_compress_kernel, npg=npg, n_prefetch=n_prefetch),
        grid_spec=grid_spec,
        out_shape=jax.ShapeDtypeStruct((nb, n_chunks * m, rows), F32),
        compiler_params=pltpu.CompilerParams(dimension_semantics=("parallel", "arbitrary"),
                                             vmem_limit_bytes=VMEM_LIMIT),
        name="nsa_compress",
    )(*args)


def _cmp_to_slc(n_rows, n_c, n_s, n_cols):
    m = np.zeros((n_rows, n_cols), np.float32)
    c = np.arange(n_c)[:, None]
    j = np.arange(n_s)[None, :]
    lo = c * CMP_STRIDE
    hi = lo + CMP_LEN
    m[1:n_c + 1, :n_s] = ((lo < (j + 1) * SLC_BLOCK) & (hi > j * SLC_BLOCK)).astype(np.float32)
    return jnp.asarray(m, BF16)


def _cmp_branch(qg4, cmp_tok, mcs, g, tq, t4, slope4, n_c):
    ck = cmp_tok[:, g * HEAD_DIM:(g + 1) * HEAD_DIM].astype(BF16)
    v0 = (NSA_KV + g) * HEAD_DIM
    cv = cmp_tok[:, v0:v0 + HEAD_DIM].astype(BF16)
    sc = _dot_nt(qg4, ck)
    col = lax.broadcasted_iota(I32, sc.shape, 1)
    dc = t4 - ((col - 1) * CMP_STRIDE + (CMP_LEN - 1))
    valid = (dc >= 0) & (col >= 1) & (col <= n_c)
    p, l, _ = _softmax_parts(sc - slope4 * dc.astype(F32), valid)
    pn = p / l
    o = _dot(pn.astype(BF16), cv)
    psum = pn[0:tq]
    for r in range(1, NSA_R):
        psum = psum + pn[r * tq:(r + 1) * tq]
    return o, _dot01(psum, mcs)


def _slc_select(imp, t1, n_s):
    blk = lax.broadcasted_iota(I32, imp.shape, 1)
    tb = _div(t1, SLC_BLOCK)
    avail = (blk <= tb) & (blk < n_s)
    forced = (blk == 0) | (blk == tb) | (blk == tb - 1)
    v = jnp.where(forced, jnp.inf, jnp.where(avail, imp, -jnp.inf))
    return _topk_mask(v, min(SLC_TOPN, n_s), n_s)


def _nsa_prompt_kernel(qt_ref, glt_ref, cmp_ref, mcst_ref, sk_ref, wk_ref, sv_ref, wv_ref, o_ref,
                       ksaug, kwaug, qaug_scr, selt_scr, ocmp_scr, m_st, l_st, acc_st,
                       *, tq, s_len, n_c):
    qi = pl.program_id(1)
    s0 = qi * tq
    n_s = s_len // SLC_BLOCK
    kc_len = MOBA_BLOCK
    assert tq == LANES and kc_len == 2 * tq

    @pl.when(qi == 0)
    def _():
        pos = lax.broadcasted_iota(I32, (s_len, HEAD_DIM), 0)
        for g in range(NSA_KV):
            ksaug[g] = _aug_keys(sk_ref[:, g * HEAD_DIM:(g + 1) * HEAD_DIM], pos)
            kwaug[g] = _aug_keys(wk_ref[:, g * HEAD_DIM:(g + 1) * HEAD_DIM], pos)

    for g in range(NSA_KV):
        qaug_scr[g] = jnp.concatenate(
            [_aug_query(qt_ref[h * HEAD_DIM:(h + 1) * HEAD_DIM, :], NSA_SLOPES[h])
             for h in range(g * NSA_R, (g + 1) * NSA_R)], axis=1)
    tl = s0 + lax.broadcasted_iota(I32, (1, tq), 1)
    sig = 1.0 / (1.0 + jnp.exp(-glt_ref[...]))

    gw = NSA_R * tq

    def lanes(x, n):
        return jnp.concatenate([x] * n, axis=1)

    def scores_all(keys):
        return jnp.concatenate([_dot(keys[g], qaug_scr[g]) for g in range(NSA_KV)], axis=1)

    def pv_all(vts, pb):
        return jnp.concatenate([_dot(vts[g], pb[:, g * gw:(g + 1) * gw]) for g in range(NSA_KV)], axis=1)

    cmp_tok = cmp_ref[...]
    ncp = cmp_tok.shape[0]
    cmp_t = cmp_tok.T
    crow = lax.broadcasted_iota(I32, (ncp, tq), 0)
    c_end = (crow - 1) * CMP_STRIDE + (CMP_LEN - 1)
    cbias = lanes(jnp.where((c_end <= tl) & (crow >= 1) & (crow <= n_c), 0.0, NEG), NSA_HEADS)
    any_c = lanes(jnp.where(tl >= (CMP_LEN - 1), 1.0, 0.0), NSA_HEADS) > 0.5
    cpos = jnp.maximum((lax.broadcasted_iota(I32, (ncp, HEAD_DIM), 0) - 1) * CMP_STRIDE + (CMP_LEN - 1), 0)
    blk = lax.broadcasted_iota(I32, (n_s, tq), 0)
    tb = _div(tl, SLC_BLOCK)
    avail = blk <= tb
    forced = (blk == 0) | (blk == tb) | (blk == tb - 1)
    ck_augs = [_aug_keys(cmp_tok[:, g * HEAD_DIM:(g + 1) * HEAD_DIM].astype(BF16), cpos) for g in range(NSA_KV)]
    cvts = [cmp_t[(NSA_KV + g) * HEAD_DIM:(NSA_KV + g + 1) * HEAD_DIM, :].astype(BF16) for g in range(NSA_KV)]
    p, l, _ = _softmax_cols(scores_all(ck_augs) + cbias)
    pn = jnp.where(any_c, p / l, 0.0)
    ocmp_scr[...] = pv_all(cvts, pn.astype(BF16))
    for g in range(NSA_KV):
        psum = pn[:, g * gw:g * gw + tq]
        for r in range(1, NSA_R):
            psum = psum + pn[:, g * gw + r * tq:g * gw + (r + 1) * tq]
        imp = _dot01_rhs(mcst_ref[...], psum)[0:n_s]
        v = jnp.where(forced, jnp.inf, jnp.where(avail, imp, -jnp.inf))
        selt_scr[g] = jnp.where(_topk_rows(v, min(SLC_TOPN, n_s), n_s), 1.0, 0.0)

    m_st[...] = jnp.full(m_st.shape, NEG, F32)
    l_st[...] = jnp.zeros(l_st.shape, F32)
    acc_st[...] = jnp.zeros(acc_st.shape, F32)
    cd = _div(s0, kc_len)
    bpc = kc_len // SLC_BLOCK
    ksub = lax.broadcasted_iota(I32, (kc_len, tq), 0)
    diag_bias = jnp.where(cd * kc_len + ksub <= tl, 0.0, NEG)

    def slc_chunk(c, extra):
        biases = []
        for g in range(NSA_KV):
            rows = selt_scr[g, pl.ds(c * bpc, bpc), :]
            sb = jnp.where(rows > 0.5, 0.0, NEG)
            bias = jnp.concatenate([jnp.broadcast_to(sb[i:i + 1, :], (SLC_BLOCK, tq)) for i in range(bpc)], axis=0)
            if extra is not None:
                bias = bias + extra
            biases.append(lanes(bias, NSA_R))
        s = scores_all([ksaug[g, pl.ds(c * kc_len, kc_len), :] for g in range(NSA_KV)])
        s = s + jnp.concatenate(biases, axis=1)
        m_old = m_st[0:1, :]
        m_new = jnp.maximum(m_old, jnp.max(s, axis=0, keepdims=True))
        alpha = jnp.exp(m_old - m_new)
        p = jnp.exp(s - m_new)
        l_st[0:1, :] = alpha * l_st[0:1, :] + jnp.sum(p, axis=0, keepdims=True)
        vts = [sv_ref[c, (NSA_KV + g) * HEAD_DIM:(NSA_KV + g + 1) * HEAD_DIM, :] for g in range(NSA_KV)]
        acc_st[...] = alpha * acc_st[...] + pv_all(vts, p.astype(BF16))
        m_st[0:1, :] = m_new

    def slc_loop(c, carry):
        slc_chunk(c, None)
        return carry

    lax.fori_loop(0, cd, slc_loop, 0)
    slc_chunk(cd, diag_bias)
    o_slc = acc_st[...] / l_st[0:1, :]

    nwc = WINDOW // LANES + 1
    wsub = lax.broadcasted_iota(I32, (tq, tq), 0)
    wlane = lax.broadcasted_iota(I32, (tq, tq), 1)
    cidx = [qi - (nwc - 1) + j for j in range(nwc)]
    parts = []
    for j in range(nwc):
        if j == 0:
            base = jnp.where(wsub > wlane, 0.0, NEG)
        elif j == nwc - 1:
            base = jnp.where(wsub <= wlane, 0.0, NEG)
        else:
            base = jnp.zeros((tq, tq), F32)
        parts.append(base + jnp.where(cidx[j] >= 0, 0.0, NEG))
    wbias = lanes(jnp.concatenate(parts, axis=0), NSA_HEADS)
    cclamp = [jnp.maximum(c, 0) for c in cidx]
    kws = [jnp.concatenate([kwaug[g, pl.ds(c * tq, tq), :] for c in cclamp], axis=0) for g in range(NSA_KV)]
    vws = [jnp.concatenate([wv_ref[c, (NSA_KV + g) * HEAD_DIM:(NSA_KV + g + 1) * HEAD_DIM, :] for c in cclamp], axis=1)
           for g in range(NSA_KV)]
    p, l, _ = _softmax_cols(scores_all(kws) + wbias)
    o_win = pv_all(vws, p.astype(BF16)) / l

    gates = [jnp.concatenate([sig[3 * h + k:3 * h + k + 1, :] for h in range(NSA_HEADS)], axis=1)
             for k in range(3)]
    og = ocmp_scr[...] * gates[0] + o_slc * gates[1] + o_win * gates[2]
    o_ref[...] = jnp.concatenate([og[:, h * tq:(h + 1) * tq] for h in range(NSA_HEADS)], axis=0).T.astype(BF16)


def _nsa_prompt(nq_t, gl_t, cmp_tok, sk, wk, sv_c, wv_c, nb, s_len):
    tq = LANES
    nqt = s_len // tq
    n_c = s_len // CMP_STRIDE - CMP_LEN // CMP_STRIDE + 1
    ncp = cmp_tok.shape[1]
    mcst = jnp.transpose(_cmp_to_slc(ncp, n_c, s_len // SLC_BLOCK, LANES))
    rows = 2 * NSA_KV * HEAD_DIM
    width = NSA_HEADS * HEAD_DIM
    kw = NSA_KV * HEAD_DIM
    return pl.pallas_call(
        functools.partial(_nsa_prompt_kernel, tq=tq, s_len=s_len, n_c=n_c),
        grid=(nb, nqt),
        in_specs=[pl.BlockSpec((None, width, tq), lambda b, i: (b, 0, i)),
                  pl.BlockSpec((None, 3 * NSA_HEADS, tq), lambda b, i: (b, 0, i)),
                  pl.BlockSpec((None,) + cmp_tok.shape[1:], lambda b, i: (b, 0, 0)),
                  pl.BlockSpec(mcst.shape, lambda b, i: (0, 0)),
                  pl.BlockSpec((s_len, kw), lambda b, i: (b, 0)),
                  pl.BlockSpec((s_len, kw), lambda b, i: (b, 0)),
                  pl.BlockSpec((None, s_len // MOBA_BLOCK, rows, MOBA_BLOCK), lambda b, i: (b, 0, 0, 0)),
                  pl.BlockSpec((None, s_len // LANES, rows, LANES), lambda b, i: (b, 0, 0, 0))],
        out_specs=pl.BlockSpec((tq, width), lambda b, i: (b * nqt + i, 0)),
        out_shape=jax.ShapeDtypeStruct((nb * s_len, width), BF16),
        scratch_shapes=[pltpu.VMEM((NSA_KV, s_len, LANES), BF16), pltpu.VMEM((NSA_KV, s_len, LANES), BF16),
                        pltpu.VMEM((NSA_KV, LANES, NSA_R * tq), BF16),
                        pltpu.VMEM((NSA_KV, s_len // SLC_BLOCK, tq), F32),
                        pltpu.VMEM((HEAD_DIM, NSA_HEADS * tq), F32),
                        pltpu.VMEM((8, NSA_HEADS * tq), F32), pltpu.VMEM((8, NSA_HEADS * tq), F32),
                        pltpu.VMEM((HEAD_DIM, NSA_HEADS * tq), F32)],
        compiler_params=pltpu.CompilerParams(dimension_semantics=("parallel", "arbitrary"),
                                             vmem_limit_bytes=VMEM_LIMIT),
        name="nsa_prompt",
    )(nq_t, gl_t, cmp_tok, mcst, sk, wk, sv_c, wv_c)


def _tri_matrix():
    i = np.arange(LANES)
    return jnp.asarray((i[:, None] <= i[None, :]).astype(np.float32), BF16)


def _fox_prompt_kernel(q_ref, kv_ref, lf_ref, u_ref, o_ref, c_scr, *, tq, s_len, kstep):
    qi = pl.program_id(1)

    @pl.when(qi == 0)
    def _():
        carry = jnp.zeros((FOX_HEADS, 1), F32)
        for blk in range(s_len // LANES):
            cs = _dot01(lf_ref[:, blk * LANES:(blk + 1) * LANES], u_ref[...]) + carry
            c_scr[:, blk * LANES:(blk + 1) * LANES] = cs
            carry = cs[:, LANES - 1:LANES]

    s0 = qi * tq
    t = s0 + lax.broadcasted_iota(I32, (tq, 1), 0)
    nh = FOX_HEADS

    def body(kmax):
        pos = lax.broadcasted_iota(I32, (tq, kmax), 1)
        causal = jnp.where(pos <= t, 0.0, NEG)
        outs = []
        for h in range(nh):
            qh = q_ref[:, h * HEAD_DIM:(h + 1) * HEAD_DIM]
            kt = kv_ref[h * HEAD_DIM:(h + 1) * HEAD_DIM, 0:kmax]
            vt = kv_ref[(nh + h) * HEAD_DIM:(nh + h + 1) * HEAD_DIM, 0:kmax]
            p, l, _ = _softmax_bias(_dot(qh, kt) + (causal - c_scr[h:h + 1, 0:kmax]))
            outs.append(_dot_nt(p.astype(BF16), vt) / l)
        o_ref[...] = jnp.concatenate(outs, axis=1).astype(BF16)

    for c in range(s_len // kstep):
        pl.when(_div(s0, kstep) == c)(functools.partial(body, kstep * (c + 1)))


def _fox_prompt(fq, fkv_b, lf_t, nb, s_len):
    tq = LANES
    nqt = s_len // tq
    rows = 2 * FOX_HEADS * HEAD_DIM
    tri = _tri_matrix()
    return pl.pallas_call(
        functools.partial(_fox_prompt_kernel, tq=tq, s_len=s_len, kstep=KEY_STEP),
        grid=(nb, nqt),
        in_specs=[pl.BlockSpec((tq, FOX_HEADS * HEAD_DIM), lambda b, i: (b * nqt + i, 0)),
                  pl.BlockSpec((None, rows, s_len), lambda b, i: (b, 0, 0)),
                  pl.BlockSpec((None, FOX_HEADS, s_len), lambda b, i: (b, 0, 0)),
                  pl.BlockSpec(tri.shape, lambda b, i: (0, 0))],
        out_specs=pl.BlockSpec((tq, FOX_HEADS * HEAD_DIM), lambda b, i: (b * nqt + i, 0)),
        out_shape=jax.ShapeDtypeStruct((nb * s_len, FOX_HEADS * HEAD_DIM), BF16),
        scratch_shapes=[pltpu.VMEM((FOX_HEADS, s_len), F32)],
        compiler_params=pltpu.CompilerParams(dimension_semantics=("parallel", "arbitrary"),
                                             vmem_limit_bytes=VMEM_LIMIT),
        name="fox_prompt",
    )(fq, fkv_b, lf_t, tri)


def _moba_prompt_kernel(qt_ref, k_ref, kv_ref, eavg_ref, o_ref, kaug, km_scr, qaug_scr, m_scr, l_scr, o_scr,
                        *, tq, s_len):
    qi = pl.program_id(1)
    n_b = s_len // MOBA_BLOCK
    nbp = m_scr.shape[1]
    assert tq == LANES and MOBA_BLOCK == 2 * tq

    @pl.when(qi == 0)
    def _():
        pos = lax.broadcasted_iota(I32, (s_len, HEAD_DIM), 0)
        for g in range(MOBA_KV):
            kaug[g] = _aug_keys(k_ref[:, g * HEAD_DIM:(g + 1) * HEAD_DIM], pos)
        km_scr[...] = _dot(eavg_ref[...], k_ref[...]).astype(BF16)
        m_scr[...] = jnp.zeros(m_scr.shape, F32)
        l_scr[...] = jnp.zeros(l_scr.shape, F32)
        o_scr[...] = jnp.zeros(o_scr.shape, F32)

    s0 = qi * tq
    tb = _div(s0, MOBA_BLOCK)
    tl = s0 + lax.broadcasted_iota(I32, (1, tq), 1)
    for g in range(MOBA_KV):
        qaug_scr[g] = jnp.concatenate(
            [_aug_query(qt_ref[h * HEAD_DIM:(h + 1) * HEAD_DIM, :], MOBA_SLOPES[h])
             for h in range(g * MOBA_R, (g + 1) * MOBA_R)], axis=1)

    def block_partial(j, bias):
        gw = MOBA_R * tq
        s = jnp.concatenate([_dot(kaug[g, pl.ds(j * MOBA_BLOCK, MOBA_BLOCK), :], qaug_scr[g])
                             for g in range(MOBA_KV)], axis=1)
        if bias is not None:
            s = s + jnp.concatenate([bias] * MOBA_KV, axis=1)
        p, l, m = _softmax_cols(s)
        pb = p.astype(BF16)
        res = []
        for g in range(MOBA_KV):
            vtc = kv_ref[j, (MOBA_KV + g) * HEAD_DIM:(MOBA_KV + g + 1) * HEAD_DIM, :]
            sl = slice(g * gw, (g + 1) * gw)
            res.append((m[:, sl], l[:, sl], _dot(vtc, pb[:, sl])))
        return res

    def past_block(j, carry):
        for g, (m, l, o) in enumerate(block_partial(j, None)):
            m_scr[g, pl.ds(j, 1), :] = m
            l_scr[g, pl.ds(j, 1), :] = l
            o_scr[g, j] = o
        return carry

    lax.fori_loop(0, tb, past_block, 0)
    ksub = lax.broadcasted_iota(I32, (MOBA_BLOCK, tq), 0)
    own_bias = jnp.where(tb * MOBA_BLOCK + ksub <= tl, 0.0, NEG)
    own = block_partial(tb, jnp.concatenate([own_bias] * MOBA_R, axis=1))

    row = lax.broadcasted_iota(I32, (nbp, MOBA_R * tq), 0)
    outs = []
    for g in range(MOBA_KV):
        m_o, l_o, o_o = own[g]
        gate = _dot(km_scr[:, g * HEAD_DIM:(g + 1) * HEAD_DIM], qaug_scr[g, 0:HEAD_DIM, :])
        sel = _topk_rows(jnp.where(row < tb, gate, -jnp.inf), min(MOBA_TOPK, n_b), n_b)
        mm = m_scr[g]
        mx = jnp.maximum(jnp.max(jnp.where(sel, mm, NEG), axis=0, keepdims=True), m_o)
        w = jnp.where(sel, jnp.exp(mm - mx), 0.0)
        w_o = jnp.exp(m_o - mx)
        den = jnp.sum(w * l_scr[g], axis=0, keepdims=True) + w_o * l_o
        num = w_o * o_o
        for j in range(n_b - 1):
            num = num + w[j:j + 1, :] * o_scr[g, j]
        og = num / den
        outs += [og[:, r * tq:(r + 1) * tq] for r in range(MOBA_R)]
    o_ref[...] = jnp.concatenate(outs, axis=0).T.astype(BF16)


def _moba_prompt(mq_t, mk, mkv_c, nb, s_len):
    tq = LANES
    nqt = s_len // tq
    rows = 2 * MOBA_KV * HEAD_DIM
    width = MOBA_HEADS * HEAD_DIM
    kw = MOBA_KV * HEAD_DIM
    n_b = s_len // MOBA_BLOCK
    nbp = 16
    assert n_b <= nbp
    e = np.zeros((nbp, s_len), np.float32)
    e[np.arange(s_len) // MOBA_BLOCK, np.arange(s_len)] = 1.0 / MOBA_BLOCK
    eavg = jnp.asarray(e, BF16)
    return pl.pallas_call(
        functools.partial(_moba_prompt_kernel, tq=tq, s_len=s_len),
        grid=(nb, nqt),
        in_specs=[pl.BlockSpec((None, width, tq), lambda b, i: (b, 0, i)),
                  pl.BlockSpec((s_len, kw), lambda b, i: (b, 0)),
                  pl.BlockSpec((None, n_b, rows, MOBA_BLOCK), lambda b, i: (b, 0, 0, 0)),
                  pl.BlockSpec(eavg.shape, lambda b, i: (0, 0))],
        out_specs=pl.BlockSpec((tq, width), lambda b, i: (b * nqt + i, 0)),
        out_shape=jax.ShapeDtypeStruct((nb * s_len, width), BF16),
        scratch_shapes=[pltpu.VMEM((MOBA_KV, s_len, LANES), BF16),
                        pltpu.VMEM((nbp, kw), BF16),
                        pltpu.VMEM((MOBA_KV, LANES, MOBA_R * tq), BF16),
                        pltpu.VMEM((MOBA_KV, nbp, MOBA_R * tq), F32), pltpu.VMEM((MOBA_KV, nbp, MOBA_R * tq), F32),
                        pltpu.VMEM((MOBA_KV, n_b, HEAD_DIM, MOBA_R * tq), F32)],
        compiler_params=pltpu.CompilerParams(dimension_semantics=("parallel", "arbitrary"),
                                             vmem_limit_bytes=VMEM_LIMIT),
        name="moba_prompt",
    )(mq_t, mk, mkv_c, eavg)


def _post_kernel(*refs, n_o, final):
    x_ref = refs[0]
    o_refs = refs[1:1 + n_o]
    wo_refs = refs[1 + n_o:1 + 2 * n_o]
    g_ref, wup_ref, wdn_ref, gf_ref, out_ref, x1_scr, h_scr, acc_scr = refs[1 + 2 * n_o:]
    j = pl.program_id(1)

    @pl.when(j == 0)
    def _():
        x1 = x_ref[...]
        for o_ref, wo_ref in zip(o_refs, wo_refs):
            x1 = x1 + _dot(o_ref[...], wo_ref[...])
        x1_scr[...] = x1
        ms = jnp.mean(x1 * x1, axis=-1, keepdims=True)
        h_scr[...] = ((x1 * lax.rsqrt(ms + RMS_EPS)) * g_ref[...]).astype(BF16)
        acc_scr[...] = jnp.zeros(acc_scr.shape, F32)

    u = jnp.maximum(_dot(h_scr[...], wup_ref[...]), 0.0)
    acc_scr[...] += _dot((u * u).astype(BF16), wdn_ref[...])

    @pl.when(j == pl.num_programs(1) - 1)
    def _():
        y = x1_scr[...] + acc_scr[...]
        if final:
            ms = jnp.mean(y * y, axis=-1, keepdims=True)
            y = (y * lax.rsqrt(ms + RMS_EPS)) * gf_ref[...]
        out_ref[...] = y


def _post(x2d, o_list, wo_list, g_mlp, w_up, w_down, g_final, final, tm, tf=1024):
    t_tot = x2d.shape[0]
    n_o = len(o_list)
    in_specs = [pl.BlockSpec((tm, D_MODEL), lambda i, j: (i, 0))]
    in_specs += [pl.BlockSpec((tm, o.shape[1]), lambda i, j: (i, 0)) for o in o_list]
    in_specs += [pl.BlockSpec(w.shape, lambda i, j: (0, 0)) for w in wo_list]
    in_specs += [pl.BlockSpec((1, D_MODEL), lambda i, j: (0, 0)),
                 pl.BlockSpec((D_MODEL, tf), lambda i, j: (0, j)),
                 pl.BlockSpec((tf, D_MODEL), lambda i, j: (j, 0)),
                 pl.BlockSpec((1, D_MODEL), lambda i, j: (0, 0))]
    return pl.pallas_call(
        functools.partial(_post_kernel, n_o=n_o, final=final),
        grid=(t_tot // tm, D_FF // tf),
        in_specs=in_specs,
        out_specs=pl.BlockSpec((tm, D_MODEL), lambda i, j: (i, 0)),
        out_shape=jax.ShapeDtypeStruct((t_tot, D_MODEL), F32),
        scratch_shapes=[pltpu.VMEM((tm, D_MODEL), F32), pltpu.VMEM((tm, D_MODEL), BF16),
                        pltpu.VMEM((tm, D_MODEL), F32)],
        compiler_params=pltpu.CompilerParams(dimension_semantics=("parallel", "arbitrary"),
                                             vmem_limit_bytes=VMEM_LIMIT),
        name="post_mlp",
    )(x2d, *o_list, *wo_list, g_mlp, w_up, w_down, g_final)


def _nsa_dec_kernel(q_ref, cmp_ref, mcs_ref, kw_ref, ocmp_ref, owin_ref, sel_ref, *, tq, q0, n_c, n_s):
    q = q_ref[...]
    cmp_tok = cmp_ref[...]
    mcs = mcs_ref[...]
    row4 = lax.broadcasted_iota(I32, (NSA_R * tq, 1), 0)
    t4 = q0 + _mod(row4, tq)
    r4 = _div(row4, tq)
    t1 = q0 + lax.broadcasted_iota(I32, (tq, 1), 0)
    nbp = mcs.shape[1]
    wlen = kw_ref.shape[1]
    for g in range(NSA_KV):
        qg4 = _stack_heads(q, g, NSA_R)
        slope4 = _row_slopes(r4, NSA_SLOPES, g, NSA_R)
        o_cmp, imp = _cmp_branch(qg4, cmp_tok, mcs, g, tq, t4, slope4, n_c)
        sel = _slc_select(imp, t1, n_s)
        sel_ref[:, g * nbp:(g + 1) * nbp] = jnp.where(sel, 1.0, 0.0)
        kwin = kw_ref[g * HEAD_DIM:(g + 1) * HEAD_DIM, :].astype(BF16)
        vwin = kw_ref[(NSA_KV + g) * HEAD_DIM:(NSA_KV + g + 1) * HEAD_DIM, :].astype(BF16)
        wp = (q0 - WINDOW) + lax.broadcasted_iota(I32, (NSA_R * tq, wlen), 1)
        dw = t4 - wp
        valid = (wp >= 0) & (dw >= 0) & (dw < WINDOW)
        p, l, _ = _softmax_parts(_dot(qg4, kwin) - slope4 * dw.astype(F32), valid)
        o_win = _dot_nt(p.astype(BF16), vwin) / l
        for r in range(NSA_R):
            h = g * NSA_R + r
            ocmp_ref[:, h * HEAD_DIM:(h + 1) * HEAD_DIM] = o_cmp[r * tq:(r + 1) * tq]
            owin_ref[:, h * HEAD_DIM:(h + 1) * HEAD_DIM] = o_win[r * tq:(r + 1) * tq]


def _nsa_dec(q8, cmp_tok, kwin_t, nb, q0, n_c, n_s):
    tq = q8.shape[1]
    nbp = -(-n_s // LANES) * LANES
    mcs = _cmp_to_slc(cmp_tok.shape[1], n_c, n_s, nbp)
    width = NSA_HEADS * HEAD_DIM
    return pl.pallas_call(
        functools.partial(_nsa_dec_kernel, tq=tq, q0=q0, n_c=n_c, n_s=n_s),
        grid=(nb,),
        in_specs=[pl.BlockSpec((None, tq, width), lambda b: (b, 0, 0)),
                  pl.BlockSpec((None,) + cmp_tok.shape[1:], lambda b: (b, 0, 0)),
                  pl.BlockSpec(mcs.shape, lambda b: (0, 0)),
                  pl.BlockSpec((None,) + kwin_t.shape[1:], lambda b: (b, 0, 0))],
        out_specs=[pl.BlockSpec((None, tq, width), lambda b: (b, 0, 0)),
                   pl.BlockSpec((None, tq, width), lambda b: (b, 0, 0)),
                   pl.BlockSpec((None, tq, NSA_KV * nbp), lambda b: (b, 0, 0))],
        out_shape=[jax.ShapeDtypeStruct((nb, tq, width), F32),
                   jax.ShapeDtypeStruct((nb, tq, width), F32),
                   jax.ShapeDtypeStruct((nb, tq, NSA_KV * nbp), F32)],
        compiler_params=pltpu.CompilerParams(dimension_semantics=("parallel",), vmem_limit_bytes=VMEM_LIMIT),
        name="nsa_decode_cmp_win",
    )(q8, cmp_tok, mcs, kwin_t)


def _stream_kernel(*refs, mode, nseg, ppseg, n_rows, hk, q0, n_segs):
    pps = nseg * ppseg
    refs = refs[1:]
    qbd_ref, slope_ref, t_ref = refs[0:3]
    k = 3
    sel_ref = selnew_ref = em_ref = None
    if mode == "slc":
        sel_ref, selnew_ref, em_ref = refs[k:k + 3]
        k += 3
    pages = refs[k:k + pps]
    k += pps
    lf_pages = None
    if mode == "fox":
        lf_pages = refs[k:k + pps]
        k += pps
    new_ref = refs[k]
    k += 1
    newlf_ref = tri_ref = None
    if mode == "fox":
        newlf_ref, tri_ref = refs[k], refs[k + 1]
        k += 2
    out_ref = refs[k]
    oparts, m_s, l_s, x_s = refs[k + 1:k + 5]
    st = pl.program_id(1)
    nsteps = pl.num_programs(1)
    qbd = qbd_ref[...]
    slope = slope_ref[...]
    tcol = t_ref[...]
    lane = lax.broadcasted_iota(I32, (n_rows, LANES), 1)
    seg_lane = lax.broadcasted_iota(I32, m_s.shape, 1)
    reps = n_rows // FOX_HEADS

    @pl.when(st == 0)
    def _():
        m_s[...] = jnp.zeros(m_s.shape, F32)
        l_s[...] = jnp.zeros(l_s.shape, F32)
        x_s[...] = jnp.zeros(x_s.shape, F32)

    def scores(page, page_pos0, lf_page, run):
        kt = page[0:hk, :].astype(BF16)
        vt = page[hk:2 * hk, :].astype(BF16)
        s_raw = _dot(qbd, kt)
        if mode == "fox":
            cs = _dot01(lf_page, tri_ref[...]) + run
            run = cs[:, LANES - 1:LANES]
            s = s_raw - jnp.concatenate([cs] * reps, axis=0)
        else:
            s = s_raw - slope * (tcol - (page_pos0 + lane)).astype(F32)
        return s_raw, s, vt, run

    def sel_mask(sref, width):
        mexp = _dot(sref[...].astype(BF16), em_ref[:, 0:width])
        return jnp.where(mexp > 0.5, 0.0, NEG)

    m_all, l_all, x_all = m_s[...], l_s[...], x_s[...]
    mb = sel_mask(sel_ref, pps * PAGE) if mode == "slc" else None
    for sg in range(nseg):
        run = jnp.zeros((FOX_HEADS, 1), F32)
        ss, vts, raws = [], [], []
        for i in range(sg * ppseg, (sg + 1) * ppseg):
            s_raw, s, vt, run = scores(pages[i][...], (st * pps + i) * PAGE,
                                       lf_pages[i][...] if mode == "fox" else None, run)
            if mode == "slc":
                s = s + jnp.concatenate([mb[:, i * PAGE:(i + 1) * PAGE]] * (n_rows // mb.shape[0]), axis=0)
            ss.append(s)
            vts.append(vt)
            raws.append(s_raw)
        smax = ss[0]
        for s in ss[1:]:
            smax = jnp.maximum(smax, s)
        m = smax.max(axis=-1, keepdims=True)
        psum = jnp.zeros((n_rows, LANES), F32)
        o = jnp.zeros((n_rows, hk), F32)
        for s, vt in zip(ss, vts):
            p = jnp.exp(s - m)
            psum = psum + p
            o = o + _dot_nt(p.astype(BF16), vt)
        l = psum.sum(axis=-1, keepdims=True)
        seg = st * nseg + sg
        oparts[seg] = o
        m_all = jnp.where(seg_lane == seg, m, m_all)
        l_all = jnp.where(seg_lane == seg, l, l_all)
        if mode == "fox":
            x_all = jnp.where(seg_lane == seg, jnp.concatenate([run] * reps, axis=0), x_all)
        elif mode == "moba":
            rsum = raws[0]
            for r_ in raws[1:]:
                rsum = rsum + r_
            gsum = rsum.sum(axis=-1, keepdims=True)
            x_all = jnp.where(seg_lane == seg, gsum * (1.0 / MOBA_BLOCK), x_all)
    m_s[...] = m_all
    l_s[...] = l_all
    x_s[...] = x_all

    @pl.when(st == nsteps - 1)
    def _():
        last = n_segs - 1
        _, s, vt, _ = scores(new_ref[...], q0, newlf_ref[...] if mode == "fox" else None,
                             jnp.zeros((FOX_HEADS, 1), F32))
        if mode == "slc":
            mnew = sel_mask(selnew_ref, PAGE)
            s = s + jnp.concatenate([mnew] * (n_rows // mnew.shape[0]), axis=0)
        p, l, m = _softmax_parts(s, ((q0 + lane) <= tcol) & (s > 0.5 * NEG))
        oparts[last] = _dot_nt(p.astype(BF16), vt)
        mm = jnp.where(seg_lane == last, m, m_all)
        ll = jnp.where(seg_lane == last, l, l_all)
        if mode == "fox":
            mm = mm + _dot01(x_all, tri_ref[...], nt=True)
            valid = seg_lane < n_segs
        elif mode == "slc":
            valid = seg_lane < n_segs
        else:
            n_b = n_segs - 1
            gate = jnp.where(seg_lane < n_b, x_all, -jnp.inf)
            valid = _topk_mask(gate, min(MOBA_TOPK, n_b), n_b) | (seg_lane == last)
        mx = jnp.max(jnp.where(valid, mm, NEG), axis=-1, keepdims=True)
        w = jnp.where(valid, jnp.exp(mm - mx), 0.0)
        den = jnp.maximum(jnp.sum(w * ll, axis=-1, keepdims=True), 1e-30)
        num = jnp.zeros((n_rows, hk), F32)
        for seg in range(n_segs):
            num = num + w[:, seg:seg + 1] * oparts[seg]
        out_ref[...] = num / den


def _stream(mode, qbd, slope_col, t_col, sel_steps, pool_t, lf_pool_t, new_t, newlf_t, page_table, q0, nseg, ppseg):
    nb, n_rows, hk = qbd.shape
    n_pages = page_table.shape[1]
    pps = nseg * ppseg
    nsteps = n_pages // pps
    n_segs = nsteps * nseg + 1
    segp = LANES
    assert n_segs <= segp and n_pages % pps == 0

    def cst(shape):
        nd = len(shape)
        return pl.BlockSpec(shape, lambda b, s, pt: (0,) * nd)

    def per_b(shape):
        nd = len(shape)
        return pl.BlockSpec((None,) + shape, lambda b, s, pt: (b,) + (0,) * nd)

    in_specs = [per_b((n_rows, hk)), cst((n_rows, 1)), cst((n_rows, 1))]
    args = [qbd, slope_col, t_col]
    if mode == "slc":
        nsel = sel_steps.shape[2]
        blocks_per_step = pps * (PAGE // SLC_BLOCK)
        assert blocks_per_step <= LANES and sel_steps.shape[1] == nsteps + 1
        em = np.zeros((LANES, pps * PAGE), np.float32)
        em[np.arange(pps * PAGE) // SLC_BLOCK, np.arange(pps * PAGE)] = 1.0
        em = jnp.asarray(em, BF16)
        in_specs += [pl.BlockSpec((None, None, nsel, LANES), lambda b, s, pt: (b, s, 0, 0)),
                     pl.BlockSpec((None, None, nsel, LANES), lambda b, s, pt: (b, nsteps, 0, 0)),
                     cst(em.shape)]
        args += [sel_steps, sel_steps, em]
    for i in range(pps):
        in_specs.append(pl.BlockSpec((None, 2 * hk, PAGE), lambda b, s, pt, i=i: (pt[b, s * pps + i], 0, 0)))
        args.append(pool_t)
    if mode == "fox":
        for i in range(pps):
            in_specs.append(pl.BlockSpec((None, FOX_HEADS, PAGE), lambda b, s, pt, i=i: (pt[b, s * pps + i], 0, 0)))
            args.append(lf_pool_t)
    in_specs.append(per_b((2 * hk, PAGE)))
    args.append(new_t)
    if mode == "fox":
        tri = _tri_matrix()
        in_specs += [per_b((FOX_HEADS, PAGE)), cst(tri.shape)]
        args += [newlf_t, tri]
    grid_spec = pltpu.PrefetchScalarGridSpec(
        num_scalar_prefetch=1, grid=(nb, nsteps), in_specs=in_specs,
        out_specs=pl.BlockSpec((None, n_rows, hk), lambda b, s, pt: (b, 0, 0)),
        scratch_shapes=[pltpu.VMEM((n_segs, n_rows, hk), F32), pltpu.VMEM((n_rows, segp), F32),
                        pltpu.VMEM((n_rows, segp), F32), pltpu.VMEM((n_rows, segp), F32)])
    return pl.pallas_call(
        functools.partial(_stream_kernel, mode=mode, nseg=nseg, ppseg=ppseg, n_rows=n_rows, hk=hk, q0=q0,
                          n_segs=n_segs),
        grid_spec=grid_spec,
        out_shape=jax.ShapeDtypeStruct((nb, n_rows, hk), F32),
        compiler_params=pltpu.CompilerParams(dimension_semantics=("parallel", "arbitrary"),
                                             vmem_limit_bytes=VMEM_LIMIT),
        name="decode_stream_" + mode,
    )(page_table, *args)


def _nsa_gate_kernel(oc_ref, os_ref, ow_ref, gl_ref, o_ref):
    sig = 1.0 / (1.0 + jnp.exp(-gl_ref[...]))
    for h in range(NSA_HEADS):
        sl = slice(h * HEAD_DIM, (h + 1) * HEAD_DIM)
        o_ref[:, sl] = (oc_ref[:, sl] * sig[:, 3 * h:3 * h + 1] + os_ref[:, sl] * sig[:, 3 * h + 1:3 * h + 2]
                        + ow_ref[:, sl] * sig[:, 3 * h + 2:3 * h + 3]).astype(BF16)


def _nsa_gate(oc, os_, ow, gl):
    return pl.pallas_call(
        _nsa_gate_kernel,
        out_shape=jax.ShapeDtypeStruct(oc.shape, BF16),
        name="nsa_gate",
    )(oc, os_, ow, gl)


def _pool_t(cache, li):
    c = jnp.transpose(cache[li], (0, 2, 3, 4, 1))
    return c.reshape(c.shape[0], -1, c.shape[-1])


def _block_diag_q(q, n_tok, n_kv, n_r, r_major=False):
    b = q.shape[0]
    q5 = q.reshape(b, n_tok, n_kv, n_r, 1, HEAD_DIM)
    eye = jnp.eye(n_kv, dtype=q.dtype).reshape(1, 1, n_kv, 1, n_kv, 1)
    x = (q5 * eye).reshape(b, n_tok, n_kv, n_r, n_kv * HEAD_DIM)
    if r_major:
        x = x.transpose(0, 3, 1, 2, 4)
    return x.reshape(b, n_tok * n_kv * n_r, n_kv * HEAD_DIM)


def _diag_heads(o, n_tok, n_kv, n_r, r_major=False):
    b = o.shape[0]
    if r_major:
        o6 = o.reshape(b, n_r, n_tok, n_kv, n_kv, HEAD_DIM).transpose(0, 2, 3, 1, 4, 5)
    else:
        o6 = o.reshape(b, n_tok, n_kv, n_r, n_kv, HEAD_DIM)
    d = jnp.einsum("btgrgd->btgrd", o6)
    return d.reshape(b * n_tok, n_kv * n_r * HEAD_DIM)


def _row_consts(n_tok, n_kv, n_r, slopes, q0, r_major=False):
    t, g, r = np.meshgrid(np.arange(n_tok), np.arange(n_kv), np.arange(n_r), indexing="ij")
    h = g * n_r + r
    if r_major:
        t, h = t.transpose(2, 0, 1), h.transpose(2, 0, 1)
    t, h = t.reshape(-1), h.reshape(-1)
    sl = np.asarray(slopes, np.float32)[h] if slopes is not None else np.zeros(h.shape, np.float32)
    return jnp.asarray(sl.reshape(-1, 1), F32), jnp.asarray((q0 + t).reshape(-1, 1), I32)


def _new_pages(kv_t, nb, n_tok):
    rows = kv_t.shape[0]
    x = kv_t.reshape(rows, nb, n_tok).transpose(1, 0, 2)
    return jnp.pad(x, ((0, 0), (0, 0), (0, PAGE - n_tok)))


def kernel(x_prompt, x_sample, cache_nsa_cmp_kv, cache_nsa_slc_kv, state_nsa_win_kv, cache_fox_kv, cache_fox_lf,
           cache_moba_kv, page_table, norm_mix, norm_mlp, w_in_even, b_fgt, w_out_even, cmp_pe, cmp_w1, cmp_w2,
           w_in_odd, w_out_odd, w_up, w_down, norm_final):
    nb_p, s_len, _ = x_prompt.shape
    nb_d, n_tok, _ = x_sample.shape
    n_pages = page_table.shape[1]
    past = n_pages * PAGE
    nq_w = NSA_HEADS * HEAD_DIM
    nkv_w = 2 * NSA_KV * HEAD_DIM
    fq_w = FOX_HEADS * HEAD_DIM
    fkv_w = 2 * FOX_HEADS * HEAD_DIM
    mq_w = MOBA_HEADS * HEAD_DIM
    mkv_w = 2 * MOBA_KV * HEAD_DIM
    ngl = 3 * NSA_HEADS

    wte = jnp.transpose(w_in_even[0])
    o_nq, o_ckv, o_skv, o_wkv = 0, nq_w, nq_w + nkv_w, nq_w + 2 * nkv_w
    o_gl = nq_w + 3 * nkv_w
    o_fq = o_gl + ngl
    o_fkv = o_fq + fq_w
    o_fl = o_fkv + fkv_w
    wn_e = jnp.concatenate([wte[o_nq:o_nq + nq_w], wte[o_fq:o_fq + fq_w], wte[o_gl:o_gl + ngl],
                            jnp.zeros((LANES - ngl, D_MODEL), F32)], axis=0)
    wn_e = jnp.transpose(wn_e).astype(BF16)
    wt_e = jnp.concatenate([wte[o_ckv:o_ckv + 3 * nkv_w], wte[o_fkv:o_fkv + fkv_w], wte[o_fl:o_fl + FOX_HEADS]],
                           axis=0).astype(BF16)
    bias_e = b_fgt[0].reshape(FOX_HEADS, 1)
    nat_e = [(0, nq_w, "q"), (nq_w, fq_w, "q"), (nq_w + fq_w, LANES, "f32")]
    r_skv, r_wkv, r_fkv, r_fl = nkv_w, 2 * nkv_w, 3 * nkv_w, 3 * nkv_w + fkv_w
    tr_e_dec = [(0, nkv_w, "f32"), (r_skv, nkv_w, "f32"), (r_wkv, nkv_w, "f32"), (r_fkv, fkv_w, "f32"),
                (r_fl, FOX_HEADS, "lf")]
    wo_t = jnp.transpose(w_in_odd[0])
    wn_o = w_in_odd[0][:, :mq_w].astype(BF16)
    wt_o = wo_t[mq_w:].astype(BF16)
    bias_o = jnp.zeros((8, 1), F32)
    nat_o = [(0, mq_w, "q")]
    tr_o_dec = [(0, mkv_w, "f32")]
    hk_n = NSA_KV * HEAD_DIM
    wn_ep = jnp.transpose(jnp.concatenate([wte[o_fq:o_fq + fq_w], wte[o_skv:o_skv + hk_n],
                                           wte[o_wkv:o_wkv + hk_n]], axis=0)).astype(BF16)
    wt_ep = jnp.concatenate([wte[o_ckv:o_ckv + 3 * nkv_w], wte[o_fkv:o_fkv + fkv_w], wte[o_fl:o_fl + FOX_HEADS],
                             wte[o_nq:o_nq + nq_w], wte[o_gl:o_gl + ngl]], axis=0).astype(BF16)
    nat_ep = [(0, fq_w, "q"), (fq_w, hk_n, "kb"), (fq_w + hk_n, hk_n, "kb")]
    r_nq = r_fl + FOX_HEADS
    r_gl = r_nq + nq_w
    tr_e_prompt = [(0, nkv_w, "f32"), (r_skv, nkv_w, "f32"), (r_skv, nkv_w, "bf16c2"), (r_wkv, nkv_w, "f32"),
                   (r_wkv, nkv_w, "bf16c"), (r_fkv, fkv_w, "f32"), (r_fkv, fkv_w, "bf16"), (r_fl, FOX_HEADS, "lf"),
                   (r_nq, nq_w, "qt"), (r_gl, ngl, "f32")]
    hk_m = MOBA_KV * HEAD_DIM
    wn_op = w_in_odd[0][:, mq_w:mq_w + hk_m].astype(BF16)
    wt_op = jnp.concatenate([wo_t[mq_w:], wo_t[:mq_w]], axis=0).astype(BF16)
    nat_op = [(0, hk_m, "kb")]
    tr_o_prompt = [(0, mkv_w, "f32"), (0, mkv_w, "bf16c2"), (mkv_w, mq_w, "qt")]
    woe = w_out_even[0].astype(BF16)
    woe_a, woe_b = woe[:nq_w], woe[nq_w:]
    woo = w_out_odd[0].astype(BF16)
    wup = w_up.astype(BF16)
    wdn = w_down.astype(BF16)
    g_mix = norm_mix.reshape(norm_mix.shape[0], 1, D_MODEL)
    g_mlp = norm_mlp.reshape(norm_mlp.shape[0], 1, D_MODEL)
    g_fin = norm_final.reshape(1, D_MODEL)
    cw = _compress_weights(cmp_pe[0], cmp_w1[0], cmp_w2[0])

    xp = x_prompt.reshape(nb_p * s_len, D_MODEL)
    tm_p = 512
    (fq, sk_n, wk_n, ckv_t, skv_t, skv_c, wkv_t, wkv_c, fkv_t, fkv_b, lf_t, nq_t, gl_t) = _proj(
        xp, g_mix[0], wn_ep, wt_ep, bias_e, nat_ep, tr_e_prompt, nb_p, s_len, tm_p)
    cmp_tok = _compress(ckv_t, None, cw, nb_p, s_len // PAGE, s_len // PAGE, paged=False)
    o_nsa = _nsa_prompt(nq_t, gl_t, cmp_tok, sk_n, wk_n, skv_c, wkv_c, nb_p, s_len)
    o_fox = _fox_prompt(fq, fkv_b, lf_t, nb_p, s_len)
    tm_post = tm_p
    xp = _post(xp, [o_nsa, o_fox], [woe_a, woe_b], g_mlp[0], wup[0], wdn[0], g_fin, False, tm_post)
    mk_n, mkv_t, mkv_c, mq_t = _proj(xp, g_mix[1], wn_op, wt_op, bias_o, nat_op, tr_o_prompt, nb_p, s_len, tm_p)
    o_moba = _moba_prompt(mq_t, mk_n, mkv_c, nb_p, s_len)
    yp = _post(xp, [o_moba], [woo], g_mlp[1], wup[1], wdn[1], g_fin, True, tm_post)
    y_prompt = yp.reshape(nb_p, s_len, D_MODEL)

    def kv_out(t, n_h):
        b, _, s = t.shape
        return jnp.transpose(t.reshape(b, 2, n_h, HEAD_DIM, s), (0, 4, 1, 2, 3))[None]

    p_cmp = kv_out(ckv_t, NSA_KV)
    p_slc = kv_out(skv_t, NSA_KV)
    wb = min(WINDOW, s_len)
    p_win = kv_out(wkv_t[:, :, s_len - wb:], NSA_KV)
    p_fox = kv_out(fkv_t, FOX_HEADS)
    p_lf = jnp.transpose(lf_t, (0, 2, 1))[None]
    p_moba = kv_out(mkv_t, MOBA_KV)

    td = nb_d * n_tok
    xd = x_sample.reshape(td, D_MODEL)
    (nq_d, fq_d, gl_d, ckv_d, skv_d, wkv_d, fkv_d, lf_d) = _proj(
        xd, g_mix[0], wn_e, wt_e, bias_e, nat_e, tr_e_dec, 1, td, td)

    def kv_out_dec(t, n_h):
        return jnp.transpose(t[0]).reshape(1, nb_d, n_tok, 2, n_h, HEAD_DIM)

    s_cmp = kv_out_dec(ckv_d, NSA_KV)
    s_slc = kv_out_dec(skv_d, NSA_KV)
    s_fox = kv_out_dec(fkv_d, FOX_HEADS)
    s_lf = jnp.transpose(lf_d[0]).reshape(1, nb_d, n_tok, FOX_HEADS)
    win_state_t = jnp.transpose(state_nsa_win_kv[0], (0, 2, 3, 4, 1)).reshape(nb_d, nkv_w, -1)
    wkv_new = wkv_d[0].reshape(nkv_w, nb_d, n_tok).transpose(1, 0, 2)
    win_all = jnp.concatenate([win_state_t, wkv_new], axis=2)
    wbuf = win_state_t.shape[2]
    s_win = jnp.transpose(win_all[:, :, -wbuf:].reshape(nb_d, 2, NSA_KV, HEAD_DIM, wbuf), (0, 4, 1, 2, 3))[None]
    assert wbuf == WINDOW
    wpad = -(-(wbuf + n_tok) // LANES) * LANES
    kwin_t = jnp.pad(win_all, ((0, 0), (0, 0), (0, wpad - wbuf - n_tok)))

    assert (past + n_tok) // CMP_STRIDE == past // CMP_STRIDE
    cmp_pool = _pool_t(cache_nsa_cmp_kv, 0)
    cmp_tok_d = _compress(cmp_pool, page_table, cw, nb_d, n_pages, 32, paged=True)
    n_c = past // CMP_STRIDE - CMP_LEN // CMP_STRIDE + 1
    n_s = -(-(past + n_tok) // SLC_BLOCK)
    tq_d = 8
    q8 = jnp.pad(nq_d.reshape(nb_d, n_tok, nq_w), ((0, 0), (0, tq_d - n_tok), (0, 0)))
    o_cmp8, o_win8, sel8 = _nsa_dec(q8, cmp_tok_d, kwin_t, nb_d, past, n_c, n_s)
    o_cmp_d = o_cmp8[:, :n_tok].reshape(td, nq_w)
    o_win_d = o_win8[:, :n_tok].reshape(td, nq_w)
    nbp = sel8.shape[2] // NSA_KV
    pps_d = 16
    nsteps_d = n_pages // pps_d
    bps = pps_d * (PAGE // SLC_BLOCK)
    sel_tg = sel8[:, :n_tok].reshape(nb_d, n_tok * NSA_KV, nbp)[:, :, :n_s]
    sel_tg = jnp.pad(sel_tg, ((0, 0), (0, 0), (0, (nsteps_d + 1) * bps - n_s)))
    sel_steps = sel_tg.reshape(nb_d, n_tok * NSA_KV, nsteps_d + 1, bps).transpose(0, 2, 1, 3)
    sel_steps = jnp.pad(sel_steps, ((0, 0), (0, 0), (0, 0), (0, LANES - bps)))
    sl_nsa, t_nsa = _row_consts(n_tok, NSA_KV, NSA_R, NSA_SLOPES, past, r_major=True)
    qbd_s = _block_diag_q(nq_d.reshape(nb_d, n_tok, nq_w), n_tok, NSA_KV, NSA_R, r_major=True)
    new_s = _new_pages(skv_d[0], nb_d, n_tok)
    o_slc_bd = _stream("slc", qbd_s, sl_nsa, t_nsa, sel_steps, _pool_t(cache_nsa_slc_kv, 0), None, new_s, None,
                       page_table, past, 1, pps_d)
    o_slc_d = _diag_heads(o_slc_bd, n_tok, NSA_KV, NSA_R, r_major=True)
    o_nsa_d = _nsa_gate(o_cmp_d, o_slc_d, o_win_d, gl_d)
    sl_fox, t_fox = _row_consts(n_tok, FOX_HEADS, 1, None, past)
    qbd_f = _block_diag_q(fq_d.reshape(nb_d, n_tok, fq_w), n_tok, FOX_HEADS, 1)
    new_f = _new_pages(fkv_d[0], nb_d, n_tok)
    newlf = _new_pages(lf_d[0], nb_d, n_tok)
    lf_pool = jnp.transpose(cache_fox_lf[0], (0, 2, 1))
    o_fox_bd = _stream("fox", qbd_f, sl_fox, t_fox, None, _pool_t(cache_fox_kv, 0), lf_pool, new_f, newlf,
                       page_table, past, 1, pps_d)
    o_fox_d = _diag_heads(o_fox_bd, n_tok, FOX_HEADS, 1).astype(BF16)
    xd = _post(xd, [o_nsa_d, o_fox_d], [woe_a, woe_b], g_mlp[0], wup[0], wdn[0], g_fin, False, td)
    mq_d, mkv_d = _proj(xd, g_mix[1], wn_o, wt_o, bias_o, nat_o, tr_o_dec, 1, td, td)
    s_moba = kv_out_dec(mkv_d, MOBA_KV)
    assert past % MOBA_BLOCK == 0 and MOBA_BLOCK == 2 * PAGE
    sl_m, t_m = _row_consts(n_tok, MOBA_KV, MOBA_R, MOBA_SLOPES, past)
    qbd_m = _block_diag_q(mq_d.reshape(nb_d, n_tok, mq_w), n_tok, MOBA_KV, MOBA_R)
    new_m = _new_pages(mkv_d[0], nb_d, n_tok)
    o_moba_bd = _stream("moba", qbd_m, sl_m, t_m, None, _pool_t(cache_moba_kv, 0), None, new_m, None,
                        page_table, past, pps_d // (MOBA_BLOCK // PAGE), MOBA_BLOCK // PAGE)
    o_moba_d = _diag_heads(o_moba_bd, n_tok, MOBA_KV, MOBA_R).astype(BF16)
    yd = _post(xd, [o_moba_d], [woo], g_mlp[1], wup[1], wdn[1], g_fin, True, td)
    y_sample = yd.reshape(nb_d, n_tok, D_MODEL)

    return (y_prompt, y_sample, p_cmp, s_cmp, p_slc, s_slc, p_win, s_win, p_fox, s_fox, p_lf, s_lf, p_moba, s_moba)
```

```python
import functools
import math

import numpy as np
import jax
import jax.numpy as jnp
from jax import lax
from jax.experimental import pallas as pl
from jax.experimental.pallas import tpu as pltpu

F32 = jnp.float32
BF16 = jnp.bfloat16
I32 = jnp.int32

D_MODEL = 1024
HEAD_DIM = 64
NSA_HEADS = 8
NSA_KV = 2
NSA_R = NSA_HEADS // NSA_KV
CMP_LEN = 32
CMP_STRIDE = 16
CMP_HIDDEN = 128
SLC_BLOCK = 64
SLC_TOPN = 16
WINDOW = 512
FOX_HEADS = 8
MOBA_HEADS = 16
MOBA_KV = 4
MOBA_R = MOBA_HEADS // MOBA_KV
MOBA_BLOCK = 256
MOBA_TOPK = 3
D_FF = 4 * D_MODEL
PAGE = 128
RMS_EPS = 1e-6
NEG = -1e30
SCALE = HEAD_DIM ** -0.5
LANES = 128
KEY_STEP = 512
VMEM_LIMIT = 56 * 1024 * 1024

NT_DIMS = (((1,), (1,)), ((), ()))


def _alibi(n):
    return [float(np.float32(2.0 ** (-8.0 * (i + 1) / n))) for i in range(n)]


NSA_SLOPES = _alibi(NSA_HEADS)
MOBA_SLOPES = _alibi(MOBA_HEADS)


def _div(x, n):
    assert n & (n - 1) == 0
    return jnp.right_shift(x, int(math.log2(n)))


def _mod(x, n):
    assert n & (n - 1) == 0
    return jnp.bitwise_and(x, n - 1)


def _dot(a, b):
    return jnp.dot(a, b, preferred_element_type=F32)


def _dot_nt(a, b):
    return lax.dot_general(a, b, NT_DIMS, preferred_element_type=F32)


def _split3(x):
    hi = x.astype(BF16)
    r1 = x - hi.astype(F32)
    mid = r1.astype(BF16)
    lo = (r1 - mid.astype(F32)).astype(BF16)
    return hi, mid, lo


def _dot01(x, mat, nt=False):
    n = x.shape[0]
    st = jnp.concatenate(_split3(x), axis=0)
    y = _dot_nt(st, mat) if nt else _dot(st, mat)
    return y[0:n] + y[n:2 * n] + y[2 * n:3 * n]


def _softmax_parts(s, valid):
    s = jnp.where(valid, s, NEG)
    m = jnp.max(s, axis=-1, keepdims=True)
    p = jnp.where(valid, jnp.exp(s - m), 0.0)
    l = jnp.maximum(jnp.sum(p, axis=-1, keepdims=True), 1e-30)
    return p, l, m


def _softmax_bias(logits):
    m = jnp.max(logits, axis=-1, keepdims=True)
    p = jnp.exp(logits - m)
    return p, jnp.sum(p, axis=-1, keepdims=True), m


def _topk_mask(v, nsel, nb):
    lane = lax.broadcasted_iota(I32, v.shape, 1)
    cnt = jnp.zeros(v.shape, I32)
    for k in range(nb):
        col = v[:, k:k + 1]
        beats = (col > v) | ((col == v) & (lane > k))
        cnt = cnt + jnp.where(beats, 1, 0)
    return (cnt < nsel) & (v > -jnp.inf)


def _dot01_rhs(mat, x):
    n = x.shape[1]
    y = _dot(mat, jnp.concatenate(_split3(x), axis=1))
    return y[:, 0:n] + y[:, n:2 * n] + y[:, 2 * n:3 * n]


def _bf16_parts(x):
    def rnd(v):
        u = np.float32(v).view(np.uint32)
        u = np.uint32((int(u) + 0x7FFF + ((int(u) >> 16) & 1)) & 0xFFFF0000)
        return float(u.view(np.float32))
    hi = rnd(x)
    mid = rnd(np.float32(np.float32(x) - np.float32(hi)))
    lo = rnd(np.float32(np.float32(x) - np.float32(hi) - np.float32(mid)))
    return hi, mid, lo


N_AUG = 6


def _aug_keys(k, pos):
    lane = lax.broadcasted_iota(I32, k.shape, 1)
    a = _div(pos, LANES).astype(F32)
    b = _mod(pos, LANES).astype(F32)
    aug = jnp.where(lane < 3, a, jnp.where(lane < N_AUG, b, 0.0)).astype(BF16)
    return jnp.concatenate([k, aug], axis=1)


def _aug_query(qt, slope):
    parts = _bf16_parts(slope)
    vals = [float(LANES) * p for p in parts] + list(parts)
    sub = lax.broadcasted_iota(I32, qt.shape, 0)
    aug = jnp.zeros(qt.shape, F32)
    for i, v in enumerate(vals):
        aug = jnp.where(sub == i, v, aug)
    return jnp.concatenate([qt, aug.astype(BF16)], axis=0)


def _softmax_cols(logits):
    m = jnp.max(logits, axis=0, keepdims=True)
    p = jnp.exp(logits - m)
    return p, jnp.sum(p, axis=0, keepdims=True), m


def _topk_rows(v, nsel, nb):
    row = lax.broadcasted_iota(I32, v.shape, 0)
    cnt = jnp.zeros(v.shape, I32)
    for k in range(nb):
        rk = v[k:k + 1, :]
        beats = (rk > v) | ((rk == v) & (row > k))
        cnt = cnt + jnp.where(beats, 1, 0)
    return (cnt < nsel) & (v > -jnp.inf)


def _stack_heads(q, g, nr):
    return jnp.concatenate([q[:, (g * nr + r) * HEAD_DIM:(g * nr + r + 1) * HEAD_DIM] for r in range(nr)], axis=0)


def _row_slopes(r4, slopes, g, nr):
    out = jnp.zeros(r4.shape, F32)
    for r in range(nr):
        out = jnp.where(r4 == r, slopes[g * nr + r], out)
    return out


_CHUNKED = {"bf16c": LANES, "bf16c2": MOBA_BLOCK}


def _proj_kernel(x_ref, g_ref, wn_ref, wt_ref, b_ref, *outs, nat, tr):
    x = x_ref[...]
    ms = jnp.mean(x * x, axis=-1, keepdims=True)
    h = (x * lax.rsqrt(ms + RMS_EPS)) * g_ref[...]
    hb = h.astype(BF16)
    yn = _dot(hb, wn_ref[...])
    yt = _dot_nt(wt_ref[...], hb)
    k = 0
    for (c0, w, kind) in nat:
        v = yn[:, c0:c0 + w]
        if kind == "q":
            outs[k][...] = (v * SCALE).astype(BF16)
        elif kind == "kb":
            outs[k][...] = v.astype(BF16)
        else:
            outs[k][...] = v
        k += 1
    for (r0, n, kind) in tr:
        v = yt[r0:r0 + n, :]
        if kind == "f32":
            outs[k][...] = v
        elif kind == "bf16":
            outs[k][...] = v.astype(BF16)
        elif kind == "qt":
            outs[k][...] = (v * SCALE).astype(BF16)
        elif kind in _CHUNKED:
            cw = _CHUNKED[kind]
            vb = v.astype(BF16)
            for c in range(v.shape[1] // cw):
                outs[k][c] = vb[:, c * cw:(c + 1) * cw]
        else:
            z = v + b_ref[...]
            outs[k][...] = jnp.minimum(z, 0.0) - jnp.log(1.0 + jnp.exp(-jnp.abs(z)))
        k += 1


def _proj(x2d, gain, wn, wt, bias, nat, tr, nb, s_len, tm):
    t_tot = x2d.shape[0]
    tpb = s_len // tm
    assert nb * s_len == t_tot
    out_shape, out_specs = [], []
    for (c0, w, kind) in nat:
        out_shape.append(jax.ShapeDtypeStruct((t_tot, w), BF16 if kind in ("q", "kb") else F32))
        out_specs.append(pl.BlockSpec((tm, w), lambda i: (i, 0)))
    for (r0, n, kind) in tr:
        if kind in _CHUNKED:
            cw = _CHUNKED[kind]
            out_shape.append(jax.ShapeDtypeStruct((nb, s_len // cw, n, cw), BF16))
            out_specs.append(pl.BlockSpec((None, tm // cw, n, cw), lambda i: (i // tpb, i % tpb, 0, 0)))
        else:
            out_shape.append(jax.ShapeDtypeStruct((nb, n, s_len), BF16 if kind in ("bf16", "qt") else F32))
            out_specs.append(pl.BlockSpec((None, n, tm), lambda i: (i // tpb, 0, i % tpb)))
    return pl.pallas_call(
        functools.partial(_proj_kernel, nat=tuple(nat), tr=tuple(tr)),
        grid=(t_tot // tm,),
        in_specs=[pl.BlockSpec((tm, D_MODEL), lambda i: (i, 0)),
                  pl.BlockSpec((1, D_MODEL), lambda i: (0, 0)),
                  pl.BlockSpec(wn.shape, lambda i: (0, 0)),
                  pl.BlockSpec(wt.shape, lambda i: (0, 0)),
                  pl.BlockSpec(bias.shape, lambda i: (0, 0))],
        out_specs=out_specs,
        out_shape=out_shape,
        compiler_params=pltpu.CompilerParams(dimension_semantics=("parallel",), vmem_limit_bytes=VMEM_LIMIT),
        name="norm_proj",
    )(x2d, gain, wn, wt, bias)


def _gelu_tanh(x):
    return 0.5 * x * (1.0 + jnp.tanh(math.sqrt(2.0 / math.pi) * (x + 0.044715 * (x * x * x))))


def _compress_kernel(*refs, npg, n_prefetch):
    refs = refs[n_prefetch:]
    pages = refs[:npg]
    pt_ref, pea_ref, peb_ref, w1a_ref, w1b_ref, w2_ref, out_ref, t_scr, carry = refs[npg:]
    sub = PAGE // CMP_STRIDE
    m = npg * sub

    @pl.when(pl.program_id(1) == 0)
    def _():
        carry[...] = jnp.zeros(carry.shape, F32)

    for k in range(npg):
        xb = pages[k][...].astype(BF16)
        tt = _dot_nt(pt_ref[...], xb)
        for p in range(CMP_STRIDE):
            t_scr[p, k * sub:(k + 1) * sub, :] = tt[p * sub:(p + 1) * sub, :]
    acc_a = jnp.zeros((m, w1a_ref.shape[2]), F32)
    acc_b = jnp.zeros((m, w1a_ref.shape[2]), F32)
    for p in range(CMP_STRIDE):
        tp = t_scr[p]
        acc_a = acc_a + _dot((tp + pea_ref[p]).astype(BF16), w1a_ref[p])
        acc_b = acc_b + _dot((tp + peb_ref[p]).astype(BF16), w1b_ref[p])
    rolled = pltpu.roll(acc_a, 1, axis=0)
    row = lax.broadcasted_iota(I32, acc_a.shape, 0)
    hid = jnp.where(row == 0, carry[...], rolled) + acc_b
    carry[...] = acc_a[m - 1:m, :]
    act = _gelu_tanh(hid)
    out_ref[...] = _dot(act.astype(BF16), w2_ref[...])


def _compress_weights(cmp_pe, cmp_w1, cmp_w2):
    nkg = 2 * NSA_KV
    eye = jnp.eye(nkg, dtype=F32)
    w1 = cmp_w1.reshape(2, CMP_LEN, HEAD_DIM, CMP_HIDDEN)
    w1 = jnp.repeat(w1, NSA_KV, axis=0)
    bd = jnp.einsum("kpdh,kl->pkdlh", w1, eye).reshape(CMP_LEN, nkg * HEAD_DIM, nkg * CMP_HIDDEN).astype(BF16)
    w2 = jnp.repeat(cmp_w2, NSA_KV, axis=0)
    w2bd = jnp.einsum("khd,kl->khld", w2, eye).reshape(nkg * CMP_HIDDEN, nkg * HEAD_DIM).astype(BF16)
    pe = jnp.repeat(cmp_pe, NSA_KV, axis=0)
    pe = pe.transpose(1, 0, 2).reshape(CMP_LEN, 1, nkg * HEAD_DIM)
    return bd[:CMP_STRIDE], bd[CMP_STRIDE:], w2bd, pe[:CMP_STRIDE], pe[CMP_STRIDE:]


def _perm_matrix():
    sub = PAGE // CMP_STRIDE
    pt = np.zeros((PAGE, PAGE), np.float32)
    for p in range(CMP_STRIDE):
        for j in range(sub):
            pt[p * sub + j, CMP_STRIDE * j + p] = 1.0
    return jnp.asarray(pt, BF16)


def _compress(pages_arr, page_table, cw, nb, n_pages, npg, paged):
    w1a, w1b, w2bd, pea, peb = cw
    rows = 2 * NSA_KV * HEAD_DIM
    m = npg * (PAGE // CMP_STRIDE)
    n_chunks = n_pages // npg
    page_specs = []
    for k in range(npg):
        if paged:
            page_specs.append(pl.BlockSpec((None, rows, PAGE), lambda b, c, pt, k=k: (pt[b, c * npg + k], 0, 0)))
        else:
            page_specs.append(pl.BlockSpec((None, rows, PAGE), lambda b, c, k=k: (b, 0, c * npg + k)))

    def const(shape):
        nd = len(shape)
        if paged:
            return pl.BlockSpec(shape, lambda b, c, pt: (0,) * nd)
        return pl.BlockSpec(shape, lambda b, c: (0,) * nd)

    perm = _perm_matrix()
    consts = [perm, pea, peb, w1a, w1b, w2bd]
    in_specs = page_specs + [const(a.shape) for a in consts]
    if paged:
        out_spec = pl.BlockSpec((None, m, rows), lambda b, c, pt: (b, c, 0))
    else:
        out_spec = pl.BlockSpec((None, m, rows), lambda b, c: (b, c, 0))
    n_prefetch = 1 if paged else 0
    grid_spec = pltpu.PrefetchScalarGridSpec(
        num_scalar_prefetch=n_prefetch, grid=(nb, n_chunks), in_specs=in_specs, out_specs=out_spec,
        scratch_shapes=[pltpu.VMEM((CMP_STRIDE, m, rows), F32), pltpu.VMEM((1, w1a.shape[2]), F32)])
    args = ([page_table] if paged else []) + [pages_arr] * npg + consts
    return pl.pallas_call(
        functools.partial(_compress_kernel, npg=npg, n_prefetch=n_prefetch),
        grid_spec=grid_spec,
        out_shape=jax.ShapeDtypeStruct((nb, n_chunks * m, rows), F32),
        compiler_params=pltpu.CompilerParams(dimension_semantics=("parallel", "arbitrary"),
                                             vmem_limit_bytes=VMEM_LIMIT),
        name="nsa_compress",
    )(*args)


def _cmp_to_slc(n_rows, n_c, n_s, n_cols):
    m = np.zeros((n_rows, n_cols), np.float32)
    c = np.arange(n_c)[:, None]
    j = np.arange(n_s)[None, :]
    lo = c * CMP_STRIDE
    hi = lo + CMP_LEN
    m[1:n_c + 1, :n_s] = ((lo < (j + 1) * SLC_BLOCK) & (hi > j * SLC_BLOCK)).astype(np.float32)
    return jnp.asarray(m, BF16)


def _cmp_branch(qg4, cmp_tok, mcs, g, tq, t4, slope4, n_c):
    ck = cmp_tok[:, g * HEAD_DIM:(g + 1) * HEAD_DIM].astype(BF16)
    v0 = (NSA_KV + g) * HEAD_DIM
    cv = cmp_tok[:, v0:v0 + HEAD_DIM].astype(BF16)
    sc = _dot_nt(qg4, ck)
    col = lax.broadcasted_iota(I32, sc.shape, 1)
    dc = t4 - ((col - 1) * CMP_STRIDE + (CMP_LEN - 1))
    valid = (dc >= 0) & (col >= 1) & (col <= n_c)
    p, l, _ = _softmax_parts(sc - slope4 * dc.astype(F32), valid)
    pn = p / l
    o = _dot(pn.astype(BF16), cv)
    psum = pn[0:tq]
    for r in range(1, NSA_R):
        psum = psum + pn[r * tq:(r + 1) * tq]
    return o, _dot01(psum, mcs)


def _slc_select(imp, t1, n_s):
    blk = lax.broadcasted_iota(I32, imp.shape, 1)
    tb = _div(t1, SLC_BLOCK)
    avail = (blk <= tb) & (blk < n_s)
    forced = (blk == 0) | (blk == tb) | (blk == tb - 1)
    v = jnp.where(forced, jnp.inf, jnp.where(avail, imp, -jnp.inf))
    return _topk_mask(v, min(SLC_TOPN, n_s), n_s)


def _nsa_prompt_kernel(qt_ref, glt_ref, cmp_ref, mcst_ref, sk_ref, wk_ref, sv_ref, wv_ref, o_ref,
                       ksaug, kwaug, qaug_scr, selt_scr, ocmp_scr, m_st, l_st, acc_st,
                       *, tq, s_len, n_c):
    qi = pl.program_id(1)
    s0 = qi * tq
    n_s = s_len // SLC_BLOCK
    kc_len = MOBA_BLOCK
    assert tq == LANES and kc_len == 2 * tq

    @pl.when(qi == 0)
    def _():
        pos = lax.broadcasted_iota(I32, (s_len, HEAD_DIM), 0)
        for g in range(NSA_KV):
            ksaug[g] = _aug_keys(sk_ref[:, g * HEAD_DIM:(g + 1) * HEAD_DIM], pos)
            kwaug[g] = _aug_keys(wk_ref[:, g * HEAD_DIM:(g + 1) * HEAD_DIM], pos)

    for g in range(NSA_KV):
        qaug_scr[g] = jnp.concatenate(
            [_aug_query(qt_ref[h * HEAD_DIM:(h + 1) * HEAD_DIM, :], NSA_SLOPES[h])
             for h in range(g * NSA_R, (g + 1) * NSA_R)], axis=1)
    tl = s0 + lax.broadcasted_iota(I32, (1, tq), 1)
    sig = 1.0 / (1.0 + jnp.exp(-glt_ref[...]))

    gw = NSA_R * tq

    def lanes(x, n):
        return jnp.concatenate([x] * n, axis=1)

    def scores_all(keys):
        return jnp.concatenate([_dot(keys[g], qaug_scr[g]) for g in range(NSA_KV)], axis=1)

    def pv_all(vts, pb):
        return jnp.concatenate([_dot(vts[g], pb[:, g * gw:(g + 1) * gw]) for g in range(NSA_KV)], axis=1)

    cmp_tok = cmp_ref[...]
    ncp = cmp_tok.shape[0]
    cmp_t = cmp_tok.T
    crow = lax.broadcasted_iota(I32, (ncp, tq), 0)
    c_end = (crow - 1) * CMP_STRIDE + (CMP_LEN - 1)
    cbias = lanes(jnp.where((c_end <= tl) & (crow >= 1) & (crow <= n_c), 0.0, NEG), NSA_HEADS)
    any_c = lanes(jnp.where(tl >= (CMP_LEN - 1), 1.0, 0.0), NSA_HEADS) > 0.5
    cpos = jnp.maximum((lax.broadcasted_iota(I32, (ncp, HEAD_DIM), 0) - 1) * CMP_STRIDE + (CMP_LEN - 1), 0)
    blk = lax.broadcasted_iota(I32, (n_s, tq), 0)
    tb = _div(tl, SLC_BLOCK)
    avail = blk <= tb
    forced = (blk == 0) | (blk == tb) | (blk == tb - 1)
    ck_augs = [_aug_keys(cmp_tok[:, g * HEAD_DIM:(g + 1) * HEAD_DIM].astype(BF16), cpos) for g in range(NSA_KV)]
    cvts = [cmp_t[(NSA_KV + g) * HEAD_DIM:(NSA_KV + g + 1) * HEAD_DIM, :].astype(BF16) for g in range(NSA_KV)]
    p, l, _ = _softmax_cols(scores_all(ck_augs) + cbias)
    pn = jnp.where(any_c, p / l, 0.0)
    ocmp_scr[...] = pv_all(cvts, pn.astype(BF16))
    for g in range(NSA_KV):
        psum = pn[:, g * gw:g * gw + tq]
        for r in range(1, NSA_R):
            psum = psum + pn[:, g * gw + r * tq:g * gw + (r + 1) * tq]
        imp = _dot01_rhs(mcst_ref[...], psum)[0:n_s]
        v = jnp.where(forced, jnp.inf, jnp.where(avail, imp, -jnp.inf))
        selt_scr[g] = jnp.where(_topk_rows(v, min(SLC_TOPN, n_s), n_s), 1.0, 0.0)

    m_st[...] = jnp.full(m_st.shape, NEG, F32)
    l_st[...] = jnp.zeros(l_st.shape, F32)
    acc_st[...] = jnp.zeros(acc_st.shape, F32)
    cd = _div(s0, kc_len)
    bpc = kc_len // SLC_BLOCK
    ksub = lax.broadcasted_iota(I32, (kc_len, tq), 0)
    diag_bias = jnp.where(cd * kc_len + ksub <= tl, 0.0, NEG)

    def slc_chunk(c, extra):
        biases = []
        for g in range(NSA_KV):
            rows = selt_scr[g, pl.ds(c * bpc, bpc), :]
            sb = jnp.where(rows > 0.5, 0.0, NEG)
            bias = jnp.concatenate([jnp.broadcast_to(sb[i:i + 1, :], (SLC_BLOCK, tq)) for i in range(bpc)], axis=0)
            if extra is not None:
                bias = bias + extra
            biases.append(lanes(bias, NSA_R))
        s = scores_all([ksaug[g, pl.ds(c * kc_len, kc_len), :] for g in range(NSA_KV)])
        s = s + jnp.concatenate(biases, axis=1)
        m_old = m_st[0:1, :]
        m_new = jnp.maximum(m_old, jnp.max(s, axis=0, keepdims=True))
        alpha = jnp.exp(m_old - m_new)
        p = jnp.exp(s - m_new)
        l_st[0:1, :] = alpha * l_st[0:1, :] + jnp.sum(p, axis=0, keepdims=True)
        vts = [sv_ref[c, (NSA_KV + g) * HEAD_DIM:(NSA_KV + g + 1) * HEAD_DIM, :] for g in range(NSA_KV)]
        acc_st[...] = alpha * acc_st[...] + pv_all(vts, p.astype(BF16))
        m_st[0:1, :] = m_new

    def slc_loop(c, carry):
        slc_chunk(c, None)
        return carry

    lax.fori_loop(0, cd, slc_loop, 0)
    slc_chunk(cd, diag_bias)
    o_slc = acc_st[...] / l_st[0:1, :]

    nwc = WINDOW // LANES + 1
    wsub = lax.broadcasted_iota(I32, (tq, tq), 0)
    wlane = lax.broadcasted_iota(I32, (tq, tq), 1)
    cidx = [qi - (nwc - 1) + j for j in range(nwc)]
    parts = []
    for j in range(nwc):
        if j == 0:
            base = jnp.where(wsub > wlane, 0.0, NEG)
        elif j == nwc - 1:
            base = jnp.where(wsub <= wlane, 0.0, NEG)
        else:
            base = jnp.zeros((tq, tq), F32)
        parts.append(base + jnp.where(cidx[j] >= 0, 0.0, NEG))
    wbias = lanes(jnp.concatenate(parts, axis=0), NSA_HEADS)
    cclamp = [jnp.maximum(c, 0) for c in cidx]
    kws = [jnp.concatenate([kwaug[g, pl.ds(c * tq, tq), :] for c in cclamp], axis=0) for g in range(NSA_KV)]
    vws = [jnp.concatenate([wv_ref[c, (NSA_KV + g) * HEAD_DIM:(NSA_KV + g + 1) * HEAD_DIM, :] for c in cclamp], axis=1)
           for g in range(NSA_KV)]
    p, l, _ = _softmax_cols(scores_all(kws) + wbias)
    o_win = pv_all(vws, p.astype(BF16)) / l

    gates = [jnp.concatenate([sig[3 * h + k:3 * h + k + 1, :] for h in range(NSA_HEADS)], axis=1)
             for k in range(3)]
    og = ocmp_scr[...] * gates[0] + o_slc * gates[1] + o_win * gates[2]
    o_ref[...] = jnp.concatenate([og[:, h * tq:(h + 1) * tq] for h in range(NSA_HEADS)], axis=0).T.astype(BF16)


def _nsa_prompt(nq_t, gl_t, cmp_tok, sk, wk, sv_c, wv_c, nb, s_len):
    tq = LANES
    nqt = s_len // tq
    n_c = s_len // CMP_STRIDE - CMP_LEN // CMP_STRIDE + 1
    ncp = cmp_tok.shape[1]
    mcst = jnp.transpose(_cmp_to_slc(ncp, n_c, s_len // SLC_BLOCK, LANES))
    rows = 2 * NSA_KV * HEAD_DIM
    width = NSA_HEADS * HEAD_DIM
    kw = NSA_KV * HEAD_DIM
    return pl.pallas_call(
        functools.partial(_nsa_prompt_kernel, tq=tq, s_len=s_len, n_c=n_c),
        grid=(nb, nqt),
        in_specs=[pl.BlockSpec((None, width, tq), lambda b, i: (b, 0, i)),
                  pl.BlockSpec((None, 3 * NSA_HEADS, tq), lambda b, i: (b, 0, i)),
                  pl.BlockSpec((None,) + cmp_tok.shape[1:], lambda b, i: (b, 0, 0)),
                  pl.BlockSpec(mcst.shape, lambda b, i: (0, 0)),
                  pl.BlockSpec((s_len, kw), lambda b, i: (b, 0)),
                  pl.BlockSpec((s_len, kw), lambda b, i: (b, 0)),
                  pl.BlockSpec((None, s_len // MOBA_BLOCK, rows, MOBA_BLOCK), lambda b, i: (b, 0, 0, 0)),
                  pl.BlockSpec((None, s_len // LANES, rows, LANES), lambda b, i: (b, 0, 0, 0))],
        out_specs=pl.BlockSpec((tq, width), lambda b, i: (b * nqt + i, 0)),
        out_shape=jax.ShapeDtypeStruct((nb * s_len, width), BF16),
        scratch_shapes=[pltpu.VMEM((NSA_KV, s_len, LANES), BF16), pltpu.VMEM((NSA_KV, s_len, LANES), BF16),
                        pltpu.VMEM((NSA_KV, LANES, NSA_R * tq), BF16),
                        pltpu.VMEM((NSA_KV, s_len // SLC_BLOCK, tq), F32),
                        pltpu.VMEM((HEAD_DIM, NSA_HEADS * tq), F32),
                        pltpu.VMEM((8, NSA_HEADS * tq), F32), pltpu.VMEM((8, NSA_HEADS * tq), F32),
                        pltpu.VMEM((HEAD_DIM, NSA_HEADS * tq), F32)],
        compiler_params=pltpu.CompilerParams(dimension_semantics=("parallel", "arbitrary"),
                                             vmem_limit_bytes=VMEM_LIMIT),
        name="nsa_prompt",
    )(nq_t, gl_t, cmp_tok, mcst, sk, wk, sv_c, wv_c)


def _tri_matrix():
    i = np.arange(LANES)
    return jnp.asarray((i[:, None] <= i[None, :]).astype(np.float32), BF16)


def _fox_prompt_kernel(q_ref, kv_ref, lf_ref, u_ref, o_ref, c_scr, *, tq, s_len, kstep):
    qi = pl.program_id(1)

    @pl.when(qi == 0)
    def _():
        carry = jnp.zeros((FOX_HEADS, 1), F32)
        for blk in range(s_len // LANES):
            cs = _dot01(lf_ref[:, blk * LANES:(blk + 1) * LANES], u_ref[...]) + carry
            c_scr[:, blk * LANES:(blk + 1) * LANES] = cs
            carry = cs[:, LANES - 1:LANES]

    s0 = qi * tq
    t = s0 + lax.broadcasted_iota(I32, (tq, 1), 0)
    nh = FOX_HEADS

    def body(kmax):
        pos = lax.broadcasted_iota(I32, (tq, kmax), 1)
        causal = jnp.where(pos <= t, 0.0, NEG)
        outs = []
        for h in range(nh):
            qh = q_ref[:, h * HEAD_DIM:(h + 1) * HEAD_DIM]
            kt = kv_ref[h * HEAD_DIM:(h + 1) * HEAD_DIM, 0:kmax]
            vt = kv_ref[(nh + h) * HEAD_DIM:(nh + h + 1) * HEAD_DIM, 0:kmax]
            p, l, _ = _softmax_bias(_dot(qh, kt) + (causal - c_scr[h:h + 1, 0:kmax]))
            outs.append(_dot_nt(p.astype(BF16), vt) / l)
        o_ref[...] = jnp.concatenate(outs, axis=1).astype(BF16)

    for c in range(s_len // kstep):
        pl.when(_div(s0, kstep) == c)(functools.partial(body, kstep * (c + 1)))


def _fox_prompt(fq, fkv_b, lf_t, nb, s_len):
    tq = LANES
    nqt = s_len // tq
    rows = 2 * FOX_HEADS * HEAD_DIM
    tri = _tri_matrix()
    return pl.pallas_call(
        functools.partial(_fox_prompt_kernel, tq=tq, s_len=s_len, kstep=KEY_STEP),
        grid=(nb, nqt),
        in_specs=[pl.BlockSpec((tq, FOX_HEADS * HEAD_DIM), lambda b, i: (b * nqt + i, 0)),
                  pl.BlockSpec((None, rows, s_len), lambda b, i: (b, 0, 0)),
                  pl.BlockSpec((None, FOX_HEADS, s_len), lambda b, i: (b, 0, 0)),
                  pl.BlockSpec(tri.shape, lambda b, i: (0, 0))],
        out_specs=pl.BlockSpec((tq, FOX_HEADS * HEAD_DIM), lambda b, i: (b * nqt + i, 0)),
        out_shape=jax.ShapeDtypeStruct((nb * s_len, FOX_HEADS * HEAD_DIM), BF16),
        scratch_shapes=[pltpu.VMEM((FOX_HEADS, s_len), F32)],
        compiler_params=pltpu.CompilerParams(dimension_semantics=("parallel", "arbitrary"),
                                             vmem_limit_bytes=VMEM_LIMIT),
        name="fox_prompt",
    )(fq, fkv_b, lf_t, tri)


def _moba_prompt_kernel(qt_ref, k_ref, kv_ref, eavg_ref, o_ref, kaug, km_scr, qaug_scr, m_scr, l_scr, o_scr,
                        *, tq, s_len):
    qi = pl.program_id(1)
    n_b = s_len // MOBA_BLOCK
    nbp = m_scr.shape[1]
    assert tq == LANES and MOBA_BLOCK == 2 * tq

    @pl.when(qi == 0)
    def _():
        pos = lax.broadcasted_iota(I32, (s_len, HEAD_DIM), 0)
        for g in range(MOBA_KV):
            kaug[g] = _aug_keys(k_ref[:, g * HEAD_DIM:(g + 1) * HEAD_DIM], pos)
        km_scr[...] = _dot(eavg_ref[...], k_ref[...]).astype(BF16)
        m_scr[...] = jnp.zeros(m_scr.shape, F32)
        l_scr[...] = jnp.zeros(l_scr.shape, F32)
        o_scr[...] = jnp.zeros(o_scr.shape, F32)

    s0 = qi * tq
    tb = _div(s0, MOBA_BLOCK)
    tl = s0 + lax.broadcasted_iota(I32, (1, tq), 1)
    for g in range(MOBA_KV):
        qaug_scr[g] = jnp.concatenate(
            [_aug_query(qt_ref[h * HEAD_DIM:(h + 1) * HEAD_DIM, :], MOBA_SLOPES[h])
             for h in range(g * MOBA_R, (g + 1) * MOBA_R)], axis=1)

    def block_partial(j, bias):
        gw = MOBA_R * tq
        s = jnp.concatenate([_dot(kaug[g, pl.ds(j * MOBA_BLOCK, MOBA_BLOCK), :], qaug_scr[g])
                             for g in range(MOBA_KV)], axis=1)
        if bias is not None:
            s = s + jnp.concatenate([bias] * MOBA_KV, axis=1)
        p, l, m = _softmax_cols(s)
        pb = p.astype(BF16)
        res = []
        for g in range(MOBA_KV):
            vtc = kv_ref[j, (MOBA_KV + g) * HEAD_DIM:(MOBA_KV + g + 1) * HEAD_DIM, :]
            sl = slice(g * gw, (g + 1) * gw)
            res.append((m[:, sl], l[:, sl], _dot(vtc, pb[:, sl])))
        return res

    def past_block(j, carry):
        for g, (m, l, o) in enumerate(block_partial(j, None)):
            m_scr[g, pl.ds(j, 1), :] = m
            l_scr[g, pl.ds(j, 1), :] = l
            o_scr[g, j] = o
        return carry

    lax.fori_loop(0, tb, past_block, 0)
    ksub = lax.broadcasted_iota(I32, (MOBA_BLOCK, tq), 0)
    own_bias = jnp.where(tb * MOBA_BLOCK + ksub <= tl, 0.0, NEG)
    own = block_partial(tb, jnp.concatenate([own_bias] * MOBA_R, axis=1))

    row = lax.broadcasted_iota(I32, (nbp, MOBA_R * tq), 0)
    outs = []
    for g in range(MOBA_KV):
        m_o, l_o, o_o = own[g]
        gate = _dot(km_scr[:, g * HEAD_DIM:(g + 1) * HEAD_DIM], qaug_scr[g, 0:HEAD_DIM, :])
        sel = _topk_rows(jnp.where(row < tb, gate, -jnp.inf), min(MOBA_TOPK, n_b), n_b)
        mm = m_scr[g]
        mx = jnp.maximum(jnp.max(jnp.where(sel, mm, NEG), axis=0, keepdims=True), m_o)
        w = jnp.where(sel, jnp.exp(mm - mx), 0.0)
        w_o = jnp.exp(m_o - mx)
        den = jnp.sum(w * l_scr[g], axis=0, keepdims=True) + w_o * l_o
        num = w_o * o_o
        for j in range(n_b - 1):
            num = num + w[j:j + 1, :] * o_scr[g, j]
        og = num / den
        outs += [og[:, r * tq:(r + 1) * tq] for r in range(MOBA_R)]
    o_ref[...] = jnp.concatenate(outs, axis=0).T.astype(BF16)


def _moba_prompt(mq_t, mk, mkv_c, nb, s_len):
    tq = LANES
    nqt = s_len // tq
    rows = 2 * MOBA_KV * HEAD_DIM
    width = MOBA_HEADS * HEAD_DIM
    kw = MOBA_KV * HEAD_DIM
    n_b = s_len // MOBA_BLOCK
    nbp = 16
    assert n_b <= nbp
    e = np.zeros((nbp, s_len), np.float32)
    e[np.arange(s_len) // MOBA_BLOCK, np.arange(s_len)] = 1.0 / MOBA_BLOCK
    eavg = jnp.asarray(e, BF16)
    return pl.pallas_call(
        functools.partial(_moba_prompt_kernel, tq=tq, s_len=s_len),
        grid=(nb, nqt),
        in_specs=[pl.BlockSpec((None, width, tq), lambda b, i: (b, 0, i)),
                  pl.BlockSpec((s_len, kw), lambda b, i: (b, 0)),
                  pl.BlockSpec((None, n_b, rows, MOBA_BLOCK), lambda b, i: (b, 0, 0, 0)),
                  pl.BlockSpec(eavg.shape, lambda b, i: (0, 0))],
        out_specs=pl.BlockSpec((tq, width), lambda b, i: (b * nqt + i, 0)),
        out_shape=jax.ShapeDtypeStruct((nb * s_len, width), BF16),
        scratch_shapes=[pltpu.VMEM((MOBA_KV, s_len, LANES), BF16),
                        pltpu.VMEM((nbp, kw), BF16),
                        pltpu.VMEM((MOBA_KV, LANES, MOBA_R * tq), BF16),
                        pltpu.VMEM((MOBA_KV, nbp, MOBA_R * tq), F32), pltpu.VMEM((MOBA_KV, nbp, MOBA_R * tq), F32),
                        pltpu.VMEM((MOBA_KV, n_b, HEAD_DIM, MOBA_R * tq), F32)],
        compiler_params=pltpu.CompilerParams(dimension_semantics=("parallel", "arbitrary"),
                                             vmem_limit_bytes=VMEM_LIMIT),
        name="moba_prompt",
    )(mq_t, mk, mkv_c, eavg)


def _post_kernel(*refs, n_o, final):
    x_ref = refs[0]
    o_refs = refs[1:1 + n_o]
    wo_refs = refs[1 + n_o:1 + 2 * n_o]
    g_ref, wup_ref, wdn_ref, gf_ref, out_ref, x1_scr, h_scr, acc_scr = refs[1 + 2 * n_o:]
    j = pl.program_id(1)

    @pl.when(j == 0)
    def _():
        x1 = x_ref[...]
        for o_ref, wo_ref in zip(o_refs, wo_refs):
            x1 = x1 + _dot(o_ref[...], wo_ref[...])
        x1_scr[...] = x1
        ms = jnp.mean(x1 * x1, axis=-1, keepdims=True)
        h_scr[...] = ((x1 * lax.rsqrt(ms + RMS_EPS)) * g_ref[...]).astype(BF16)
        acc_scr[...] = jnp.zeros(acc_scr.shape, F32)

    u = jnp.maximum(_dot(h_scr[...], wup_ref[...]), 0.0)
    acc_scr[...] += _dot((u * u).astype(BF16), wdn_ref[...])

    @pl.when(j == pl.num_programs(1) - 1)
    def _():
        y = x1_scr[...] + acc_scr[...]
        if final:
            ms = jnp.mean(y * y, axis=-1, keepdims=True)
            y = (y * lax.rsqrt(ms + RMS_EPS)) * gf_ref[...]
        out_ref[...] = y


def _post(x2d, o_list, wo_list, g_mlp, w_up, w_down, g_final, final, tm, tf=1024):
    t_tot = x2d.shape[0]
    n_o = len(o_list)
    in_specs = [pl.BlockSpec((tm, D_MODEL), lambda i, j: (i, 0))]
    in_specs += [pl.BlockSpec((tm, o.shape[1]), lambda i, j: (i, 0)) for o in o_list]
    in_specs += [pl.BlockSpec(w.shape, lambda i, j: (0, 0)) for w in wo_list]
    in_specs += [pl.BlockSpec((1, D_MODEL), lambda i, j: (0, 0)),
                 pl.BlockSpec((D_MODEL, tf), lambda i, j: (0, j)),
                 pl.BlockSpec((tf, D_MODEL), lambda i, j: (j, 0)),
                 pl.BlockSpec((1, D_MODEL), lambda i, j: (0, 0))]
    return pl.pallas_call(
        functools.partial(_post_kernel, n_o=n_o, final=final),
        grid=(t_tot // tm, D_FF // tf),
        in_specs=in_specs,
        out_specs=pl.BlockSpec((tm, D_MODEL), lambda i, j: (i, 0)),
        out_shape=jax.ShapeDtypeStruct((t_tot, D_MODEL), F32),
        scratch_shapes=[pltpu.VMEM((tm, D_MODEL), F32), pltpu.VMEM((tm, D_MODEL), BF16),
                        pltpu.VMEM((tm, D_MODEL), F32)],
        compiler_params=pltpu.CompilerParams(dimension_semantics=("parallel", "arbitrary"),
                                             vmem_limit_bytes=VMEM_LIMIT),
        name="post_mlp",
    )(x2d, *o_list, *wo_list, g_mlp, w_up, w_down, g_final)


def _nsa_dec_kernel(q_ref, cmp_ref, mcs_ref, kw_ref, ocmp_ref, owin_ref, sel_ref, *, tq, q0, n_c, n_s):
    q = q_ref[...]
    cmp_tok = cmp_ref[...]
    mcs = mcs_ref[...]
    row4 = lax.broadcasted_iota(I32, (NSA_R * tq, 1), 0)
    t4 = q0 + _mod(row4, tq)
    r4 = _div(row4, tq)
    t1 = q0 + lax.broadcasted_iota(I32, (tq, 1), 0)
    nbp = mcs.shape[1]
    wlen = kw_ref.shape[1]
    for g in range(NSA_KV):
        qg4 = _stack_heads(q, g, NSA_R)
        slope4 = _row_slopes(r4, NSA_SLOPES, g, NSA_R)
        o_cmp, imp = _cmp_branch(qg4, cmp_tok, mcs, g, tq, t4, slope4, n_c)
        sel = _slc_select(imp, t1, n_s)
        sel_ref[:, g * nbp:(g + 1) * nbp] = jnp.where(sel, 1.0, 0.0)
        kwin = kw_ref[g * HEAD_DIM:(g + 1) * HEAD_DIM, :].astype(BF16)
        vwin = kw_ref[(NSA_KV + g) * HEAD_DIM:(NSA_KV + g + 1) * HEAD_DIM, :].astype(BF16)
        wp = (q0 - WINDOW) + lax.broadcasted_iota(I32, (NSA_R * tq, wlen), 1)
        dw = t4 - wp
        valid = (wp >= 0) & (dw >= 0) & (dw < WINDOW)
        p, l, _ = _softmax_parts(_dot(qg4, kwin) - slope4 * dw.astype(F32), valid)
        o_win = _dot_nt(p.astype(BF16), vwin) / l
        for r in range(NSA_R):
            h = g * NSA_R + r
            ocmp_ref[:, h * HEAD_DIM:(h + 1) * HEAD_DIM] = o_cmp[r * tq:(r + 1) * tq]
            owin_ref[:, h * HEAD_DIM:(h + 1) * HEAD_DIM] = o_win[r * tq:(r + 1) * tq]


def _nsa_dec(q8, cmp_tok, kwin_t, nb, q0, n_c, n_s):
    tq = q8.shape[1]
    nbp = -(-n_s // LANES) * LANES
    mcs = _cmp_to_slc(cmp_tok.shape[1], n_c, n_s, nbp)
    width = NSA_HEADS * HEAD_DIM
    return pl.pallas_call(
        functools.partial(_nsa_dec_kernel, tq=tq, q0=q0, n_c=n_c, n_s=n_s),
        grid=(nb,),
        in_specs=[pl.BlockSpec((None, tq, width), lambda b: (b, 0, 0)),
                  pl.BlockSpec((None,) + cmp_tok.shape[1:], lambda b: (b, 0, 0)),
                  pl.BlockSpec(mcs.shape, lambda b: (0, 0)),
                  pl.BlockSpec((None,) + kwin_t.shape[1:], lambda b: (b, 0, 0))],
        out_specs=[pl.BlockSpec((None, tq, width), lambda b: (b, 0, 0)),
                   pl.BlockSpec((None, tq, width), lambda b: (b, 0, 0)),
                   pl.BlockSpec((None, tq, NSA_KV * nbp), lambda b: (b, 0, 0))],
        out_shape=[jax.ShapeDtypeStruct((nb, tq, width), F32),
                   jax.ShapeDtypeStruct((nb, tq, width), F32),
                   jax.ShapeDtypeStruct((nb, tq, NSA_KV * nbp), F32)],
        compiler_params=pltpu.CompilerParams(dimension_semantics=("parallel",), vmem_limit_bytes=VMEM_LIMIT),
        name="nsa_decode_cmp_win",
    )(q8, cmp_tok, mcs, kwin_t)


def _stream_kernel(*refs, mode, nseg, ppseg, n_rows, hk, q0, n_segs):
    pps = nseg * ppseg
    refs = refs[1:]
    qbd_ref, slope_ref, t_ref = refs[0:3]
    k = 3
    sel_ref = selnew_ref = em_ref = None
    if mode == "slc":
        sel_ref, selnew_ref, em_ref = refs[k:k + 3]
        k += 3
    pages = refs[k:k + pps]
    k += pps
    lf_pages = None
    if mode == "fox":
        lf_pages = refs[k:k + pps]
        k += pps
    new_ref = refs[k]
    k += 1
    newlf_ref = tri_ref = None
    if mode == "fox":
        newlf_ref, tri_ref = refs[k], refs[k + 1]
        k += 2
    out_ref = refs[k]
    oparts, m_s, l_s, x_s = refs[k + 1:k + 5]
    st = pl.program_id(1)
    nsteps = pl.num_programs(1)
    qbd = qbd_ref[...]
    slope = slope_ref[...]
    tcol = t_ref[...]
    lane = lax.broadcasted_iota(I32, (n_rows, LANES), 1)
    seg_lane = lax.broadcasted_iota(I32, m_s.shape, 1)
    reps = n_rows // FOX_HEADS

    @pl.when(st == 0)
    def _():
        m_s[...] = jnp.zeros(m_s.shape, F32)
        l_s[...] = jnp.zeros(l_s.shape, F32)
        x_s[...] = jnp.zeros(x_s.shape, F32)

    def scores(page, page_pos0, lf_page, run):
        kt = page[0:hk, :].astype(BF16)
        vt = page[hk:2 * hk, :].astype(BF16)
        s_raw = _dot(qbd, kt)
        if mode == "fox":
            cs = _dot01(lf_page, tri_ref[...]) + run
            run = cs[:, LANES - 1:LANES]
            s = s_raw - jnp.concatenate([cs] * reps, axis=0)
        else:
            s = s_raw - slope * (tcol - (page_pos0 + lane)).astype(F32)
        return s_raw, s, vt, run

    def sel_mask(sref, width):
        mexp = _dot(sref[...].astype(BF16), em_ref[:, 0:width])
        return jnp.where(mexp > 0.5, 0.0, NEG)

    m_all, l_all, x_all = m_s[...], l_s[...], x_s[...]
    mb = sel_mask(sel_ref, pps * PAGE) if mode == "slc" else None
    seg_w = ppseg * PAGE
    lane_w = lax.broadcasted_iota(I32, (n_rows, seg_w), 1)
    for sg in range(nseg if mode != "fox" else 0):
        idx = range(sg * ppseg, (sg + 1) * ppseg)
        kt = jnp.concatenate([pages[i][0:hk, :].astype(BF16) for i in idx], axis=1)
        vt = jnp.concatenate([pages[i][hk:2 * hk, :].astype(BF16) for i in idx], axis=1)
        s_raw = _dot(qbd, kt)
        pos0 = (st * pps + sg * ppseg) * PAGE
        s = s_raw - slope * (tcol - (pos0 + lane_w)).astype(F32)
        if mode == "slc":
            s = s + jnp.concatenate([mb[:, sg * seg_w:(sg + 1) * seg_w]] * (n_rows // mb.shape[0]), axis=0)
        m = s.max(axis=-1, keepdims=True)
        p = jnp.exp(s - m)
        l = p.sum(axis=-1, keepdims=True)
        seg = st * nseg + sg
        oparts[seg] = _dot_nt(p.astype(BF16), vt)
        m_all = jnp.where(seg_lane == seg, m, m_all)
        l_all = jnp.where(seg_lane == seg, l, l_all)
        if mode == "moba":
            gsum = s_raw.sum(axis=-1, keepdims=True)
            x_all = jnp.where(seg_lane == seg, gsum * (1.0 / MOBA_BLOCK), x_all)
    for sg in range(nseg if mode == "fox" else 0):
        run = jnp.zeros((FOX_HEADS, 1), F32)
        ss, vts, raws = [], [], []
        for i in range(sg * ppseg, (sg + 1) * ppseg):
            s_raw, s, vt, run = scores(pages[i][...], (st * pps + i) * PAGE,
                                       lf_pages[i][...] if mode == "fox" else None, run)
            if mode == "slc":
                s = s + jnp.concatenate([mb[:, i * PAGE:(i + 1) * PAGE]] * (n_rows // mb.shape[0]), axis=0)
            ss.append(s)
            vts.append(vt)
            raws.append(s_raw)
        smax = ss[0]
        for s in ss[1:]:
            smax = jnp.maximum(smax, s)
        m = smax.max(axis=-1, keepdims=True)
        psum = jnp.zeros((n_rows, LANES), F32)
        o = jnp.zeros((n_rows, hk), F32)
        for s, vt in zip(ss, vts):
            p = jnp.exp(s - m)
            psum = psum + p
            o = o + _dot_nt(p.astype(BF16), vt)
        l = psum.sum(axis=-1, keepdims=True)
        seg = st * nseg + sg
        oparts[seg] = o
        m_all = jnp.where(seg_lane == seg, m, m_all)
        l_all = jnp.where(seg_lane == seg, l, l_all)
        if mode == "fox":
            x_all = jnp.where(seg_lane == seg, jnp.concatenate([run] * reps, axis=0), x_all)
        elif mode == "moba":
            rsum = raws[0]
            for r_ in raws[1:]:
                rsum = rsum + r_
            gsum = rsum.sum(axis=-1, keepdims=True)
            x_all = jnp.where(seg_lane == seg, gsum * (1.0 / MOBA_BLOCK), x_all)
    m_s[...] = m_all
    l_s[...] = l_all
    x_s[...] = x_all

    @pl.when(st == nsteps - 1)
    def _():
        last = n_segs - 1
        _, s, vt, _ = scores(new_ref[...], q0, newlf_ref[...] if mode == "fox" else None,
                             jnp.zeros((FOX_HEADS, 1), F32))
        if mode == "slc":
            mnew = sel_mask(selnew_ref, PAGE)
            s = s + jnp.concatenate([mnew] * (n_rows // mnew.shape[0]), axis=0)
        p, l, m = _softmax_parts(s, ((q0 + lane) <= tcol) & (s > 0.5 * NEG))
        oparts[last] = _dot_nt(p.astype(BF16), vt)
        mm = jnp.where(seg_lane == last, m, m_all)
        ll = jnp.where(seg_lane == last, l, l_all)
        if mode == "fox":
            mm = mm + _dot01(x_all, tri_ref[...], nt=True)
            valid = seg_lane < n_segs
        elif mode == "slc":
            valid = seg_lane < n_segs
        else:
            n_b = n_segs - 1
            gate = jnp.where(seg_lane < n_b, x_all, -jnp.inf)
            valid = _topk_mask(gate, min(MOBA_TOPK, n_b), n_b) | (seg_lane == last)
        mx = jnp.max(jnp.where(valid, mm, NEG), axis=-1, keepdims=True)
        w = jnp.where(valid, jnp.exp(mm - mx), 0.0)
        den = jnp.maximum(jnp.sum(w * ll, axis=-1, keepdims=True), 1e-30)
        num = jnp.zeros((n_rows, hk), F32)
        for seg in range(n_segs):
            num = num + w[:, seg:seg + 1] * oparts[seg]
        out_ref[...] = num / den


def _stream(mode, qbd, slope_col, t_col, sel_steps, pool_t, lf_pool_t, new_t, newlf_t, page_table, q0, nseg, ppseg):
    nb, n_rows, hk = qbd.shape
    n_pages = page_table.shape[1]
    pps = nseg * ppseg
    nsteps = n_pages // pps
    n_segs = nsteps * nseg + 1
    segp = LANES
    assert n_segs <= segp and n_pages % pps == 0

    def cst(shape):
        nd = len(shape)
        return pl.BlockSpec(shape, lambda b, s, pt: (0,) * nd)

    def per_b(shape):
        nd = len(shape)
        return pl.BlockSpec((None,) + shape, lambda b, s, pt: (b,) + (0,) * nd)

    in_specs = [per_b((n_rows, hk)), cst((n_rows, 1)), cst((n_rows, 1))]
    args = [qbd, slope_col, t_col]
    if mode == "slc":
        nsel = sel_steps.shape[2]
        blocks_per_step = pps * (PAGE // SLC_BLOCK)
        assert blocks_per_step <= LANES and sel_steps.shape[1] == nsteps + 1
        em = np.zeros((LANES, pps * PAGE), np.float32)
        em[np.arange(pps * PAGE) // SLC_BLOCK, np.arange(pps * PAGE)] = 1.0
        em = jnp.asarray(em, BF16)
        in_specs += [pl.BlockSpec((None, None, nsel, LANES), lambda b, s, pt: (b, s, 0, 0)),
                     pl.BlockSpec((None, None, nsel, LANES), lambda b, s, pt: (b, nsteps, 0, 0)),
                     cst(em.shape)]
        args += [sel_steps, sel_steps, em]
    for i in range(pps):
        in_specs.append(pl.BlockSpec((None, 2 * hk, PAGE), lambda b, s, pt, i=i: (pt[b, s * pps + i], 0, 0)))
        args.append(pool_t)
    if mode == "fox":
        for i in range(pps):
            in_specs.append(pl.BlockSpec((None, FOX_HEADS, PAGE), lambda b, s, pt, i=i: (pt[b, s * pps + i], 0, 0)))
            args.append(lf_pool_t)
    in_specs.append(per_b((2 * hk, PAGE)))
    args.append(new_t)
    if mode == "fox":
        tri = _tri_matrix()
        in_specs += [per_b((FOX_HEADS, PAGE)), cst(tri.shape)]
        args += [newlf_t, tri]
    grid_spec = pltpu.PrefetchScalarGridSpec(
        num_scalar_prefetch=1, grid=(nb, nsteps), in_specs=in_specs,
        out_specs=pl.BlockSpec((None, n_rows, hk), lambda b, s, pt: (b, 0, 0)),
        scratch_shapes=[pltpu.VMEM((n_segs, n_rows, hk), F32), pltpu.VMEM((n_rows, segp), F32),
                        pltpu.VMEM((n_rows, segp), F32), pltpu.VMEM((n_rows, segp), F32)])
    return pl.pallas_call(
        functools.partial(_stream_kernel, mode=mode, nseg=nseg, ppseg=ppseg, n_rows=n_rows, hk=hk, q0=q0,
                          n_segs=n_segs),
        grid_spec=grid_spec,
        out_shape=jax.ShapeDtypeStruct((nb, n_rows, hk), F32),
        compiler_params=pltpu.CompilerParams(dimension_semantics=("parallel", "arbitrary"),
                                             vmem_limit_bytes=VMEM_LIMIT),
        name="decode_stream_" + mode,
    )(page_table, *args)


def _nsa_gate_kernel(oc_ref, os_ref, ow_ref, gl_ref, o_ref):
    sig = 1.0 / (1.0 + jnp.exp(-gl_ref[...]))
    for h in range(NSA_HEADS):
        sl = slice(h * HEAD_DIM, (h + 1) * HEAD_DIM)
        o_ref[:, sl] = (oc_ref[:, sl] * sig[:, 3 * h:3 * h + 1] + os_ref[:, sl] * sig[:, 3 * h + 1:3 * h + 2]
                        + ow_ref[:, sl] * sig[:, 3 * h + 2:3 * h + 3]).astype(BF16)


def _nsa_gate(oc, os_, ow, gl):
    return pl.pallas_call(
        _nsa_gate_kernel,
        out_shape=jax.ShapeDtypeStruct(oc.shape, BF16),
        name="nsa_gate",
    )(oc, os_, ow, gl)


def _pool_t(cache, li):
    c = jnp.transpose(cache[li], (0, 2, 3, 4, 1))
    return c.reshape(c.shape[0], -1, c.shape[-1])


def _block_diag_q(q, n_tok, n_kv, n_r, r_major=False):
    b = q.shape[0]
    q5 = q.reshape(b, n_tok, n_kv, n_r, 1, HEAD_DIM)
    eye = jnp.eye(n_kv, dtype=q.dtype).reshape(1, 1, n_kv, 1, n_kv, 1)
    x = (q5 * eye).reshape(b, n_tok, n_kv, n_r, n_kv * HEAD_DIM)
    if r_major:
        x = x.transpose(0, 3, 1, 2, 4)
    return x.reshape(b, n_tok * n_kv * n_r, n_kv * HEAD_DIM)


def _diag_heads(o, n_tok, n_kv, n_r, r_major=False):
    b = o.shape[0]
    if r_major:
        o6 = o.reshape(b, n_r, n_tok, n_kv, n_kv, HEAD_DIM).transpose(0, 2, 3, 1, 4, 5)
    else:
        o6 = o.reshape(b, n_tok, n_kv, n_r, n_kv, HEAD_DIM)
    d = jnp.einsum("btgrgd->btgrd", o6)
    return d.reshape(b * n_tok, n_kv * n_r * HEAD_DIM)


def _row_consts(n_tok, n_kv, n_r, slopes, q0, r_major=False):
    t, g, r = np.meshgrid(np.arange(n_tok), np.arange(n_kv), np.arange(n_r), indexing="ij")
    h = g * n_r + r
    if r_major:
        t, h = t.transpose(2, 0, 1), h.transpose(2, 0, 1)
    t, h = t.reshape(-1), h.reshape(-1)
    sl = np.asarray(slopes, np.float32)[h] if slopes is not None else np.zeros(h.shape, np.float32)
    return jnp.asarray(sl.reshape(-1, 1), F32), jnp.asarray((q0 + t).reshape(-1, 1), I32)


def _new_pages(kv_t, nb, n_tok):
    rows = kv_t.shape[0]
    x = kv_t.reshape(rows, nb, n_tok).transpose(1, 0, 2)
    return jnp.pad(x, ((0, 0), (0, 0), (0, PAGE - n_tok)))


def kernel(x_prompt, x_sample, cache_nsa_cmp_kv, cache_nsa_slc_kv, state_nsa_win_kv, cache_fox_kv, cache_fox_lf,
           cache_moba_kv, page_table, norm_mix, norm_mlp, w_in_even, b_fgt, w_out_even, cmp_pe, cmp_w1, cmp_w2,
           w_in_odd, w_out_odd, w_up, w_down, norm_final):
    nb_p, s_len, _ = x_prompt.shape
    nb_d, n_tok, _ = x_sample.shape
    n_pages = page_table.shape[1]
    past = n_pages * PAGE
    nq_w = NSA_HEADS * HEAD_DIM
    nkv_w = 2 * NSA_KV * HEAD_DIM
    fq_w = FOX_HEADS * HEAD_DIM
    fkv_w = 2 * FOX_HEADS * HEAD_DIM
    mq_w = MOBA_HEADS * HEAD_DIM
    mkv_w = 2 * MOBA_KV * HEAD_DIM
    ngl = 3 * NSA_HEADS

    wte = jnp.transpose(w_in_even[0])
    o_nq, o_ckv, o_skv, o_wkv = 0, nq_w, nq_w + nkv_w, nq_w + 2 * nkv_w
    o_gl = nq_w + 3 * nkv_w
    o_fq = o_gl + ngl
    o_fkv = o_fq + fq_w
    o_fl = o_fkv + fkv_w
    wn_e = jnp.concatenate([wte[o_nq:o_nq + nq_w], wte[o_fq:o_fq + fq_w], wte[o_gl:o_gl + ngl],
                            jnp.zeros((LANES - ngl, D_MODEL), F32)], axis=0)
    wn_e = jnp.transpose(wn_e).astype(BF16)
    wt_e = jnp.concatenate([wte[o_ckv:o_ckv + 3 * nkv_w], wte[o_fkv:o_fkv + fkv_w], wte[o_fl:o_fl + FOX_HEADS]],
                           axis=0).astype(BF16)
    bias_e = b_fgt[0].reshape(FOX_HEADS, 1)
    nat_e = [(0, nq_w, "q"), (nq_w, fq_w, "q"), (nq_w + fq_w, LANES, "f32")]
    r_skv, r_wkv, r_fkv, r_fl = nkv_w, 2 * nkv_w, 3 * nkv_w, 3 * nkv_w + fkv_w
    tr_e_dec = [(0, nkv_w, "f32"), (r_skv, nkv_w, "f32"), (r_wkv, nkv_w, "f32"), (r_fkv, fkv_w, "f32"),
                (r_fl, FOX_HEADS, "lf")]
    wo_t = jnp.transpose(w_in_odd[0])
    wn_o = w_in_odd[0][:, :mq_w].astype(BF16)
    wt_o = wo_t[mq_w:].astype(BF16)
    bias_o = jnp.zeros((8, 1), F32)
    nat_o = [(0, mq_w, "q")]
    tr_o_dec = [(0, mkv_w, "f32")]
    hk_n = NSA_KV * HEAD_DIM
    wn_ep = jnp.transpose(jnp.concatenate([wte[o_fq:o_fq + fq_w], wte[o_skv:o_skv + hk_n],
                                           wte[o_wkv:o_wkv + hk_n]], axis=0)).astype(BF16)
    wt_ep = jnp.concatenate([wte[o_ckv:o_ckv + 3 * nkv_w], wte[o_fkv:o_fkv + fkv_w], wte[o_fl:o_fl + FOX_HEADS],
                             wte[o_nq:o_nq + nq_w], wte[o_gl:o_gl + ngl]], axis=0).astype(BF16)
    nat_ep = [(0, fq_w, "q"), (fq_w, hk_n, "kb"), (fq_w + hk_n, hk_n, "kb")]
    r_nq = r_fl + FOX_HEADS
    r_gl = r_nq + nq_w
    tr_e_prompt = [(0, nkv_w, "f32"), (r_skv, nkv_w, "f32"), (r_skv, nkv_w, "bf16c2"), (r_wkv, nkv_w, "f32"),
                   (r_wkv, nkv_w, "bf16c"), (r_fkv, fkv_w, "f32"), (r_fkv, fkv_w, "bf16"), (r_fl, FOX_HEADS, "lf"),
                   (r_nq, nq_w, "qt"), (r_gl, ngl, "f32")]
    hk_m = MOBA_KV * HEAD_DIM
    wn_op = w_in_odd[0][:, mq_w:mq_w + hk_m].astype(BF16)
    wt_op = jnp.concatenate([wo_t[mq_w:], wo_t[:mq_w]], axis=0).astype(BF16)
    nat_op = [(0, hk_m, "kb")]
    tr_o_prompt = [(0, mkv_w, "f32"), (0, mkv_w, "bf16c2"), (mkv_w, mq_w, "qt")]
    woe = w_out_even[0].astype(BF16)
    woe_a, woe_b = woe[:nq_w], woe[nq_w:]
    woo = w_out_odd[0].astype(BF16)
    wup = w_up.astype(BF16)
    wdn = w_down.astype(BF16)
    g_mix = norm_mix.reshape(norm_mix.shape[0], 1, D_MODEL)
    g_mlp = norm_mlp.reshape(norm_mlp.shape[0], 1, D_MODEL)
    g_fin = norm_final.reshape(1, D_MODEL)
    cw = _compress_weights(cmp_pe[0], cmp_w1[0], cmp_w2[0])

    xp = x_prompt.reshape(nb_p * s_len, D_MODEL)
    tm_p = 512
    (fq, sk_n, wk_n, ckv_t, skv_t, skv_c, wkv_t, wkv_c, fkv_t, fkv_b, lf_t, nq_t, gl_t) = _proj(
        xp, g_mix[0], wn_ep, wt_ep, bias_e, nat_ep, tr_e_prompt, nb_p, s_len, tm_p)
    cmp_tok = _compress(ckv_t, None, cw, nb_p, s_len // PAGE, s_len // PAGE, paged=False)
    o_nsa = _nsa_prompt(nq_t, gl_t, cmp_tok, sk_n, wk_n, skv_c, wkv_c, nb_p, s_len)
    o_fox = _fox_prompt(fq, fkv_b, lf_t, nb_p, s_len)
    tm_post = tm_p
    xp = _post(xp, [o_nsa, o_fox], [woe_a, woe_b], g_mlp[0], wup[0], wdn[0], g_fin, False, tm_post)
    mk_n, mkv_t, mkv_c, mq_t = _proj(xp, g_mix[1], wn_op, wt_op, bias_o, nat_op, tr_o_prompt, nb_p, s_len, tm_p)
    o_moba = _moba_prompt(mq_t, mk_n, mkv_c, nb_p, s_len)
    yp = _post(xp, [o_moba], [woo], g_mlp[1], wup[1], wdn[1], g_fin, True, tm_post)
    y_prompt = yp.reshape(nb_p, s_len, D_MODEL)

    def kv_out(t, n_h):
        b, _, s = t.shape
        return jnp.transpose(t.reshape(b, 2, n_h, HEAD_DIM, s), (0, 4, 1, 2, 3))[None]

    p_cmp = kv_out(ckv_t, NSA_KV)
    p_slc = kv_out(skv_t, NSA_KV)
    wb = min(WINDOW, s_len)
    p_win = kv_out(wkv_t[:, :, s_len - wb:], NSA_KV)
    p_fox = kv_out(fkv_t, FOX_HEADS)
    p_lf = jnp.transpose(lf_t, (0, 2, 1))[None]
    p_moba = kv_out(mkv_t, MOBA_KV)

    td = nb_d * n_tok
    xd = x_sample.reshape(td, D_MODEL)
    (nq_d, fq_d, gl_d, ckv_d, skv_d, wkv_d, fkv_d, lf_d) = _proj(
        xd, g_mix[0], wn_e, wt_e, bias_e, nat_e, tr_e_dec, 1, td, td)

    def kv_out_dec(t, n_h):
        return jnp.transpose(t[0]).reshape(1, nb_d, n_tok, 2, n_h, HEAD_DIM)

    s_cmp = kv_out_dec(ckv_d, NSA_KV)
    s_slc = kv_out_dec(skv_d, NSA_KV)
    s_fox = kv_out_dec(fkv_d, FOX_HEADS)
    s_lf = jnp.transpose(lf_d[0]).reshape(1, nb_d, n_tok, FOX_HEADS)
    win_state_t = jnp.transpose(state_nsa_win_kv[0], (0, 2, 3, 4, 1)).reshape(nb_d, nkv_w, -1)
    wkv_new = wkv_d[0].reshape(nkv_w, nb_d, n_tok).transpose(1, 0, 2)
    win_all = jnp.concatenate([win_state_t, wkv_new], axis=2)
    wbuf = win_state_t.shape[2]
    s_win = jnp.transpose(win_all[:, :, -wbuf:].reshape(nb_d, 2, NSA_KV, HEAD_DIM, wbuf), (0, 4, 1, 2, 3))[None]
    assert wbuf == WINDOW
    wpad = -(-(wbuf + n_tok) // LANES) * LANES
    kwin_t = jnp.pad(win_all, ((0, 0), (0, 0), (0, wpad - wbuf - n_tok)))

    assert (past + n_tok) // CMP_STRIDE == past // CMP_STRIDE
    cmp_pool = _pool_t(cache_nsa_cmp_kv, 0)
    cmp_tok_d = _compress(cmp_pool, page_table, cw, nb_d, n_pages, 32, paged=True)
    n_c = past // CMP_STRIDE - CMP_LEN // CMP_STRIDE + 1
    n_s = -(-(past + n_tok) // SLC_BLOCK)
    tq_d = 8
    q8 = jnp.pad(nq_d.reshape(nb_d, n_tok, nq_w), ((0, 0), (0, tq_d - n_tok), (0, 0)))
    o_cmp8, o_win8, sel8 = _nsa_dec(q8, cmp_tok_d, kwin_t, nb_d, past, n_c, n_s)
    o_cmp_d = o_cmp8[:, :n_tok].reshape(td, nq_w)
    o_win_d = o_win8[:, :n_tok].reshape(td, nq_w)
    nbp = sel8.shape[2] // NSA_KV
    pps_d = 16
    nsteps_d = n_pages // pps_d
    bps = pps_d * (PAGE // SLC_BLOCK)
    sel_tg = sel8[:, :n_tok].reshape(nb_d, n_tok * NSA_KV, nbp)[:, :, :n_s]
    sel_tg = jnp.pad(sel_tg, ((0, 0), (0, 0), (0, (nsteps_d + 1) * bps - n_s)))
    sel_steps = sel_tg.reshape(nb_d, n_tok * NSA_KV, nsteps_d + 1, bps).transpose(0, 2, 1, 3)
    sel_steps = jnp.pad(sel_steps, ((0, 0), (0, 0), (0, 0), (0, LANES - bps)))
    sl_nsa, t_nsa = _row_consts(n_tok, NSA_KV, NSA_R, NSA_SLOPES, past, r_major=True)
    qbd_s = _block_diag_q(nq_d.reshape(nb_d, n_tok, nq_w), n_tok, NSA_KV, NSA_R, r_major=True)
    new_s = _new_pages(skv_d[0], nb_d, n_tok)
    o_slc_bd = _stream("slc", qbd_s, sl_nsa, t_nsa, sel_steps, _pool_t(cache_nsa_slc_kv, 0), None, new_s, None,
                       page_table, past, 1, pps_d)
    o_slc_d = _diag_heads(o_slc_bd, n_tok, NSA_KV, NSA_R, r_major=True)
    o_nsa_d = _nsa_gate(o_cmp_d, o_slc_d, o_win_d, gl_d)
    sl_fox, t_fox = _row_consts(n_tok, FOX_HEADS, 1, None, past)
    qbd_f = _block_diag_q(fq_d.reshape(nb_d, n_tok, fq_w), n_tok, FOX_HEADS, 1)
    new_f = _new_pages(fkv_d[0], nb_d, n_tok)
    newlf = _new_pages(lf_d[0], nb_d, n_tok)
    lf_pool = jnp.transpose(cache_fox_lf[0], (0, 2, 1))
    o_fox_bd = _stream("fox", qbd_f, sl_fox, t_fox, None, _pool_t(cache_fox_kv, 0), lf_pool, new_f, newlf,
                       page_table, past, 1, pps_d)
    o_fox_d = _diag_heads(o_fox_bd, n_tok, FOX_HEADS, 1).astype(BF16)
    xd = _post(xd, [o_nsa_d, o_fox_d], [woe_a, woe_b], g_mlp[0], wup[0], wdn[0], g_fin, False, td)
    mq_d, mkv_d = _proj(xd, g_mix[1], wn_o, wt_o, bias_o, nat_o, tr_o_dec, 1, td, td)
    s_moba = kv_out_dec(mkv_d, MOBA_KV)
    assert past % MOBA_BLOCK == 0 and MOBA_BLOCK == 2 * PAGE
    sl_m, t_m = _row_consts(n_tok, MOBA_KV, MOBA_R, MOBA_SLOPES, past)
    qbd_m = _block_diag_q(mq_d.reshape(nb_d, n_tok, mq_w), n_tok, MOBA_KV, MOBA_R)
    new_m = _new_pages(mkv_d[0], nb_d, n_tok)
    o_moba_bd = _stream("moba", qbd_m, sl_m, t_m, None, _pool_t(cache_moba_kv, 0), None, new_m, None,
                        page_table, past, pps_d // (MOBA_BLOCK // PAGE), MOBA_BLOCK // PAGE)
    o_moba_d = _diag_heads(o_moba_bd, n_tok, MOBA_KV, MOBA_R).astype(BF16)
    yd = _post(xd, [o_moba_d], [woo], g_mlp[1], wup[1], wdn[1], g_fin, True, td)
    y_sample = yd.reshape(nb_d, n_tok, D_MODEL)

    return (y_prompt, y_sample, p_cmp, s_cmp, p_slc, s_slc, p_win, s_win, p_fox, s_fox, p_lf, s_lf, p_moba, s_moba)
```
